```python
import math
import jax, jax.numpy as jnp
from jax import lax
import numpy as np

D_MODEL = 2048
BATCH = 4
SEQ = 4096
DEPTH = 4
DEC_BATCH = 16
DEC_SEQ = 64
PAST_LEN = 1024

CHUNK = 64
N_META = 16
GROUP_W = D_MODEL // 4
MIX_W = 4 * GROUP_W
HEAD_DIM = 64
A_HEADS = GROUP_W // HEAD_DIM
A_KV_HEADS = 2
A_GROUP = A_HEADS // A_KV_HEADS
KV_W = A_KV_HEADS * HEAD_DIM
WINDOW = 128
WIN_CHUNKS = WINDOW // CHUNK
POOL_SIZES = (2, 4, 8, 16)
POOL_GW = GROUP_W // 4
POOL_STATE = 15
C_HEAD_DIM = 64
C_HEADS = GROUP_W // C_HEAD_DIM
C_GROUPS = 2
C_STATE = 128
C_CONV_CH = GROUP_W + 2 * C_GROUPS * C_STATE
CONV_W = 4
D_BLOCKS = 8
D_BLOCK_W = GROUP_W // D_BLOCKS
LRU_C = 8.0
PROJ_SIZES = (GROUP_W, KV_W, KV_W, GROUP_W, GROUP_W, C_CONV_CH, C_HEADS, GROUP_W, GROUP_W)
PROJ_W = sum(PROJ_SIZES)
N_EXPERTS = 64
TOP_K = 8
N_GROUP = 8
TOPK_GROUP = 4
D_EXPERT = D_MODEL // 4
ROUTE_SCALE = 2.5
MOE_BLOCK = 128
ALPHA = (2 * DEPTH) ** 0.25
BETA = (8 * DEPTH) ** -0.25
LN_EPS = 1e-5

kernel_name = 'hymba_style_streaming_hybrid_step'

F32 = jnp.float32


def layer_norm(x, g, b):
    xf = x.astype(F32)
    mu = xf.mean(-1, keepdims=True)
    var = jnp.square(xf - mu).mean(-1, keepdims=True)
    return ((xf - mu) * lax.rsqrt(var + LN_EPS) * g.astype(F32) + b.astype(F32)).astype(x.dtype)


def rms_norm(x, g):
    xf = x.astype(F32)
    return xf * lax.rsqrt(jnp.mean(xf * xf, -1, keepdims=True) + 1e-6) * g.astype(F32)


def sink_attention(q, k, v, mask, sinks):
    s = jnp.einsum('bnqkgd,bnskd->bnkgqs', q, k).astype(F32) * (HEAD_DIM ** -0.5)
    s = jnp.where(mask[None, :, None, None, None, :], s, -jnp.inf)
    sink = sinks.astype(F32).reshape(1, 1, A_KV_HEADS, A_GROUP, 1, 1)
    m = jnp.maximum(s.max(-1, keepdims=True), sink)
    p = jnp.exp(s - m)
    probs = p / (p.sum(-1, keepdims=True) + jnp.exp(sink - m))
    return jnp.einsum('bnkgqs,bnskd->bnqkgd', probs.astype(v.dtype), v)


def attn_prompt(q, k, v, sinks):
    b, l = q.shape[:2]
    t = l - N_META
    nc = t // CHUNK
    qm = q[:, :N_META].reshape(b, 1, N_META, A_KV_HEADS, A_GROUP, HEAD_DIM)
    y_meta = sink_attention(qm, k[:, None, :N_META], v[:, None, :N_META], jnp.ones((1, N_META), bool), sinks)
    qr = q[:, N_META:].reshape(b, nc, CHUNK, A_KV_HEADS, A_GROUP, HEAD_DIM)

    def band(r):
        rc = r[:, N_META:].reshape(b, nc, CHUNK, A_KV_HEADS, HEAD_DIM)
        rp = jnp.pad(rc, ((0, 0), (WIN_CHUNKS, 0), (0, 0), (0, 0), (0, 0)))
        blocks = [rp[:, o:o + nc] for o in range(WIN_CHUNKS + 1)]
        meta = jnp.broadcast_to(r[:, None, :N_META], (b, nc, N_META, A_KV_HEADS, HEAD_DIM))
        return jnp.concatenate([meta] + blocks, axis=2)

    valid = (jnp.arange(nc)[:, None] + jnp.arange(WIN_CHUNKS + 1)[None, :] - WIN_CHUNKS) >= 0
    mask = jnp.concatenate([jnp.ones((nc, N_META), bool), jnp.repeat(valid, CHUNK, axis=1)], axis=1)
    y_real = sink_attention(qr, band(k), band(v), mask, sinks)
    return jnp.concatenate([y_meta.reshape(b, N_META, GROUP_W), y_real.reshape(b, t, GROUP_W)], axis=1)


def attn_sample(q, k, v, meta_k, meta_v, win_k, win_v, sinks):
    b, l = q.shape[:2]
    qg = q.reshape(b, 1, l, A_KV_HEADS, A_GROUP, HEAD_DIM)
    keys = jnp.concatenate([meta_k.astype(k.dtype), win_k.astype(k.dtype), k], axis=1)[:, None]
    vals = jnp.concatenate([meta_v.astype(v.dtype), win_v.astype(v.dtype), v], axis=1)[:, None]
    y = sink_attention(qg, keys, vals, jnp.ones((1, keys.shape[2]), bool), sinks)
    return y.reshape(b, l, GROUP_W)


def pool_mixer(u, prev, n_prev, w, scale):
    b, l, c = u.shape
    up = jnp.concatenate([prev.astype(u.dtype), u], axis=1)
    upf = up.astype(F32)
    cs = jnp.concatenate([jnp.zeros((b, 1, c), F32), jnp.cumsum(upf, axis=1)], axis=1)
    pos = jnp.arange(l)
    outs = []
    for g, win in enumerate(POOL_SIZES):
        sl = slice(g * POOL_GW, (g + 1) * POOL_GW)
        tot = cs[:, POOL_STATE + 1:POOL_STATE + 1 + l, sl] - cs[:, POOL_STATE + 1 - win:POOL_STATE + 1 - win + l, sl]
        cnt = jnp.minimum(win, n_prev + pos + 1).astype(F32)
        outs.append(tot / cnt[None, :, None])
    d = (jnp.concatenate(outs, axis=-1) - u.astype(F32)).reshape(b, l, 4, POOL_GW)
    y = jnp.einsum('blgc,gce->blge', d, w.astype(F32)).reshape(b, l, c) * scale.astype(F32)
    return y, up[:, -POOL_STATE:]


def causal_conv(u, prev, w, bias):
    l = u.shape[1]
    up = jnp.concatenate([prev.astype(u.dtype), u], axis=1)
    out = sum(up[:, j:j + l] * w[j] for j in range(CONV_W)) + bias
    return out, up[:, -(CONV_W - 1):]


def ssd_chunked(x, dt, a, bm, cm, h0):
    b, l = x.shape[:2]
    pad = (-l) % CHUNK
    padt = lambda t: jnp.pad(t, [(0, 0), (0, pad)] + [(0, 0)] * (t.ndim - 2))
    x, dt, bm, cm = padt(x), padt(dt), padt(bm), padt(cm)
    nc = (l + pad) // CHUNK
    hpg = C_HEADS // C_GROUPS
    xc = x.reshape(b, nc, CHUNK, C_HEADS, C_HEAD_DIM)
    dtc = dt.reshape(b, nc, CHUNK, C_HEADS)
    bh = jnp.repeat(bm, hpg, axis=2).reshape(b, nc, CHUNK, C_HEADS, C_STATE)
    ch = jnp.repeat(cm, hpg, axis=2).reshape(b, nc, CHUNK, C_HEADS, C_STATE)
    cum = jnp.cumsum(dtc * a, axis=2)
    seg = cum[:, :, :, None, :] - cum[:, :, None, :, :]
    causal = jnp.tril(jnp.ones((CHUNK, CHUNK), bool))[None, None, :, :, None]
    decay = jnp.exp(jnp.where(causal, seg, -jnp.inf))
    scores = jnp.einsum('bcthn,bcshn->bctsh', ch, bh) * decay * dtc[:, :, None, :, :]
    y_diag = jnp.einsum('bctsh,bcshp->bcthp', scores, xc)
    to_end = jnp.exp(cum[:, :, -1:, :] - cum) * dtc
    chunk_states = jnp.einsum('bcshn,bcsh,bcshp->bchpn', bh, to_end, xc)
    chunk_decay = jnp.exp(cum[:, :, -1, :])

    def step(h, inp):
        dec, st = inp
        return dec[:, :, None, None] * h + st, h

    h_final, h_prev = lax.scan(step, h0, (chunk_decay.transpose(1, 0, 2), chunk_states.transpose(1, 0, 2, 3, 4)))
    h_prev = h_prev.transpose(1, 0, 2, 3, 4)
    y_off = jnp.einsum('bcthn,bchpn,bcth->bcthp', ch, h_prev, jnp.exp(cum))
    y = (y_diag + y_off).reshape(b, nc * CHUNK, C_HEADS, C_HEAD_DIM)[:, :l]
    return y, h_final


def ssm_mixer(z, xbc, dt_raw, conv_prev, h0, lp):
    b, l, _ = z.shape
    xbc, conv_new = causal_conv(xbc, conv_prev, lp['ssm_conv_w'], lp['ssm_conv_b'])
    xbc = jax.nn.silu(xbc.astype(F32))
    gn = C_GROUPS * C_STATE
    xs = xbc[..., :GROUP_W].reshape(b, l, C_HEADS, C_HEAD_DIM)
    bm = xbc[..., GROUP_W:GROUP_W + gn].reshape(b, l, C_GROUPS, C_STATE)
    cm = xbc[..., GROUP_W + gn:].reshape(b, l, C_GROUPS, C_STATE)
    dt = jax.nn.softplus(dt_raw.astype(F32) + lp['ssm_dt_bias'].astype(F32))
    a = -jnp.exp(lp['ssm_a_log'].astype(F32))
    y, h = ssd_chunked(xs, dt, a, bm, cm, h0.astype(F32))
    y = y + lp['ssm_d'].astype(F32)[:, None] * xs
    y = y.reshape(b, l, GROUP_W) * jax.nn.silu(z.astype(F32))
    return rms_norm(y, lp['ssm_norm_g']), conv_new, h


def linear_recurrence(a, u, h0):
    def comb(e1, e2):
        a1, u1 = e1
        a2, u2 = e2
        return a1 * a2, a2 * u1 + u2
    a_cum, u_cum = lax.associative_scan(comb, (a, u), axis=1)
    return u_cum + a_cum * h0[:, None]


def lru_mixer(rx, rg, conv_prev, h0, lp):
    b, l, _ = rx.shape
    xc, conv_new = causal_conv(rx, conv_prev, lp['lru_conv_w'], lp['lru_conv_b'])
    xb = xc.astype(F32).reshape(b, l, D_BLOCKS, D_BLOCK_W)
    r = jax.nn.sigmoid(jnp.einsum('blhi,hij->blhj', xb, lp['lru_wr'].astype(F32)) + lp['lru_br'].astype(F32))
    i = jax.nn.sigmoid(jnp.einsum('blhi,hij->blhj', xb, lp['lru_wi'].astype(F32)) + lp['lru_bi'].astype(F32))
    log_a = -LRU_C * r * jax.nn.softplus(-lp['lru_lambda'].astype(F32).reshape(D_BLOCKS, D_BLOCK_W))
    a = jnp.exp(log_a)
    u = jnp.sqrt(jnp.maximum(1.0 - jnp.exp(2.0 * log_a), 0.0)) * (i * xb)
    h = linear_recurrence(a.reshape(b, l, GROUP_W), u.reshape(b, l, GROUP_W), h0.astype(F32))
    y = h * jax.nn.gelu(rg.astype(F32))
    return y, conv_new, h[:, -1]


def swiglu(x, wg, wu, wd):
    return (jax.nn.silu(x @ wg) * (x @ wu)) @ wd


def sparse_experts(x, eidx, w, wg, wu, wd):
    n, d = x.shape
    nk = n * TOP_K
    flat_e = eidx.reshape(nk).astype(jnp.int32)
    flat_t = jnp.repeat(jnp.arange(n, dtype=jnp.int32), TOP_K)
    flat_w = w.reshape(nk)
    order = jnp.argsort(flat_e)
    se, st, sw = flat_e[order], flat_t[order], flat_w[order]
    counts = jnp.bincount(flat_e, length=N_EXPERTS).astype(jnp.int32)
    starts = jnp.cumsum(counts) - counts
    padded = (counts + MOE_BLOCK - 1) // MOE_BLOCK * MOE_BLOCK
    pends = jnp.cumsum(padded)
    pstarts = pends - padded
    dest = pstarts[se] + jnp.arange(nk, dtype=jnp.int32) - starts[se]
    nb = (nk + N_EXPERTS * (MOE_BLOCK - 1) + MOE_BLOCK - 1) // MOE_BLOCK
    buf_t = jnp.full((nb * MOE_BLOCK,), n, jnp.int32).at[dest].set(st)
    buf_w = jnp.zeros((nb * MOE_BLOCK,), F32).at[dest].set(sw)
    blk_e = jnp.minimum(jnp.searchsorted(pends, jnp.arange(nb, dtype=jnp.int32) * MOE_BLOCK, side='right'), N_EXPERTS - 1)
    xpad = jnp.concatenate([x, jnp.zeros((1, d), x.dtype)], axis=0)

    def expert_block(args):
        tok, wt, e = args
        xb = xpad[tok]
        hb = jax.nn.silu(xb @ wg[e]) * (xb @ wu[e])
        return (hb @ wd[e]).astype(F32) * wt[:, None]

    yb = lax.map(expert_block, (buf_t.reshape(nb, MOE_BLOCK), buf_w.reshape(nb, MOE_BLOCK), blk_e))
    return jax.ops.segment_sum(yb.reshape(nb * MOE_BLOCK, d), buf_t, num_segments=n + 1)[:n]


def moe(x, lp):
    n = x.shape[0]
    scores = jax.nn.sigmoid(jnp.einsum('nd,de->ne', x.astype(F32), lp['router_w'].astype(F32)))
    biased = scores + lp['router_bias'].astype(F32)
    grp = biased.reshape(n, N_GROUP, N_EXPERTS // N_GROUP)
    grp_score = lax.top_k(grp, 2)[0].sum(-1)
    _, gidx = lax.top_k(grp_score, TOPK_GROUP)
    gmask = jax.nn.one_hot(gidx, N_GROUP).sum(1) > 0
    emask = jnp.repeat(gmask, N_EXPERTS // N_GROUP, axis=1)
    _, eidx = lax.top_k(jnp.where(emask, biased, -jnp.inf), TOP_K)
    wsel = jnp.take_along_axis(scores, eidx, axis=1)
    wsel = wsel / wsel.sum(-1, keepdims=True) * ROUTE_SCALE
    routed = sparse_experts(x, eidx, wsel, lp['exp_w_gate'], lp['exp_w_up'], lp['exp_w_down'])
    shared = swiglu(x, lp['sh_w_gate'], lp['sh_w_up'], lp['sh_w_down']).astype(F32)
    return (routed + shared).astype(x.dtype)


def layer(x, lp, st):
    b, l, _ = x.shape
    proj = jnp.einsum('bld,dp->blp', x, lp['w_in'])
    parts, o = [], 0
    for size in PROJ_SIZES:
        parts.append(proj[..., o:o + size])
        o += size
    q, k, v, u, z, xbc, dt_raw, rx, rg = parts
    q = q.reshape(b, l, A_HEADS, HEAD_DIM)
    k = k.reshape(b, l, A_KV_HEADS, HEAD_DIM)
    v = v.reshape(b, l, A_KV_HEADS, HEAD_DIM)
    if st is None:
        ya = attn_prompt(q, k, v, lp['attn_sinks'])
        attn_new = (k[:, :N_META], v[:, :N_META], k[:, -WINDOW:], v[:, -WINDOW:])
        pool_prev = jnp.zeros((b, POOL_STATE, GROUP_W), x.dtype)
        n_prev = 0
        ssm_conv_prev = jnp.zeros((b, CONV_W - 1, C_CONV_CH), x.dtype)
        ssm_h0 = jnp.zeros((b, C_HEADS, C_HEAD_DIM, C_STATE), F32)
        lru_conv_prev = jnp.zeros((b, CONV_W - 1, GROUP_W), x.dtype)
        lru_h0 = jnp.zeros((b, GROUP_W), F32)
    else:
        meta_k, meta_v, win_k, win_v, pool_prev, ssm_conv_prev, ssm_h0, lru_conv_prev, lru_h0 = st
        ya = attn_sample(q, k, v, meta_k, meta_v, win_k, win_v, lp['attn_sinks'])
        attn_new = (k, v)
        n_prev = POOL_STATE
    yb, pool_new = pool_mixer(u, pool_prev, n_prev, lp['pool_w'], lp['pool_scale'])
    yc, ssm_conv_new, ssm_h = ssm_mixer(z, xbc, dt_raw, ssm_conv_prev, ssm_h0, lp)
    yd, lru_conv_new, lru_h = lru_mixer(rx, rg, lru_conv_prev, lru_h0, lp)
    mix = jnp.concatenate([ya.astype(x.dtype), yb.astype(x.dtype), yc.astype(x.dtype), yd.astype(x.dtype)], axis=-1)
    x = layer_norm(ALPHA * x + jnp.einsum('blm,md->bld', mix, lp['w_out']), lp['ln1_g'], lp['ln1_b'])
    x = layer_norm(ALPHA * x + moe(x.reshape(b * l, D_MODEL), lp).reshape(b, l, D_MODEL), lp['ln2_g'], lp['ln2_b'])
    return x, attn_new + (pool_new, ssm_conv_new, ssm_h, lru_conv_new, lru_h)


def setup_inputs(seed: int = 0) -> dict:
    key = jax.random.key(seed)
    ks = iter(jax.random.split(key, 64))

    def nrm(shape, scale=1.0):
        return jax.random.normal(next(ks), shape, F32) * scale

    def gain(shape):
        return 1.0 + nrm(shape, 0.02)

    cache_len = min(WINDOW, PAST_LEN)
    dt0 = jnp.exp(jax.random.uniform(next(ks), (DEPTH, C_HEADS), F32, math.log(1e-3), math.log(1e-1)))
    dt_bias = dt0 + jnp.log(-jnp.expm1(-dt0))
    a_log = jnp.log(jax.random.uniform(next(ks), (DEPTH, C_HEADS), F32, 1.0, 16.0))
    a0 = jax.random.uniform(next(ks), (DEPTH, GROUP_W), F32, 0.9, 0.999)
    sig = a0 ** (1.0 / LRU_C)
    lam = jnp.log(sig) - jnp.log1p(-sig)
    kvs = (DEPTH, DEC_BATCH)
    return {
        'x_prompt': nrm((BATCH, SEQ, D_MODEL)),
        'x_sample': nrm((DEC_BATCH, DEC_SEQ, D_MODEL)),
        'cache_attn_meta_k': nrm(kvs + (N_META, A_KV_HEADS, HEAD_DIM)),
        'cache_attn_meta_v': nrm(kvs + (N_META, A_KV_HEADS, HEAD_DIM)),
        'cache_attn_k': nrm(kvs + (cache_len, A_KV_HEADS, HEAD_DIM)),
        'cache_attn_v': nrm(kvs + (cache_len, A_KV_HEADS, HEAD_DIM)),
        'state_pool': nrm(kvs + (POOL_STATE, GROUP_W)),
        'state_ssm_conv': nrm(kvs + (CONV_W - 1, C_CONV_CH)),
        'state_ssm': nrm(kvs + (C_HEADS, C_HEAD_DIM, C_STATE), 0.1),
        'state_lru_conv': nrm(kvs + (CONV_W - 1, GROUP_W)),
        'state_lru': nrm(kvs + (GROUP_W,), 0.5),
        'meta_tokens': nrm((N_META, D_MODEL)),
        'ln_in_g': gain((D_MODEL,)),
        'ln_in_b': nrm((D_MODEL,), 0.02),
        'w_in': nrm((DEPTH, D_MODEL, PROJ_W), D_MODEL ** -0.5),
        'w_out': nrm((DEPTH, MIX_W, D_MODEL), BETA * MIX_W ** -0.5),
        'attn_sinks': nrm((DEPTH, A_HEADS), 0.5),
        'pool_w': nrm((DEPTH, 4, POOL_GW, POOL_GW), POOL_GW ** -0.5),
        'pool_scale': gain((DEPTH, GROUP_W)),
        'ssm_conv_w': nrm((DEPTH, CONV_W, C_CONV_CH), CONV_W ** -0.5),
        'ssm_conv_b': nrm((DEPTH, C_CONV_CH), 0.02),
        'ssm_dt_bias': dt_bias,
        'ssm_a_log': a_log,
        'ssm_d': gain((DEPTH, C_HEADS)),
        'ssm_norm_g': gain((DEPTH, GROUP_W)),
        'lru_conv_w': nrm((DEPTH, CONV_W, GROUP_W), CONV_W ** -0.5),
        'lru_conv_b': nrm((DEPTH, GROUP_W), 0.02),
        'lru_wr': nrm((DEPTH, D_BLOCKS, D_BLOCK_W, D_BLOCK_W), D_BLOCK_W ** -0.5),
        'lru_br': nrm((DEPTH, D_BLOCKS, D_BLOCK_W), 0.02),
        'lru_wi': nrm((DEPTH, D_BLOCKS, D_BLOCK_W, D_BLOCK_W), D_BLOCK_W ** -0.5),
        'lru_bi': nrm((DEPTH, D_BLOCKS, D_BLOCK_W), 0.02),
        'lru_lambda': lam,
        'ln1_g': gain((DEPTH, D_MODEL)),
        'ln1_b': nrm((DEPTH, D_MODEL), 0.02),
        'ln2_g': gain((DEPTH, D_MODEL)),
        'ln2_b': nrm((DEPTH, D_MODEL), 0.02),
        'router_w': nrm((DEPTH, D_MODEL, N_EXPERTS), D_MODEL ** -0.5),
        'router_bias': nrm((DEPTH, N_EXPERTS), 0.01),
        'exp_w_gate': nrm((DEPTH, N_EXPERTS, D_MODEL, D_EXPERT), D_MODEL ** -0.5),
        'exp_w_up': nrm((DEPTH, N_EXPERTS, D_MODEL, D_EXPERT), D_MODEL ** -0.5),
        'exp_w_down': nrm((DEPTH, N_EXPERTS, D_EXPERT, D_MODEL), BETA * D_EXPERT ** -0.5),
        'sh_w_gate': nrm((DEPTH, D_MODEL, D_EXPERT), D_MODEL ** -0.5),
        'sh_w_up': nrm((DEPTH, D_MODEL, D_EXPERT), D_MODEL ** -0.5),
        'sh_w_down': nrm((DEPTH, D_EXPERT, D_MODEL), BETA * D_EXPERT ** -0.5),
    }


def reference(x_prompt, x_sample, cache_attn_meta_k, cache_attn_meta_v, cache_attn_k, cache_attn_v, state_pool,
              state_ssm_conv, state_ssm, state_lru_conv, state_lru, meta_tokens, ln_in_g, ln_in_b, w_in, w_out,
              attn_sinks, pool_w, pool_scale, ssm_conv_w, ssm_conv_b, ssm_dt_bias, ssm_a_log, ssm_d, ssm_norm_g,
              lru_conv_w, lru_conv_b, lru_wr, lru_br, lru_wi, lru_bi, lru_lambda, ln1_g, ln1_b, ln2_g, ln2_b,
              router_w, router_bias, exp_w_gate, exp_w_up, exp_w_down, sh_w_gate, sh_w_up, sh_w_down):
    b = x_prompt.shape[0]
    meta = jnp.broadcast_to(meta_tokens[None].astype(x_prompt.dtype), (b, N_META, D_MODEL))
    hp = layer_norm(jnp.concatenate([meta, x_prompt], axis=1), ln_in_g, ln_in_b)
    hs = layer_norm(x_sample, ln_in_g, ln_in_b)
    p_st, s_st = [], []
    for i in range(DEPTH):
        lp = dict(w_in=w_in[i], w_out=w_out[i], attn_sinks=attn_sinks[i], pool_w=pool_w[i], pool_scale=pool_scale[i],
                  ssm_conv_w=ssm_conv_w[i], ssm_conv_b=ssm_conv_b[i], ssm_dt_bias=ssm_dt_bias[i], ssm_a_log=ssm_a_log[i],
                  ssm_d=ssm_d[i], ssm_norm_g=ssm_norm_g[i], lru_conv_w=lru_conv_w[i], lru_conv_b=lru_conv_b[i],
                  lru_wr=lru_wr[i], lru_br=lru_br[i], lru_wi=lru_wi[i], lru_bi=lru_bi[i], lru_lambda=lru_lambda[i],
                  ln1_g=ln1_g[i], ln1_b=ln1_b[i], ln2_g=ln2_g[i], ln2_b=ln2_b[i], router_w=router_w[i],
                  router_bias=router_bias[i], exp_w_gate=exp_w_gate[i], exp_w_up=exp_w_up[i], exp_w_down=exp_w_down[i],
                  sh_w_gate=sh_w_gate[i], sh_w_up=sh_w_up[i], sh_w_down=sh_w_down[i])
        hp, ps = layer(hp, lp, None)
        st = (cache_attn_meta_k[i], cache_attn_meta_v[i], cache_attn_k[i], cache_attn_v[i], state_pool[i],
              state_ssm_conv[i], state_ssm[i], state_lru_conv[i], state_lru[i])
        hs, ss = layer(hs, lp, st)
        p_st.append(ps)
        s_st.append(ss)
    stk = lambda sts, j: jnp.stack([s[j] for s in sts])
    return (hp[:, N_META:], hs,
            stk(p_st, 0), stk(p_st, 1), stk(p_st, 2), stk(p_st, 3), stk(p_st, 4), stk(p_st, 5), stk(p_st, 6),
            stk(p_st, 7), stk(p_st, 8),
            stk(s_st, 0), stk(s_st, 1), stk(s_st, 2), stk(s_st, 3), stk(s_st, 4), stk(s_st, 5), stk(s_st, 6))
```

```python
import functools
import math

import jax
import jax.numpy as jnp
from jax import lax
from jax.experimental import pallas as pl
from jax.experimental.pallas import tpu as pltpu

F32 = jnp.float32
BF16 = jnp.bfloat16

D_MODEL = 2048
BATCH = 4
SEQ = 4096
DEPTH = 4
DEC_BATCH = 16
DEC_SEQ = 64
N_META = 16
GROUP_W = 512
HEAD_DIM = 64
A_HEADS = 8
A_KV_HEADS = 2
A_GROUP = 4
KV_W = 128
WINDOW = 128
POOL_SIZES = (2, 4, 8, 16)
POOL_GW = 128
POOL_STATE = 15
C_HEADS = 8
C_STATE = 128
C_CONV_CH = 1024
CONV_W = 4
D_BLOCKS = 8
D_BLOCK_W = 64
LRU_C = 8.0
N_EXPERTS = 64
TOP_K = 8
N_GROUP = 8
TOPK_GROUP = 4
D_EXPERT = 512
ROUTE_SCALE = 2.5
ALPHA = (2 * DEPTH) ** 0.25
LN_EPS = 1e-5

LANE = 128
SUBLANE = 8
VMEM_LIMIT = 56 * 1024 * 1024

P_ROWS = BATCH * SEQ
S_ROWS = DEC_BATCH * DEC_SEQ
M_ROWS = BATCH * N_META
S_ROW0 = P_ROWS
M_ROW0 = P_ROWS + S_ROWS
NT = P_ROWS + S_ROWS + M_ROWS
TM = 512
N_TILES = -(-NT // TM)

C_Q, C_U, C_Z, C_RX, C_RG, C_XS, C_B, C_C, C_K, C_V, C_DT = (
    0, 512, 1024, 1536, 2048, 2560, 3072, 3328, 3584, 3712, 3840)
PROJ_P = 4096
PROJ_TN = 1024

MOE_T = 256
NK = NT * TOP_K
MOE_NB = (NK + N_EXPERTS * (MOE_T - 1) + MOE_T - 1) // MOE_T


def _cparams(sem):
    return pltpu.CompilerParams(dimension_semantics=sem, vmem_limit_bytes=VMEM_LIMIT)


def _ln(x, g, b):
    mu = jnp.mean(x, axis=-1, keepdims=True)
    xc = x - mu
    var = jnp.mean(xc * xc, axis=-1, keepdims=True)
    return xc * lax.rsqrt(var + LN_EPS) * g + b


def _dot(a, b):
    return jnp.dot(a, b, preferred_element_type=F32)


def _dot_nt(a, b, precision=None):
    return lax.dot_general(a, b, (((1,), (1,)), ((), ())), precision=precision,
                           preferred_element_type=F32)


def _full(shape):
    nd = len(shape)
    return pl.BlockSpec(shape, lambda *_: (0,) * nd)


def _ln_in_kernel(xp_ref, xs_ref, meta_ref, g_ref, b_ref, o_ref):
    i = pl.program_id(0)
    g = g_ref[...]
    b = b_ref[...]
    n_p = P_ROWS // TM
    n_s = S_ROWS // TM

    @pl.when(i < n_p)
    def _():
        o_ref[...] = _ln(xp_ref[...], g, b)

    @pl.when((i >= n_p) & (i < n_p + n_s))
    def _():
        o_ref[...] = _ln(xs_ref[...], g, b)

    @pl.when(i == n_p + n_s)
    def _():
        m = _ln(meta_ref[...], g, b)
        for r in range(BATCH):
            o_ref[N_META * r:N_META * (r + 1), :] = m


def ln_in(xp2, xs2, meta, g, b):
    n_p = P_ROWS // TM
    n_s = S_ROWS // TM
    return pl.pallas_call(
        _ln_in_kernel,
        grid=(N_TILES,),
        in_specs=[
            pl.BlockSpec((TM, D_MODEL), lambda i: (jnp.minimum(i, n_p - 1), 0)),
            pl.BlockSpec((TM, D_MODEL), lambda i: (jnp.clip(i - n_p, 0, n_s - 1), 0)),
            _full((N_META, D_MODEL)),
            _full((1, D_MODEL)),
            _full((1, D_MODEL)),
        ],
        out_specs=pl.BlockSpec((TM, D_MODEL), lambda i: (i, 0)),
        out_shape=jax.ShapeDtypeStruct((NT, D_MODEL), F32),
        compiler_params=_cparams(("arbitrary",)),
        name="ln_in",
    )(xp2, xs2, meta, g, b)


def _in_proj_kernel(x_ref, w_ref, o_ref):
    o_ref[...] = _dot(x_ref[...].astype(BF16), w_ref[...])


def in_proj(h, w_bf):
    return pl.pallas_call(
        _in_proj_kernel,
        grid=(N_TILES, PROJ_P // PROJ_TN),
        in_specs=[
            pl.BlockSpec((TM, D_MODEL), lambda i, n: (i, 0)),
            pl.BlockSpec((D_MODEL, PROJ_TN), lambda i, n: (0, n)),
        ],
        out_specs=pl.BlockSpec((TM, PROJ_TN), lambda i, n: (i, n)),
        out_shape=jax.ShapeDtypeStruct((NT, PROJ_P), F32),
        compiler_params=_cparams(("arbitrary", "arbitrary")),
        name="in_proj",
    )(h, w_bf)


ATT_PAD = 64
ATT_WIN = ATT_PAD + WINDOW + 64


def _attn_kernel(sink_ref, q_ref, kc_ref, vc_ref, kp_ref, vp_ref, km_ref, vm_ref, o_ref,
                 kbuf, vbuf, *, ch, nq, use_meta, chunk0):
    j = pl.program_id(1)
    tq = ch * nq
    zpad = jnp.zeros((ATT_PAD - N_META, KV_W), BF16)
    kbuf[0:N_META, :] = km_ref[...].astype(BF16)
    kbuf[N_META:ATT_PAD, :] = zpad
    vbuf[0:N_META, :] = vm_ref[...].astype(BF16)
    vbuf[N_META:ATT_PAD, :] = zpad
    kbuf[ATT_PAD:ATT_PAD + WINDOW, :] = kp_ref[...].astype(BF16)
    vbuf[ATT_PAD:ATT_PAD + WINDOW, :] = vp_ref[...].astype(BF16)
    kbuf[ATT_PAD + WINDOW:ATT_PAD + WINDOW + tq, :] = kc_ref[...].astype(BF16)
    vbuf[ATT_PAD + WINDOW:ATT_PAD + WINDOW + tq, :] = vc_ref[...].astype(BF16)
    if ch < 64:
        zc = jnp.zeros((64 - ch, KV_W), BF16)
        kbuf[ATT_PAD + WINDOW + tq:ATT_PAD + WINDOW + tq + 64 - ch, :] = zc
        vbuf[ATT_PAD + WINDOW + tq:ATT_PAD + WINDOW + tq + 64 - ch, :] = zc

    rows = A_GROUP * ch
    col = lax.broadcasted_iota(jnp.int32, (rows, ATT_WIN), 1)
    row = lax.broadcasted_iota(jnp.int32, (rows, 1), 0)
    lane = lax.broadcasted_iota(jnp.int32, (ch, LANE), 1)
    lo = lane < HEAD_DIM

    for i in range(nq):
        c = chunk0 + j * nq + i
        first_band = ATT_PAD + 64 * jnp.maximum(2 - c, 0)
        valid = (col >= first_band) & (col < ATT_PAD + WINDOW + ch)
        if use_meta:
            valid = valid | (col < N_META)
        kcat = jnp.concatenate(
            [kbuf[0:ATT_PAD, :], kbuf[ATT_PAD + ch * i:ATT_PAD + ch * i + WINDOW + 64, :]], axis=0)
        vcat = jnp.concatenate(
            [vbuf[0:ATT_PAD, :], vbuf[ATT_PAD + ch * i:ATT_PAD + ch * i + WINDOW + 64, :]], axis=0)
        qi = q_ref[ch * i:ch * (i + 1), :] * (HEAD_DIM ** -0.5)
        tiles = [qi[:, LANE * t:LANE * (t + 1)] for t in range(A_HEADS // 2)]
        out_tiles = [None] * (A_HEADS // 2)
        for kh in range(A_KV_HEADS):
            qs = []
            for r in range(A_GROUP):
                h = A_GROUP * kh + r
                t = tiles[h // 2]
                if h % 2 != kh:
                    t = pltpu.roll(t, HEAD_DIM, axis=1)
                keep = lo if kh == 0 else jnp.logical_not(lo)
                qs.append(jnp.where(keep, t, 0.0).astype(BF16))
            qz = jnp.concatenate(qs, axis=0)
            s = _dot_nt(qz, kcat)
            s = jnp.where(valid, s, -jnp.inf)
            sink = jnp.zeros((rows, 1), F32)
            for r in range(A_GROUP):
                sink = jnp.where((row >= r * ch) & (row < (r + 1) * ch),
                                 sink_ref[A_GROUP * kh + r], sink)
            m = jnp.maximum(jnp.max(s, axis=-1, keepdims=True), sink)
            p = jnp.exp(s - m)
            den = jnp.sum(p, axis=-1, keepdims=True) + jnp.exp(sink - m)
            probs = (p / den).astype(BF16)
            o = _dot(probs, vcat)
            for r in range(A_GROUP):
                h = A_GROUP * kh + r
                oh = o[r * ch:(r + 1) * ch, :]
                if h % 2 != kh:
                    oh = pltpu.roll(oh, HEAD_DIM, axis=1)
                keep = lo if h % 2 == 0 else jnp.logical_not(lo)
                prev = out_tiles[h // 2]
                out_tiles[h // 2] = jnp.where(keep, oh, 0.0 if prev is None else prev)
        o_ref[ch * i:ch * (i + 1), :] = jnp.concatenate(out_tiles, axis=1)


def attn_call(sinks, q_src, kc_src, kp_src, km_src, *, nb, length, ch, nq, use_meta, chunk0,
              q_row0, kp_map, km_map, kp_cols, km_cols):
    tq = ch * nq
    nj = length // tq
    qb0 = q_row0 // tq
    kern = functools.partial(_attn_kernel, ch=ch, nq=nq, use_meta=use_meta, chunk0=chunk0)
    return pl.pallas_call(
        kern,
        grid=(nb, nj),
        in_specs=[
            pl.BlockSpec(memory_space=pltpu.SMEM),
            pl.BlockSpec((tq, GROUP_W), lambda b, j: (qb0 + b * nj + j, C_Q // GROUP_W)),
            pl.BlockSpec((tq, KV_W), lambda b, j: (qb0 + b * nj + j, C_K // KV_W)),
            pl.BlockSpec((tq, KV_W), lambda b, j: (qb0 + b * nj + j, C_V // KV_W)),
            pl.BlockSpec((WINDOW, KV_W), lambda b, j: (kp_map(b, j), kp_cols[0])),
            pl.BlockSpec((WINDOW, KV_W), lambda b, j: (kp_map(b, j), kp_cols[1])),
            pl.BlockSpec((N_META, KV_W), lambda b, j: (km_map(b, j), km_cols[0])),
            pl.BlockSpec((N_META, KV_W), lambda b, j: (km_map(b, j), km_cols[1])),
        ],
        out_specs=pl.BlockSpec((tq, GROUP_W), lambda b, j: (b * nj + j, 0)),
        out_shape=jax.ShapeDtypeStruct((nb * length, GROUP_W), F32),
        scratch_shapes=[pltpu.VMEM((ATT_PAD + WINDOW + tq + 64, KV_W), BF16),
                        pltpu.VMEM((ATT_PAD + WINDOW + tq + 64, KV_W), BF16)],
        compiler_params=_cparams(("arbitrary", "arbitrary")),
        name="attn",
    )(sinks, q_src, kc_src, kc_src, kp_src[0], kp_src[1], km_src[0], km_src[1])


def _pool_kernel(u_ref, prev_ref, w_ref, scale_ref, o_ref, buf, *, tb, ramp):
    j = pl.program_id(1)

    @pl.when(j == 0)
    def _():
        buf[0:16, :] = prev_ref[...]

    buf[16:16 + tb, :] = u_ref[...]
    pos = j * tb + lax.broadcasted_iota(jnp.int32, (tb, 1), 0)
    outs = []
    for g, win in enumerate(POOL_SIZES):
        sl = slice(g * POOL_GW, (g + 1) * POOL_GW)
        tot = buf[16:16 + tb, sl]
        for k in range(1, win):
            tot = tot + buf[16 - k:16 - k + tb, sl]
        if ramp:
            cnt = jnp.minimum(win, pos + 1).astype(F32)
            mean = tot / cnt
        else:
            mean = tot * (1.0 / win)
        d = mean - buf[16:16 + tb, sl]
        outs.append(_dot(d.astype(BF16), w_ref[g]))
    o_ref[...] = jnp.concatenate(outs, axis=1) * scale_ref[...]
    buf[0:16, :] = buf[tb:tb + 16, :]


def pool_call(proj, prev_src, prev_map, prev_col, w_bf, scale, *, nb, length, tb, row0, ramp):
    nj = length // tb
    rb0 = row0 // tb
    kern = functools.partial(_pool_kernel, tb=tb, ramp=ramp)
    return pl.pallas_call(
        kern,
        grid=(nb, nj),
        in_specs=[
            pl.BlockSpec((tb, GROUP_W), lambda b, j: (rb0 + b * nj + j, C_U // GROUP_W)),
            pl.BlockSpec((16, GROUP_W), lambda b, j: (prev_map(b), prev_col)),
            _full((4, POOL_GW, POOL_GW)),
            _full((1, GROUP_W)),
        ],
        out_specs=pl.BlockSpec((tb, GROUP_W), lambda b, j: (b * nj + j, 0)),
        out_shape=jax.ShapeDtypeStruct((nb * length, GROUP_W), F32),
        scratch_shapes=[pltpu.VMEM((tb + 16, GROUP_W), F32)],
        compiler_params=_cparams(("arbitrary", "arbitrary")),
        name="pool",
    )(proj, prev_src, w_bf, scale)


def _conv_block(buf, x_ref, w_ref, b_ref, tb):
    buf[8:8 + tb, :] = x_ref[...]
    acc = b_ref[...] + buf[5:5 + tb, :] * w_ref[0:1, :]
    for k in range(1, CONV_W):
        acc = acc + buf[5 + k:5 + k + tb, :] * w_ref[k:k + 1, :]
    return acc


def _conv_carry(buf, tb):
    buf[0:8, :] = buf[tb:tb + 8, :]


def _col(x, h):
    lane = lax.broadcasted_iota(jnp.int32, x.shape, 1)
    return jnp.sum(jnp.where(lane == h, x, 0.0), axis=1, keepdims=True)


def _ssd_kernel(xs_ref, bm_ref, cm_ref, dt_ref, z_ref, px_ref, pb_ref, pc_ref, h0_ref,
                wx_ref, wb_ref, wc_ref, bx_ref, bb_ref, bc_ref,
                dtb_ref, alog_ref, dskip_ref, ng_ref,
                y_ref, hout_ref, bufx, bufb, bufc, hst, *, q):
    j = pl.program_id(1)
    nj = pl.num_programs(1)
    hi = lax.Precision.HIGHEST

    @pl.when(j == 0)
    def _():
        bufx[0:8, :] = px_ref[...]
        bufb[0:8, :] = pb_ref[...]
        bufc[0:8, :] = pc_ref[...]
        hst[...] = h0_ref[...]

    xs = jax.nn.silu(_conv_block(bufx, xs_ref, wx_ref, bx_ref, q))
    bm = jax.nn.silu(_conv_block(bufb, bm_ref, wb_ref, bb_ref, q))
    cm = jax.nn.silu(_conv_block(bufc, cm_ref, wc_ref, bc_ref, q))
    _conv_carry(bufx, q)
    _conv_carry(bufb, q)
    _conv_carry(bufc, q)

    lane1 = lax.broadcasted_iota(jnp.int32, (1, LANE), 1)
    hmask = lane1 < C_HEADS
    dt = jnp.where(hmask, jax.nn.softplus(dt_ref[...] + dtb_ref[...]), 0.0)
    a = jnp.where(hmask, -jnp.exp(alog_ref[...]), 0.0)
    dta = dt * a
    ri = lax.broadcasted_iota(jnp.int32, (q, q), 0)
    ci = lax.broadcasted_iota(jnp.int32, (q, q), 1)
    tri = ri >= ci
    cum = jnp.dot(tri.astype(F32), dta, precision=hi, preferred_element_type=F32)
    eye = (lax.broadcasted_iota(jnp.int32, (LANE, LANE), 0)
           == lax.broadcasted_iota(jnp.int32, (LANE, LANE), 1)).astype(F32)
    cum_t = _dot_nt(eye, cum, precision=hi)
    dt_t = _dot_nt(eye, dt, precision=hi)
    ecum = jnp.exp(cum)
    cum_last = cum[q - 1:q, :]
    te = jnp.exp(cum_last - cum) * dt

    lane = lax.broadcasted_iota(jnp.int32, (q, LANE), 1)
    lo = lane < 64
    bm_bf = bm.astype(BF16)
    cm_bf = cm.astype(BF16)
    cb = [_dot_nt(cm_bf[:, LANE * g:LANE * (g + 1)], bm_bf[:, LANE * g:LANE * (g + 1)])
          for g in range(2)]
    yoff = [_dot_nt(cm_bf[:, LANE * g:LANE * (g + 1)], hst[256 * g:256 * (g + 1), :].astype(BF16))
            for g in range(2)]

    y_tiles = []
    xw_tiles = []
    for k in range(C_HEADS // 2):
        g = k // 2
        x_pair = xs[:, LANE * k:LANE * (k + 1)]
        ydiag = None
        for par in range(2):
            h = 2 * k + par
            seg = _col(cum, h) - cum_t[h:h + 1, :]
            lm = jnp.exp(jnp.where(tri, seg, -jnp.inf))
            mm = (cb[g] * lm * dt_t[h:h + 1, :]).astype(BF16)
            xm = jnp.where(lo if par == 0 else jnp.logical_not(lo), x_pair, 0.0).astype(BF16)
            part = _dot(mm, xm)
            ydiag = part if ydiag is None else ydiag + part
        e_pair = jnp.where(lo, _col(ecum, 2 * k), _col(ecum, 2 * k + 1))
        te_pair = jnp.where(lo, _col(te, 2 * k), _col(te, 2 * k + 1))
        kk = k % 2
        y_tiles.append(ydiag + yoff[g][:, LANE * kk:LANE * (kk + 1)] * e_pair
                       + dskip_ref[:, LANE * k:LANE * (k + 1)] * x_pair)
        xw_tiles.append((x_pair * te_pair).astype(BF16))

    eye2 = (lax.broadcasted_iota(jnp.int32, (256, 256), 0)
            == lax.broadcasted_iota(jnp.int32, (256, 256), 1)).astype(BF16)
    for g in range(2):
        xw = jnp.concatenate(xw_tiles[2 * g:2 * g + 2], axis=1)
        xw_t = _dot_nt(eye2, xw).astype(BF16)
        s_new = _dot(xw_t, bm_bf[:, LANE * g:LANE * (g + 1)])
        dec = jnp.concatenate(
            [jnp.broadcast_to(jnp.exp(cum_t[4 * g + r:4 * g + r + 1, q - 1:q]), (64, LANE))
             for r in range(4)], axis=0)
        hst[256 * g:256 * (g + 1), :] = dec * hst[256 * g:256 * (g + 1), :] + s_new

    y = jnp.concatenate(y_tiles, axis=1) * jax.nn.silu(z_ref[...])
    y = y * lax.rsqrt(jnp.mean(y * y, axis=-1, keepdims=True) + 1e-6) * ng_ref[...]
    y_ref[...] = y

    @pl.when(j == nj - 1)
    def _():
        hout_ref[...] = hst[...]


def ssd_call(proj, prev_src, prev_map, prev_cols, h0, lw, *, nb, length, q, row0):
    nj = length // q
    rb0 = row0 // q
    kern = functools.partial(_ssd_kernel, q=q)
    blk = lambda width, col: pl.BlockSpec((q, width), lambda b, j: (rb0 + b * nj + j, col // width))
    pblk = lambda width, col: pl.BlockSpec((8, width), lambda b, j: (prev_map(b), col))
    return pl.pallas_call(
        kern,
        grid=(nb, nj),
        in_specs=[
            blk(512, C_XS), blk(256, C_B), blk(256, C_C), blk(LANE, C_DT), blk(512, C_Z),
            pblk(512, prev_cols[0]), pblk(256, prev_cols[1]), pblk(256, prev_cols[2]),
            pl.BlockSpec((None, 512, C_STATE), lambda b, j: (b, 0, 0)),
            _full((CONV_W, 512)), _full((CONV_W, 256)), _full((CONV_W, 256)),
            _full((1, 512)), _full((1, 256)), _full((1, 256)),
            _full((1, LANE)), _full((1, LANE)), _full((1, 512)), _full((1, 512)),
        ],
        out_specs=[
            pl.BlockSpec((q, GROUP_W), lambda b, j: (b * nj + j, 0)),
            pl.BlockSpec((None, 512, C_STATE), lambda b, j: (b, 0, 0)),
        ],
        out_shape=[jax.ShapeDtypeStruct((nb * length, GROUP_W), F32),
                   jax.ShapeDtypeStruct((nb, 512, C_STATE), F32)],
        scratch_shapes=[pltpu.VMEM((q + 8, 512), F32), pltpu.VMEM((q + 8, 256), F32),
                        pltpu.VMEM((q + 8, 256), F32), pltpu.VMEM((512, C_STATE), F32)],
        compiler_params=_cparams(("arbitrary", "arbitrary")),
        name="ssd",
    )(proj, proj, proj, proj, proj, prev_src, prev_src, prev_src, h0,
      lw['cw_x'], lw['cw_b'], lw['cw_c'], lw['cb_x'], lw['cb_b'], lw['cb_c'],
      lw['dt_bias'], lw['a_log'], lw['d_skip'], lw['norm_g'])


def _lru_kernel(rx_ref, rg_ref, prev_ref, h0_ref, cw_ref, cb_ref, wr_ref, wi_ref,
                br_ref, bi_ref, lam_ref, y_ref, hout_ref, buf, hc, *, tb):
    j = pl.program_id(1)
    nj = pl.num_programs(1)

    @pl.when(j == 0)
    def _():
        buf[0:8, :] = prev_ref[...]
        hc[...] = jnp.broadcast_to(h0_ref[...], hc.shape)

    xc = _conv_block(buf, rx_ref, cw_ref, cb_ref, tb)
    _conv_carry(buf, tb)
    rs, gs = [], []
    for s in range(GROUP_W // LANE):
        xb = xc[:, LANE * s:LANE * (s + 1)].astype(BF16)
        rs.append(_dot(xb, wr_ref[s]))
        gs.append(_dot(xb, wi_ref[s]))
    r = jax.nn.sigmoid(jnp.concatenate(rs, axis=1) + br_ref[...])
    gi = jax.nn.sigmoid(jnp.concatenate(gs, axis=1) + bi_ref[...])
    log_a = -LRU_C * r * jax.nn.softplus(-lam_ref[...])
    a = jnp.exp(log_a)
    u = jnp.sqrt(jnp.maximum(1.0 - jnp.exp(2.0 * log_a), 0.0)) * (gi * xc)
    t = lax.broadcasted_iota(jnp.int32, (tb, 1), 0)
    d = 1
    while d < tb:
        a_sh = jnp.where(t >= d, pltpu.roll(a, d, axis=0), 1.0)
        u_sh = jnp.where(t >= d, pltpu.roll(u, d, axis=0), 0.0)
        u = a * u_sh + u
        a = a * a_sh
        d *= 2
    h = u + a * hc[0:1, :]
    y_ref[...] = h * jax.nn.gelu(rg_ref[...])
    hc[...] = jnp.broadcast_to(h[tb - 1:tb, :], hc.shape)

    @pl.when(j == nj - 1)
    def _():
        hout_ref[...] = h[tb - 1:tb, :]


def lru_call(proj, prev_src, prev_map, prev_col, h0, lw, *, nb, length, tb, row0):
    nj = length // tb
    rb0 = row0 // tb
    kern = functools.partial(_lru_kernel, tb=tb)
    return pl.pallas_call(
        kern,
        grid=(nb, nj),
        in_specs=[
            pl.BlockSpec((tb, GROUP_W), lambda b, j: (rb0 + b * nj + j, C_RX // GROUP_W)),
            pl.BlockSpec((tb, GROUP_W), lambda b, j: (rb0 + b * nj + j, C_RG // GROUP_W)),
            pl.BlockSpec((8, GROUP_W), lambda b, j: (prev_map(b), prev_col)),
            pl.BlockSpec((None, 1, GROUP_W), lambda b, j: (b, 0, 0)),
            _full((CONV_W, GROUP_W)), _full((1, GROUP_W)),
            _full((4, LANE, LANE)), _full((4, LANE, LANE)),
            _full((1, GROUP_W)), _full((1, GROUP_W)), _full((1, GROUP_W)),
        ],
        out_specs=[
            pl.BlockSpec((tb, GROUP_W), lambda b, j: (b * nj + j, 0)),
            pl.BlockSpec((None, 1, GROUP_W), lambda b, j: (b, 0, 0)),
        ],
        out_shape=[jax.ShapeDtypeStruct((nb * length, GROUP_W), F32),
                   jax.ShapeDtypeStruct((nb, 1, GROUP_W), F32)],
        scratch_shapes=[pltpu.VMEM((tb + 8, GROUP_W), F32), pltpu.VMEM((8, GROUP_W), F32)],
        compiler_params=_cparams(("arbitrary", "arbitrary")),
        name="lru",
    )(proj, proj, prev_src, h0, lw['cw'], lw['cb'], lw['wr'], lw['wi'],
      lw['br'], lw['bi'], lw['lam'])


def _out_proj_kernel(mix_ref, h_ref, w_ref, g_ref, b_ref, rwh_ref, rwl_ref, x1_ref, sc_ref):
    y = _dot(mix_ref[...].astype(BF16), w_ref[...])
    x1 = _ln(ALPHA * h_ref[...] + y, g_ref[...], b_ref[...])
    x1_ref[...] = x1
    xh = x1.astype(BF16)
    xl = (x1 - xh.astype(F32)).astype(BF16)
    logits = _dot(xh, rwh_ref[...]) + (_dot(xl, rwh_ref[...]) + _dot(xh, rwl_ref[...]))
    sc_ref[...] = jax.nn.sigmoid(logits)


def out_proj(mix, h, w_bf, g, b, rw_hi, rw_lo):
    return pl.pallas_call(
        _out_proj_kernel,
        grid=(N_TILES,),
        in_specs=[
            pl.BlockSpec((TM, D_MODEL), lambda i: (i, 0)),
            pl.BlockSpec((TM, D_MODEL), lambda i: (i, 0)),
            _full((D_MODEL, D_MODEL)),
            _full((1, D_MODEL)), _full((1, D_MODEL)),
            _full((D_MODEL, LANE)), _full((D_MODEL, LANE)),
        ],
        out_specs=[pl.BlockSpec((TM, D_MODEL), lambda i: (i, 0)),
                   pl.BlockSpec((TM, LANE), lambda i: (i, 0))],
        out_shape=[jax.ShapeDtypeStruct((NT, D_MODEL), F32),
                   jax.ShapeDtypeStruct((NT, LANE), F32)],
        compiler_params=_cparams(("arbitrary",)),
        name="out_proj",
    )(mix, h, w_bf, g, b, rw_hi, rw_lo)


def _moe_kernel(blk_e_ref, nused_ref, x_ref, wg_ref, wu_ref, wd_ref, o_ref, wg_bf, wu_bf, wd_bf):
    i = pl.program_id(0)
    changed = jnp.logical_or(i == 0, blk_e_ref[i] != blk_e_ref[jnp.maximum(i - 1, 0)])

    @pl.when(jnp.logical_and(changed, i < nused_ref[0]))
    def _():
        wg_bf[...] = wg_ref[...].astype(BF16)
        wu_bf[...] = wu_ref[...].astype(BF16)
        wd_bf[...] = wd_ref[...].astype(BF16)

    @pl.when(i < nused_ref[0])
    def _():
        x = x_ref[...]
        hb = jax.nn.silu(_dot(x, wg_bf[...])) * _dot(x, wu_bf[...])
        o_ref[...] = _dot(hb.astype(BF16), wd_bf[...])

    @pl.when(i >= nused_ref[0])
    def _():
        o_ref[...] = jnp.zeros_like(o_ref)


def moe_experts(blk_e, nused, xg, wg, wu, wd):
    grid_spec = pltpu.PrefetchScalarGridSpec(
        num_scalar_prefetch=2,
        grid=(MOE_NB,),
        in_specs=[
            pl.BlockSpec((MOE_T, D_MODEL), lambda i, be, nu: (i, 0)),
            pl.BlockSpec((None, D_MODEL, D_EXPERT), lambda i, be, nu: (be[i], 0, 0)),
            pl.BlockSpec((None, D_MODEL, D_EXPERT), lambda i, be, nu: (be[i], 0, 0)),
            pl.BlockSpec((None, D_EXPERT, D_MODEL), lambda i, be, nu: (be[i], 0, 0)),
        ],
        out_specs=pl.BlockSpec((MOE_T, D_MODEL), lambda i, be, nu: (i, 0)),
        scratch_shapes=[pltpu.VMEM((D_MODEL, D_EXPERT), BF16),
                        pltpu.VMEM((D_MODEL, D_EXPERT), BF16),
                        pltpu.VMEM((D_EXPERT, D_MODEL), BF16)],
    )
    return pl.pallas_call(
        _moe_kernel,
        grid_spec=grid_spec,
        out_shape=jax.ShapeDtypeStruct((MOE_NB * MOE_T, D_MODEL), F32),
        compiler_params=_cparams(("arbitrary",)),
        name="moe_experts",
    )(blk_e, nused, xg, wg, wu, wd)


def _shared_kernel(x1_ref, routed_ref, wg_ref, wu_ref, wd_ref, g_ref, b_ref, o_ref):
    x1 = x1_ref[...]
    xb = x1.astype(BF16)
    hb = jax.nn.silu(_dot(xb, wg_ref[...])) * _dot(xb, wu_ref[...])
    shared = _dot(hb.astype(BF16), wd_ref[...])
    o_ref[...] = _ln(ALPHA * x1 + (routed_ref[...] + shared), g_ref[...], b_ref[...])


def shared_ln2(x1, routed, wg, wu, wd, g, b):
    return pl.pallas_call(
        _shared_kernel,
        grid=(N_TILES,),
        in_specs=[
            pl.BlockSpec((TM, D_MODEL), lambda i: (i, 0)),
            pl.BlockSpec((TM, D_MODEL), lambda i: (i, 0)),
            _full((D_MODEL, D_EXPERT)), _full((D_MODEL, D_EXPERT)), _full((D_EXPERT, D_MODEL)),
            _full((1, D_MODEL)), _full((1, D_MODEL)),
        ],
        out_specs=pl.BlockSpec((TM, D_MODEL), lambda i: (i, 0)),
        out_shape=jax.ShapeDtypeStruct((NT, D_MODEL), F32),
        compiler_params=_cparams(("arbitrary",)),
        name="shared_ln2",
    )(x1, routed, wg, wu, wd, g, b)


def _route(scores, router_bias):
    n = scores.shape[0]
    biased = scores + router_bias
    grp = biased.reshape(n, N_GROUP, N_EXPERTS // N_GROUP)
    grp_score = lax.top_k(grp, 2)[0].sum(-1)
    _, gidx = lax.top_k(grp_score, TOPK_GROUP)
    gmask = jax.nn.one_hot(gidx, N_GROUP).sum(1) > 0
    emask = jnp.repeat(gmask, N_EXPERTS // N_GROUP, axis=1)
    _, eidx = lax.top_k(jnp.where(emask, biased, -jnp.inf), TOP_K)
    wsel = jnp.take_along_axis(scores, eidx, axis=1)
    wsel = wsel / wsel.sum(-1, keepdims=True) * ROUTE_SCALE
    return eidx.astype(jnp.int32), wsel


def _schedule(eidx):
    flat_e = eidx.reshape(NK)
    order = jnp.argsort(flat_e)
    se = flat_e[order]
    counts = jnp.bincount(flat_e, length=N_EXPERTS).astype(jnp.int32)
    starts = jnp.cumsum(counts) - counts
    padded = (counts + MOE_T - 1) // MOE_T * MOE_T
    pends = jnp.cumsum(padded)
    pstarts = pends - padded
    dest = pstarts[se] + jnp.arange(NK, dtype=jnp.int32) - starts[se]
    buf_t = jnp.zeros((MOE_NB * MOE_T,), jnp.int32).at[dest].set((order // TOP_K).astype(jnp.int32))
    pos = jnp.zeros((NK,), jnp.int32).at[order].set(dest).reshape(NT, TOP_K)
    blk_e = jnp.minimum(
        jnp.searchsorted(pends, jnp.arange(MOE_NB, dtype=jnp.int32) * MOE_T, side='right'),
        N_EXPERTS - 1).astype(jnp.int32)
    nused = (pends[-1] // MOE_T).astype(jnp.int32).reshape(1)
    return buf_t, pos, blk_e, nused


def _pad_rows(x, rows):
    b, r, c = x.shape
    return jnp.pad(x, ((0, 0), (rows - r, 0), (0, 0))).reshape(b * rows, c)


def _layer(h, st, lw):
    proj = in_proj(h, lw['w_in'])
    sinks = lw['sinks']
    zeros16 = jnp.zeros((BATCH * 16, GROUP_W), F32)
    zeros8 = jnp.zeros((BATCH * 8, C_CONV_CH), F32)

    meta_blk = lambda b, j=None: M_ROW0 // N_META + b
    ya_m = attn_call(sinks, proj, proj, (proj, proj), (proj, proj), nb=BATCH, length=N_META,
                     ch=N_META, nq=1, use_meta=False, chunk0=0, q_row0=M_ROW0,
                     kp_map=lambda b, j: 0, km_map=meta_blk,
                     kp_cols=(C_K // KV_W, C_V // KV_W), km_cols=(C_K // KV_W, C_V // KV_W))
    nq_p = 4
    ya_p = attn_call(sinks, proj, proj, (proj, proj), (proj, proj), nb=BATCH, length=SEQ,
                     ch=64, nq=nq_p, use_meta=True, chunk0=0, q_row0=0,
                     kp_map=lambda b, j: jnp.maximum(b * (SEQ // WINDOW) + j * (64 * nq_p // WINDOW) - 1, 0),
                     km_map=meta_blk,
                     kp_cols=(C_K // KV_W, C_V // KV_W), km_cols=(C_K // KV_W, C_V // KV_W))
    ya_s = attn_call(sinks, proj, proj, (st['win_k'], st['win_v']), (st['meta_k'], st['meta_v']),
                     nb=DEC_BATCH, length=DEC_SEQ, ch=64, nq=1, use_meta=True, chunk0=2,
                     q_row0=S_ROW0, kp_map=lambda b, j: b, km_map=lambda b, j: b,
                     kp_cols=(0, 0), km_cols=(0, 0))

    yb_m = pool_call(proj, zeros16, lambda b: b, 0, lw['pool_w'], lw['pool_scale'],
                     nb=BATCH, length=N_META, tb=N_META, row0=M_ROW0, ramp=True)
    yb_p = pool_call(proj, proj, lambda b: M_ROW0 // 16 + b, C_U // GROUP_W, lw['pool_w'],
                     lw['pool_scale'], nb=BATCH, length=SEQ, tb=512, row0=0, ramp=False)
    yb_s = pool_call(proj, st['pool'], lambda b: b, 0, lw['pool_w'], lw['pool_scale'],
                     nb=DEC_BATCH, length=DEC_SEQ, tb=DEC_SEQ, row0=S_ROW0, ramp=False)

    ssd_w = lw['ssd']
    h0z = jnp.zeros((BATCH, 512, C_STATE), F32)
    yc_m, hc_m = ssd_call(proj, zeros8, lambda b: b, (0, 2, 3), h0z, ssd_w,
                          nb=BATCH, length=N_META, q=N_META, row0=M_ROW0)
    meta_tail = lambda b: (M_ROW0 + 8) // 8 + 2 * b
    yc_p, hc_p = ssd_call(proj, proj, meta_tail, (C_XS // 512, C_B // 256, C_C // 256), hc_m, ssd_w,
                          nb=BATCH, length=SEQ, q=256, row0=0)
    yc_s, hc_s = ssd_call(proj, st['ssm_conv'], lambda b: b, (0, 2, 3), st['ssm'], ssd_w,
                          nb=DEC_BATCH, length=DEC_SEQ, q=DEC_SEQ, row0=S_ROW0)

    lru_w = lw['lru']
    l0z = jnp.zeros((BATCH, 1, GROUP_W), F32)
    yd_m, hd_m = lru_call(proj, zeros8, lambda b: b, 0, l0z, lru_w,
                          nb=BATCH, length=N_META, tb=N_META, row0=M_ROW0)
    yd_p, hd_p = lru_call(proj, proj, meta_tail, C_RX // GROUP_W, hd_m, lru_w,
                          nb=BATCH, length=SEQ, tb=256, row0=0)
    yd_s, hd_s = lru_call(proj, st['lru_conv'], lambda b: b, 0, st['lru'], lru_w,
                          nb=DEC_BATCH, length=DEC_SEQ, tb=DEC_SEQ, row0=S_ROW0)

    mix = jnp.concatenate([
        jnp.concatenate([ya_p, yb_p, yc_p, yd_p], axis=1),
        jnp.concatenate([ya_s, yb_s, yc_s, yd_s], axis=1),
        jnp.concatenate([ya_m, yb_m, yc_m, yd_m], axis=1)], axis=0)

    x1, sc = out_proj(mix, h, lw['w_out'], lw['ln1_g'], lw['ln1_b'], lw['rw_hi'], lw['rw_lo'])

    eidx, wsel = _route(sc[:, :N_EXPERTS], lw['router_bias'])
    buf_t, pos, blk_e, nused = _schedule(eidx)
    xg = x1.astype(BF16)[buf_t]
    yb = moe_experts(blk_e, nused, xg, lw['wg'], lw['wu'], lw['wd'])
    routed = jnp.sum(yb[pos] * wsel[:, :, None], axis=1)
    h_new = shared_ln2(x1, routed, lw['sh_wg'], lw['sh_wu'], lw['sh_wd'], lw['ln2_g'], lw['ln2_b'])

    pr = proj[:P_ROWS].reshape(BATCH, SEQ, PROJ_P)
    sr = proj[S_ROW0:S_ROW0 + S_ROWS].reshape(DEC_BATCH, DEC_SEQ, PROJ_P)
    mr = proj[M_ROW0:].reshape(BATCH, N_META, PROJ_P)
    kv4 = lambda x: x.reshape(x.shape[0], x.shape[1], A_KV_HEADS, HEAD_DIM)
    p_state = (
        kv4(mr[:, :, C_K:C_K + KV_W]), kv4(mr[:, :, C_V:C_V + KV_W]),
        kv4(pr[:, -WINDOW:, C_K:C_K + KV_W]), kv4(pr[:, -WINDOW:, C_V:C_V + KV_W]),
        pr[:, -POOL_STATE:, C_U:C_U + GROUP_W],
        pr[:, -(CONV_W - 1):, C_XS:C_XS + C_CONV_CH],
        hc_p.reshape(BATCH, C_HEADS, 64, C_STATE),
        pr[:, -(CONV_W - 1):, C_RX:C_RX + GROUP_W],
        hd_p.reshape(BATCH, GROUP_W),
    )
    s_state = (
        kv4(sr[:, :, C_K:C_K + KV_W]), kv4(sr[:, :, C_V:C_V + KV_W]),
        sr[:, -POOL_STATE:, C_U:C_U + GROUP_W],
        sr[:, -(CONV_W - 1):, C_XS:C_XS + C_CONV_CH],
        hc_s.reshape(DEC_BATCH, C_HEADS, 64, C_STATE),
        sr[:, -(CONV_W - 1):, C_RX:C_RX + GROUP_W],
        hd_s.reshape(DEC_BATCH, GROUP_W),
    )
    return h_new, p_state, s_state


def _block_diag(w):
    z = jnp.zeros((D_BLOCK_W, D_BLOCK_W), w.dtype)
    return jnp.stack([jnp.block([[w[2 * s], z], [z, w[2 * s + 1]]]) for s in range(4)])


def _pad_lanes(v, width=LANE):
    return jnp.pad(v, (0, width - v.shape[0])).reshape(1, width)


def kernel(x_prompt, x_sample, cache_attn_meta_k, cache_attn_meta_v, cache_attn_k, cache_attn_v, state_pool, state_ssm_conv, state_ssm, state_lru_conv, state_lru, meta_tokens, ln_in_g, ln_in_b, w_in, w_out, attn_sinks, pool_w, pool_scale, ssm_conv_w, ssm_conv_b, ssm_dt_bias, ssm_a_log, ssm_d, ssm_norm_g, lru_conv_w, lru_conv_b, lru_wr, lru_br, lru_wi, lru_bi, lru_lambda, ln1_g, ln1_b, ln2_g, ln2_b, router_w, router_bias, exp_w_gate, exp_w_up, exp_w_down, sh_w_gate, sh_w_up, sh_w_down):
    row = lambda v: v.reshape(1, -1).astype(F32)
    h = ln_in(x_prompt.reshape(P_ROWS, D_MODEL), x_sample.reshape(S_ROWS, D_MODEL),
              meta_tokens.astype(F32), row(ln_in_g), row(ln_in_b))
    p_states, s_states = [], []
    for i in range(DEPTH):
        wi = w_in[i]
        s0 = 0
        parts = {}
        for name, size in zip(('q', 'k', 'v', 'u', 'z', 'xbc', 'dt', 'rx', 'rg'),
                              (512, 128, 128, 512, 512, 1024, 8, 512, 512)):
            parts[name] = wi[:, s0:s0 + size]
            s0 += size
        w_in_p = jnp.concatenate(
            [parts[n] for n in ('q', 'u', 'z', 'rx', 'rg', 'xbc', 'k', 'v', 'dt')]
            + [jnp.zeros((D_MODEL, PROJ_P - C_DT - C_HEADS), F32)], axis=1).astype(BF16)
        rw = jnp.pad(router_w[i].astype(F32), ((0, 0), (0, LANE - N_EXPERTS)))
        rw_hi = rw.astype(BF16)
        rw_lo = (rw - rw_hi.astype(F32)).astype(BF16)
        cw = ssm_conv_w[i].astype(F32)
        cb = ssm_conv_b[i].astype(F32)
        lw = dict(
            w_in=w_in_p, w_out=w_out[i].astype(BF16), sinks=attn_sinks[i].astype(F32),
            pool_w=pool_w[i].astype(BF16), pool_scale=row(pool_scale[i]),
            ssd=dict(cw_x=cw[:, :512], cw_b=cw[:, 512:768], cw_c=cw[:, 768:],
                     cb_x=row(cb[:512]), cb_b=row(cb[512:768]), cb_c=row(cb[768:]),
                     dt_bias=_pad_lanes(ssm_dt_bias[i].astype(F32)),
                     a_log=_pad_lanes(ssm_a_log[i].astype(F32)),
                     d_skip=row(jnp.repeat(ssm_d[i].astype(F32), 64)),
                     norm_g=row(ssm_norm_g[i])),
            lru=dict(cw=lru_conv_w[i].astype(F32), cb=row(lru_conv_b[i]),
                     wr=_block_diag(lru_wr[i]).astype(BF16), wi=_block_diag(lru_wi[i]).astype(BF16),
                     br=row(lru_br[i]), bi=row(lru_bi[i]), lam=row(lru_lambda[i])),
            ln1_g=row(ln1_g[i]), ln1_b=row(ln1_b[i]), ln2_g=row(ln2_g[i]), ln2_b=row(ln2_b[i]),
            rw_hi=rw_hi, rw_lo=rw_lo, router_bias=router_bias[i].astype(F32),
            wg=exp_w_gate[i], wu=exp_w_up[i], wd=exp_w_down[i],
            sh_wg=sh_w_gate[i].astype(BF16), sh_wu=sh_w_up[i].astype(BF16),
            sh_wd=sh_w_down[i].astype(BF16),
        )
        st = dict(
            meta_k=cache_attn_meta_k[i].reshape(DEC_BATCH * N_META, KV_W),
            meta_v=cache_attn_meta_v[i].reshape(DEC_BATCH * N_META, KV_W),
            win_k=cache_attn_k[i].reshape(DEC_BATCH * WINDOW, KV_W),
            win_v=cache_attn_v[i].reshape(DEC_BATCH * WINDOW, KV_W),
            pool=_pad_rows(state_pool[i], 16),
            ssm_conv=_pad_rows(state_ssm_conv[i], 8),
            ssm=state_ssm[i].reshape(DEC_BATCH, 512, C_STATE),
            lru_conv=_pad_rows(state_lru_conv[i], 8),
            lru=state_lru[i].reshape(DEC_BATCH, 1, GROUP_W),
        )
        h, ps, ss = _layer(h, st, lw)
        p_states.append(ps)
        s_states.append(ss)
    stk = lambda sts, j: jnp.stack([s[j] for s in sts])
    y_prompt = h[:P_ROWS].reshape(BATCH, SEQ, D_MODEL)
    y_sample = h[S_ROW0:S_ROW0 + S_ROWS].reshape(DEC_BATCH, DEC_SEQ, D_MODEL)
    return ((y_prompt, y_sample)
            + tuple(stk(p_states, j) for j in range(9))
            + tuple(stk(s_states, j) for j in range(7)))
```

```python
import functools
import math

import jax
import jax.numpy as jnp
from jax import lax
from jax.experimental import pallas as pl
from jax.experimental.pallas import tpu as pltpu

F32 = jnp.float32
BF16 = jnp.bfloat16

D_MODEL = 2048
BATCH = 4
SEQ = 4096
DEPTH = 4
DEC_BATCH = 16
DEC_SEQ = 64
N_META = 16
GROUP_W = 512
HEAD_DIM = 64
A_HEADS = 8
A_KV_HEADS = 2
A_GROUP = 4
KV_W = 128
WINDOW = 128
POOL_SIZES = (2, 4, 8, 16)
POOL_GW = 128
POOL_STATE = 15
C_HEADS = 8
C_STATE = 128
C_CONV_CH = 1024
CONV_W = 4
D_BLOCKS = 8
D_BLOCK_W = 64
LRU_C = 8.0
N_EXPERTS = 64
TOP_K = 8
N_GROUP = 8
TOPK_GROUP = 4
D_EXPERT = 512
ROUTE_SCALE = 2.5
ALPHA = (2 * DEPTH) ** 0.25
LN_EPS = 1e-5

LANE = 128
SUBLANE = 8
VMEM_LIMIT = 56 * 1024 * 1024

P_ROWS = BATCH * SEQ
S_ROWS = DEC_BATCH * DEC_SEQ
M_ROWS = BATCH * N_META
S_ROW0 = P_ROWS
M_ROW0 = P_ROWS + S_ROWS
NT = P_ROWS + S_ROWS + M_ROWS
TM = 512
N_TILES = -(-NT // TM)

C_Q, C_U, C_Z, C_RX, C_RG, C_XS, C_B, C_C, C_K, C_V, C_DT = (
    0, 512, 1024, 1536, 2048, 2560, 3072, 3328, 3584, 3712, 3840)
PROJ_P = 4096
PROJ_TN = 1024

MOE_T = 256
NK = NT * TOP_K
MOE_NB = (NK + N_EXPERTS * (MOE_T - 1) + MOE_T - 1) // MOE_T


def _cparams(sem):
    return pltpu.CompilerParams(dimension_semantics=sem, vmem_limit_bytes=VMEM_LIMIT)


def _ln(x, g, b):
    mu = jnp.mean(x, axis=-1, keepdims=True)
    xc = x - mu
    var = jnp.mean(xc * xc, axis=-1, keepdims=True)
    return xc * lax.rsqrt(var + LN_EPS) * g + b


def _dot(a, b):
    return jnp.dot(a, b, preferred_element_type=F32)


def _dot_nt(a, b, precision=None):
    return lax.dot_general(a, b, (((1,), (1,)), ((), ())), precision=precision,
                           preferred_element_type=F32)


def _full(shape):
    nd = len(shape)
    return pl.BlockSpec(shape, lambda *_: (0,) * nd)


def _ln_in_kernel(xp_ref, xs_ref, meta_ref, g_ref, b_ref, o_ref):
    i = pl.program_id(0)
    g = g_ref[...]
    b = b_ref[...]
    n_p = P_ROWS // TM
    n_s = S_ROWS // TM

    @pl.when(i < n_p)
    def _():
        o_ref[...] = _ln(xp_ref[...], g, b)

    @pl.when((i >= n_p) & (i < n_p + n_s))
    def _():
        o_ref[...] = _ln(xs_ref[...], g, b)

    @pl.when(i == n_p + n_s)
    def _():
        m = _ln(meta_ref[...], g, b)
        for r in range(BATCH):
            o_ref[N_META * r:N_META * (r + 1), :] = m


def ln_in(xp2, xs2, meta, g, b):
    n_p = P_ROWS // TM
    n_s = S_ROWS // TM
    return pl.pallas_call(
        _ln_in_kernel,
        grid=(N_TILES,),
        in_specs=[
            pl.BlockSpec((TM, D_MODEL), lambda i: (jnp.minimum(i, n_p - 1), 0)),
            pl.BlockSpec((TM, D_MODEL), lambda i: (jnp.clip(i - n_p, 0, n_s - 1), 0)),
            _full((N_META, D_MODEL)),
            _full((1, D_MODEL)),
            _full((1, D_MODEL)),
        ],
        out_specs=pl.BlockSpec((TM, D_MODEL), lambda i: (i, 0)),
        out_shape=jax.ShapeDtypeStruct((NT, D_MODEL), F32),
        compiler_params=_cparams(("arbitrary",)),
        name="ln_in",
    )(xp2, xs2, meta, g, b)


def _in_proj_kernel(x_ref, w_ref, o_ref):
    o_ref[...] = _dot(x_ref[...].astype(BF16), w_ref[...])


def in_proj(h, w_bf):
    return pl.pallas_call(
        _in_proj_kernel,
        grid=(N_TILES, PROJ_P // PROJ_TN),
        in_specs=[
            pl.BlockSpec((TM, D_MODEL), lambda i, n: (i, 0)),
            pl.BlockSpec((D_MODEL, PROJ_TN), lambda i, n: (0, n)),
        ],
        out_specs=pl.BlockSpec((TM, PROJ_TN), lambda i, n: (i, n)),
        out_shape=jax.ShapeDtypeStruct((NT, PROJ_P), F32),
        compiler_params=_cparams(("arbitrary", "arbitrary")),
        name="in_proj",
    )(h, w_bf)


ATT_PAD = 64
ATT_WIN = ATT_PAD + WINDOW + 64


def _attn_kernel(sink_ref, q_ref, kc_ref, vc_ref, kp_ref, vp_ref, km_ref, vm_ref, o_ref,
                 kbuf, vbuf, *, ch, nq, use_meta, chunk0):
    j = pl.program_id(1)
    tq = ch * nq
    zpad = jnp.zeros((ATT_PAD - N_META, KV_W), BF16)
    kbuf[0:N_META, :] = km_ref[...].astype(BF16)
    kbuf[N_META:ATT_PAD, :] = zpad
    vbuf[0:N_META, :] = vm_ref[...].astype(BF16)
    vbuf[N_META:ATT_PAD, :] = zpad
    kbuf[ATT_PAD:ATT_PAD + WINDOW, :] = kp_ref[...].astype(BF16)
    vbuf[ATT_PAD:ATT_PAD + WINDOW, :] = vp_ref[...].astype(BF16)
    kbuf[ATT_PAD + WINDOW:ATT_PAD + WINDOW + tq, :] = kc_ref[...].astype(BF16)
    vbuf[ATT_PAD + WINDOW:ATT_PAD + WINDOW + tq, :] = vc_ref[...].astype(BF16)
    if ch < 64:
        zc = jnp.zeros((64 - ch, KV_W), BF16)
        kbuf[ATT_PAD + WINDOW + tq:ATT_PAD + WINDOW + tq + 64 - ch, :] = zc
        vbuf[ATT_PAD + WINDOW + tq:ATT_PAD + WINDOW + tq + 64 - ch, :] = zc

    rows = A_GROUP * ch
    col = lax.broadcasted_iota(jnp.int32, (rows, ATT_WIN), 1)
    row = lax.broadcasted_iota(jnp.int32, (rows, 1), 0)
    lane = lax.broadcasted_iota(jnp.int32, (ch, LANE), 1)
    lo = lane < HEAD_DIM

    for i in range(nq):
        c = chunk0 + j * nq + i
        first_band = ATT_PAD + 64 * jnp.maximum(2 - c, 0)
        valid = (col >= first_band) & (col < ATT_PAD + WINDOW + ch)
        if use_meta:
            valid = valid | (col < N_META)
        kcat = jnp.concatenate(
            [kbuf[0:ATT_PAD, :], kbuf[ATT_PAD + ch * i:ATT_PAD + ch * i + WINDOW + 64, :]], axis=0)
        vcat = jnp.concatenate(
            [vbuf[0:ATT_PAD, :], vbuf[ATT_PAD + ch * i:ATT_PAD + ch * i + WINDOW + 64, :]], axis=0)
        qi = q_ref[ch * i:ch * (i + 1), :] * (HEAD_DIM ** -0.5)
        tiles = [qi[:, LANE * t:LANE * (t + 1)] for t in range(A_HEADS // 2)]
        out_tiles = [None] * (A_HEADS // 2)
        for kh in range(A_KV_HEADS):
            qs = []
            for r in range(A_GROUP):
                h = A_GROUP * kh + r
                t = tiles[h // 2]
                if h % 2 != kh:
                    t = pltpu.roll(t, HEAD_DIM, axis=1)
                keep = lo if kh == 0 else jnp.logical_not(lo)
                qs.append(jnp.where(keep, t, 0.0).astype(BF16))
            qz = jnp.concatenate(qs, axis=0)
            s = _dot_nt(qz, kcat)
            s = jnp.where(valid, s, -jnp.inf)
            sink = jnp.zeros((rows, 1), F32)
            for r in range(A_GROUP):
                sink = jnp.where((row >= r * ch) & (row < (r + 1) * ch),
                                 sink_ref[A_GROUP * kh + r], sink)
            m = jnp.maximum(jnp.max(s, axis=-1, keepdims=True), sink)
            p = jnp.exp(s - m)
            den = jnp.sum(p, axis=-1, keepdims=True) + jnp.exp(sink - m)
            probs = (p / den).astype(BF16)
            o = _dot(probs, vcat)
            for r in range(A_GROUP):
                h = A_GROUP * kh + r
                oh = o[r * ch:(r + 1) * ch, :]
                if h % 2 != kh:
                    oh = pltpu.roll(oh, HEAD_DIM, axis=1)
                keep = lo if h % 2 == 0 else jnp.logical_not(lo)
                prev = out_tiles[h // 2]
                out_tiles[h // 2] = jnp.where(keep, oh, 0.0 if prev is None else prev)
        o_ref[ch * i:ch * (i + 1), :] = jnp.concatenate(out_tiles, axis=1)


def attn_call(sinks, q_src, kc_src, kp_src, km_src, *, nb, length, ch, nq, use_meta, chunk0,
              q_row0, kp_map, km_map, kp_cols, km_cols):
    tq = ch * nq
    nj = length // tq
    qb0 = q_row0 // tq
    kern = functools.partial(_attn_kernel, ch=ch, nq=nq, use_meta=use_meta, chunk0=chunk0)
    return pl.pallas_call(
        kern,
        grid=(nb, nj),
        in_specs=[
            pl.BlockSpec(memory_space=pltpu.SMEM),
            pl.BlockSpec((tq, GROUP_W), lambda b, j: (qb0 + b * nj + j, C_Q // GROUP_W)),
            pl.BlockSpec((tq, KV_W), lambda b, j: (qb0 + b * nj + j, C_K // KV_W)),
            pl.BlockSpec((tq, KV_W), lambda b, j: (qb0 + b * nj + j, C_V // KV_W)),
            pl.BlockSpec((WINDOW, KV_W), lambda b, j: (kp_map(b, j), kp_cols[0])),
            pl.BlockSpec((WINDOW, KV_W), lambda b, j: (kp_map(b, j), kp_cols[1])),
            pl.BlockSpec((N_META, KV_W), lambda b, j: (km_map(b, j), km_cols[0])),
            pl.BlockSpec((N_META, KV_W), lambda b, j: (km_map(b, j), km_cols[1])),
        ],
        out_specs=pl.BlockSpec((tq, GROUP_W), lambda b, j: (b * nj + j, 0)),
        out_shape=jax.ShapeDtypeStruct((nb * length, GROUP_W), F32),
        scratch_shapes=[pltpu.VMEM((ATT_PAD + WINDOW + tq + 64, KV_W), BF16),
                        pltpu.VMEM((ATT_PAD + WINDOW + tq + 64, KV_W), BF16)],
        compiler_params=_cparams(("arbitrary", "arbitrary")),
        name="attn",
    )(sinks, q_src, kc_src, kc_src, kp_src[0], kp_src[1], km_src[0], km_src[1])


def _pool_kernel(u_ref, prev_ref, w_ref, scale_ref, o_ref, buf, *, tb, ramp):
    j = pl.program_id(1)

    @pl.when(j == 0)
    def _():
        buf[0:16, :] = prev_ref[...]

    buf[16:16 + tb, :] = u_ref[...]
    pos = j * tb + lax.broadcasted_iota(jnp.int32, (tb, 1), 0)
    outs = []
    for g, win in enumerate(POOL_SIZES):
        sl = slice(g * POOL_GW, (g + 1) * POOL_GW)
        tot = buf[16:16 + tb, sl]
        for k in range(1, win):
            tot = tot + buf[16 - k:16 - k + tb, sl]
        if ramp:
            cnt = jnp.minimum(win, pos + 1).astype(F32)
            mean = tot / cnt
        else:
            mean = tot * (1.0 / win)
        d = mean - buf[16:16 + tb, sl]
        outs.append(_dot(d.astype(BF16), w_ref[g]))
    o_ref[...] = jnp.concatenate(outs, axis=1) * scale_ref[...]
    buf[0:16, :] = buf[tb:tb + 16, :]


def pool_call(proj, prev_src, prev_map, prev_col, w_bf, scale, *, nb, length, tb, row0, ramp):
    nj = length // tb
    rb0 = row0 // tb
    kern = functools.partial(_pool_kernel, tb=tb, ramp=ramp)
    return pl.pallas_call(
        kern,
        grid=(nb, nj),
        in_specs=[
            pl.BlockSpec((tb, GROUP_W), lambda b, j: (rb0 + b * nj + j, C_U // GROUP_W)),
            pl.BlockSpec((16, GROUP_W), lambda b, j: (prev_map(b), prev_col)),
            _full((4, POOL_GW, POOL_GW)),
            _full((1, GROUP_W)),
        ],
        out_specs=pl.BlockSpec((tb, GROUP_W), lambda b, j: (b * nj + j, 0)),
        out_shape=jax.ShapeDtypeStruct((nb * length, GROUP_W), F32),
        scratch_shapes=[pltpu.VMEM((tb + 16, GROUP_W), F32)],
        compiler_params=_cparams(("arbitrary", "arbitrary")),
        name="pool",
    )(proj, prev_src, w_bf, scale)


def _conv_block(buf, x_ref, w_ref, b_ref, tb):
    buf[8:8 + tb, :] = x_ref[...]
    acc = b_ref[...] + buf[5:5 + tb, :] * w_ref[0:1, :]
    for k in range(1, CONV_W):
        acc = acc + buf[5 + k:5 + k + tb, :] * w_ref[k:k + 1, :]
    return acc


def _conv_carry(buf, tb):
    buf[0:8, :] = buf[tb:tb + 8, :]


def _col(x, h):
    lane = lax.broadcasted_iota(jnp.int32, x.shape, 1)
    return jnp.sum(jnp.where(lane == h, x, 0.0), axis=1, keepdims=True)


def _ssd_kernel(xs_ref, bm_ref, cm_ref, dt_ref, z_ref, px_ref, pb_ref, pc_ref, h0_ref,
                wx_ref, wb_ref, wc_ref, bx_ref, bb_ref, bc_ref,
                dtb_ref, alog_ref, dskip_ref, ng_ref,
                y_ref, hout_ref, bufx, bufb, bufc, hst, *, q):
    j = pl.program_id(1)
    nj = pl.num_programs(1)
    hi = lax.Precision.HIGHEST

    @pl.when(j == 0)
    def _():
        bufx[0:8, :] = px_ref[...]
        bufb[0:8, :] = pb_ref[...]
        bufc[0:8, :] = pc_ref[...]
        hst[...] = h0_ref[...]

    xs = jax.nn.silu(_conv_block(bufx, xs_ref, wx_ref, bx_ref, q))
    bm = jax.nn.silu(_conv_block(bufb, bm_ref, wb_ref, bb_ref, q))
    cm = jax.nn.silu(_conv_block(bufc, cm_ref, wc_ref, bc_ref, q))
    _conv_carry(bufx, q)
    _conv_carry(bufb, q)
    _conv_carry(bufc, q)

    lane1 = lax.broadcasted_iota(jnp.int32, (1, LANE), 1)
    hmask = lane1 < C_HEADS
    dt = jnp.where(hmask, jax.nn.softplus(dt_ref[...] + dtb_ref[...]), 0.0)
    a = jnp.where(hmask, -jnp.exp(alog_ref[...]), 0.0)
    dta = dt * a
    ri = lax.broadcasted_iota(jnp.int32, (q, q), 0)
    ci = lax.broadcasted_iota(jnp.int32, (q, q), 1)
    tri = ri >= ci
    cum = jnp.dot(tri.astype(F32), dta, precision=hi, preferred_element_type=F32)
    eye = (lax.broadcasted_iota(jnp.int32, (LANE, LANE), 0)
           == lax.broadcasted_iota(jnp.int32, (LANE, LANE), 1)).astype(F32)
    cum_t = _dot_nt(eye, cum, precision=hi)
    dt_t = _dot_nt(eye, dt, precision=hi)
    ecum = jnp.exp(cum)
    cum_last = cum[q - 1:q, :]
    te = jnp.exp(cum_last - cum) * dt

    lane = lax.broadcasted_iota(jnp.int32, (q, LANE), 1)
    lo = lane < 64
    bm_bf = bm.astype(BF16)
    cm_bf = cm.astype(BF16)
    cb = [_dot_nt(cm_bf[:, LANE * g:LANE * (g + 1)], bm_bf[:, LANE * g:LANE * (g + 1)])
          for g in range(2)]
    yoff = [_dot_nt(cm_bf[:, LANE * g:LANE * (g + 1)], hst[256 * g:256 * (g + 1), :].astype(BF16))
            for g in range(2)]

    y_tiles = []
    xw_tiles = []
    for k in range(C_HEADS // 2):
        g = k // 2
        x_pair = xs[:, LANE * k:LANE * (k + 1)]
        ydiag = None
        for par in range(2):
            h = 2 * k + par
            seg = _col(cum, h) - cum_t[h:h + 1, :]
            lm = jnp.exp(jnp.where(tri, seg, -jnp.inf))
            mm = (cb[g] * lm * dt_t[h:h + 1, :]).astype(BF16)
            xm = jnp.where(lo if par == 0 else jnp.logical_not(lo), x_pair, 0.0).astype(BF16)
            part = _dot(mm, xm)
            ydiag = part if ydiag is None else ydiag + part
        e_pair = jnp.where(lo, _col(ecum, 2 * k), _col(ecum, 2 * k + 1))
        te_pair = jnp.where(lo, _col(te, 2 * k), _col(te, 2 * k + 1))
        kk = k % 2
        y_tiles.append(ydiag + yoff[g][:, LANE * kk:LANE * (kk + 1)] * e_pair
                       + dskip_ref[:, LANE * k:LANE * (k + 1)] * x_pair)
        xw_tiles.append((x_pair * te_pair).astype(BF16))

    eye2 = (lax.broadcasted_iota(jnp.int32, (256, 256), 0)
            == lax.broadcasted_iota(jnp.int32, (256, 256), 1)).astype(BF16)
    for g in range(2):
        xw = jnp.concatenate(xw_tiles[2 * g:2 * g + 2], axis=1)
        xw_t = _dot_nt(eye2, xw).astype(BF16)
        s_new = _dot(xw_t, bm_bf[:, LANE * g:LANE * (g + 1)])
        dec = jnp.concatenate(
            [jnp.broadcast_to(jnp.exp(cum_t[4 * g + r:4 * g + r + 1, q - 1:q]), (64, LANE))
             for r in range(4)], axis=0)
        hst[256 * g:256 * (g + 1), :] = dec * hst[256 * g:256 * (g + 1), :] + s_new

    y = jnp.concatenate(y_tiles, axis=1) * jax.nn.silu(z_ref[...])
    y = y * lax.rsqrt(jnp.mean(y * y, axis=-1, keepdims=True) + 1e-6) * ng_ref[...]
    y_ref[...] = y

    @pl.when(j == nj - 1)
    def _():
        hout_ref[...] = hst[...]


def ssd_call(proj, prev_src, prev_map, prev_cols, h0, lw, *, nb, length, q, row0):
    nj = length // q
    rb0 = row0 // q
    kern = functools.partial(_ssd_kernel, q=q)
    blk = lambda width, col: pl.BlockSpec((q, width), lambda b, j: (rb0 + b * nj + j, col // width))
    pblk = lambda width, col: pl.BlockSpec((8, width), lambda b, j: (prev_map(b), col))
    return pl.pallas_call(
        kern,
        grid=(nb, nj),
        in_specs=[
            blk(512, C_XS), blk(256, C_B), blk(256, C_C), blk(LANE, C_DT), blk(512, C_Z),
            pblk(512, prev_cols[0]), pblk(256, prev_cols[1]), pblk(256, prev_cols[2]),
            pl.BlockSpec((None, 512, C_STATE), lambda b, j: (b, 0, 0)),
            _full((CONV_W, 512)), _full((CONV_W, 256)), _full((CONV_W, 256)),
            _full((1, 512)), _full((1, 256)), _full((1, 256)),
            _full((1, LANE)), _full((1, LANE)), _full((1, 512)), _full((1, 512)),
        ],
        out_specs=[
            pl.BlockSpec((q, GROUP_W), lambda b, j: (b * nj + j, 0)),
            pl.BlockSpec((None, 512, C_STATE), lambda b, j: (b, 0, 0)),
        ],
        out_shape=[jax.ShapeDtypeStruct((nb * length, GROUP_W), F32),
                   jax.ShapeDtypeStruct((nb, 512, C_STATE), F32)],
        scratch_shapes=[pltpu.VMEM((q + 8, 512), F32), pltpu.VMEM((q + 8, 256), F32),
                        pltpu.VMEM((q + 8, 256), F32), pltpu.VMEM((512, C_STATE), F32)],
        compiler_params=_cparams(("arbitrary", "arbitrary")),
        name="ssd",
    )(proj, proj, proj, proj, proj, prev_src, prev_src, prev_src, h0,
      lw['cw_x'], lw['cw_b'], lw['cw_c'], lw['cb_x'], lw['cb_b'], lw['cb_c'],
      lw['dt_bias'], lw['a_log'], lw['d_skip'], lw['norm_g'])


def _lru_kernel(rx_ref, rg_ref, prev_ref, h0_ref, cw_ref, cb_ref, wr_ref, wi_ref,
                br_ref, bi_ref, lam_ref, y_ref, hout_ref, buf, hc, *, tb):
    j = pl.program_id(1)
    nj = pl.num_programs(1)

    @pl.when(j == 0)
    def _():
        buf[0:8, :] = prev_ref[...]
        hc[...] = jnp.broadcast_to(h0_ref[...], hc.shape)

    xc = _conv_block(buf, rx_ref, cw_ref, cb_ref, tb)
    _conv_carry(buf, tb)
    rs, gs = [], []
    for s in range(GROUP_W // LANE):
        xb = xc[:, LANE * s:LANE * (s + 1)].astype(BF16)
        rs.append(_dot(xb, wr_ref[s]))
        gs.append(_dot(xb, wi_ref[s]))
    r = jax.nn.sigmoid(jnp.concatenate(rs, axis=1) + br_ref[...])
    gi = jax.nn.sigmoid(jnp.concatenate(gs, axis=1) + bi_ref[...])
    log_a = -LRU_C * r * jax.nn.softplus(-lam_ref[...])
    a = jnp.exp(log_a)
    u = jnp.sqrt(jnp.maximum(1.0 - jnp.exp(2.0 * log_a), 0.0)) * (gi * xc)
    t = lax.broadcasted_iota(jnp.int32, (tb, 1), 0)
    d = 1
    while d < tb:
        a_sh = jnp.where(t >= d, pltpu.roll(a, d, axis=0), 1.0)
        u_sh = jnp.where(t >= d, pltpu.roll(u, d, axis=0), 0.0)
        u = a * u_sh + u
        a = a * a_sh
        d *= 2
    h = u + a * hc[0:1, :]
    y_ref[...] = h * jax.nn.gelu(rg_ref[...])
    hc[...] = jnp.broadcast_to(h[tb - 1:tb, :], hc.shape)

    @pl.when(j == nj - 1)
    def _():
        hout_ref[...] = h[tb - 1:tb, :]


def lru_call(proj, prev_src, prev_map, prev_col, h0, lw, *, nb, length, tb, row0):
    nj = length // tb
    rb0 = row0 // tb
    kern = functools.partial(_lru_kernel, tb=tb)
    return pl.pallas_call(
        kern,
        grid=(nb, nj),
        in_specs=[
            pl.BlockSpec((tb, GROUP_W), lambda b, j: (rb0 + b * nj + j, C_RX // GROUP_W)),
            pl.BlockSpec((tb, GROUP_W), lambda b, j: (rb0 + b * nj + j, C_RG // GROUP_W)),
            pl.BlockSpec((8, GROUP_W), lambda b, j: (prev_map(b), prev_col)),
            pl.BlockSpec((None, 1, GROUP_W), lambda b, j: (b, 0, 0)),
            _full((CONV_W, GROUP_W)), _full((1, GROUP_W)),
            _full((4, LANE, LANE)), _full((4, LANE, LANE)),
            _full((1, GROUP_W)), _full((1, GROUP_W)), _full((1, GROUP_W)),
        ],
        out_specs=[
            pl.BlockSpec((tb, GROUP_W), lambda b, j: (b * nj + j, 0)),
            pl.BlockSpec((None, 1, GROUP_W), lambda b, j: (b, 0, 0)),
        ],
        out_shape=[jax.ShapeDtypeStruct((nb * length, GROUP_W), F32),
                   jax.ShapeDtypeStruct((nb, 1, GROUP_W), F32)],
        scratch_shapes=[pltpu.VMEM((tb + 8, GROUP_W), F32), pltpu.VMEM((8, GROUP_W), F32)],
        compiler_params=_cparams(("arbitrary", "arbitrary")),
        name="lru",
    )(proj, proj, prev_src, h0, lw['cw'], lw['cb'], lw['wr'], lw['wi'],
      lw['br'], lw['bi'], lw['lam'])


HALF = D_MODEL // 2


def _pack_bf16_pairs(x):
    bits = lax.bitcast_convert_type(x.astype(BF16).astype(F32), jnp.uint32)
    return (bits[:, :HALF] >> 16) | (bits[:, HALF:] & jnp.uint32(0xFFFF0000))


def _unpack_bf16_pairs(w):
    lo = lax.bitcast_convert_type(w << 16, F32).astype(BF16)
    hi = lax.bitcast_convert_type(w & jnp.uint32(0xFFFF0000), F32).astype(BF16)
    return lo, hi


def _out_proj_kernel(mix_ref, h_ref, w_ref, g_ref, b_ref, rwh_ref, rwl_ref, x1_ref, xp_ref, sc_ref):
    y = _dot(mix_ref[...].astype(BF16), w_ref[...])
    x1 = _ln(ALPHA * h_ref[...] + y, g_ref[...], b_ref[...])
    x1_ref[...] = x1
    xp_ref[...] = _pack_bf16_pairs(x1)
    xh = x1.astype(BF16)
    xl = (x1 - xh.astype(F32)).astype(BF16)
    logits = _dot(xh, rwh_ref[...]) + (_dot(xl, rwh_ref[...]) + _dot(xh, rwl_ref[...]))
    sc_ref[...] = jax.nn.sigmoid(logits)


def out_proj(mix, h, w_bf, g, b, rw_hi, rw_lo):
    return pl.pallas_call(
        _out_proj_kernel,
        grid=(N_TILES,),
        in_specs=[
            pl.BlockSpec((TM, D_MODEL), lambda i: (i, 0)),
            pl.BlockSpec((TM, D_MODEL), lambda i: (i, 0)),
            _full((D_MODEL, D_MODEL)),
            _full((1, D_MODEL)), _full((1, D_MODEL)),
            _full((D_MODEL, LANE)), _full((D_MODEL, LANE)),
        ],
        out_specs=[pl.BlockSpec((TM, D_MODEL), lambda i: (i, 0)),
                   pl.BlockSpec((TM, HALF), lambda i: (i, 0)),
                   pl.BlockSpec((TM, LANE), lambda i: (i, 0))],
        out_shape=[jax.ShapeDtypeStruct((NT, D_MODEL), F32),
                   jax.ShapeDtypeStruct((NT, HALF), jnp.uint32),
                   jax.ShapeDtypeStruct((NT, LANE), F32)],
        compiler_params=_cparams(("arbitrary",)),
        name="out_proj",
    )(mix, h, w_bf, g, b, rw_hi, rw_lo)


_BIG = 4096


def _group_allreduce(x, lane, op):
    for sh in (1, 2, 4):
        up = pltpu.roll(x, sh, axis=1)
        dn = pltpu.roll(x, LANE - sh, axis=1)
        x = op(x, jnp.where((lane & sh) != 0, up, dn))
    return x


def _router_kernel(sc_ref, bias_ref, e_ref, r_ref, w_ref, cnt_ref, run):
    i = pl.program_id(0)

    @pl.when(i == 0)
    def _():
        run[...] = jnp.zeros_like(run)

    sc = sc_ref[...]
    lane = lax.broadcasted_iota(jnp.int32, (TM, LANE), 1)
    valid = lane < N_EXPERTS
    neg = -jnp.inf
    biased = jnp.where(valid, sc + bias_ref[...], neg)
    gmax = _group_allreduce(biased, lane, jnp.maximum)
    first = _group_allreduce(jnp.where(biased == gmax, lane, _BIG), lane, jnp.minimum)
    second = _group_allreduce(jnp.where(lane == first, neg, biased), lane, jnp.maximum)
    gs = jnp.where(valid, gmax + second, neg)
    grp = lane >> 3
    cand = jnp.full((TM, LANE), neg, F32)
    for _ in range(TOPK_GROUP):
        m = jnp.max(gs, axis=1, keepdims=True)
        g1 = jnp.min(jnp.where(gs == m, grp, _BIG), axis=1, keepdims=True)
        hit = grp == g1
        cand = jnp.where(hit, biased, cand)
        gs = jnp.where(hit, neg, gs)
    sel = jnp.zeros((TM, LANE), F32)
    e_out = jnp.zeros((TM, LANE), jnp.int32)
    w_out = jnp.zeros((TM, LANE), F32)
    idxs = []
    for k in range(TOP_K):
        m = jnp.max(cand, axis=1, keepdims=True)
        ik = jnp.min(jnp.where(cand == m, lane, _BIG), axis=1, keepdims=True)
        hit = lane == ik
        vk = jnp.sum(jnp.where(hit, sc, 0.0), axis=1, keepdims=True)
        sel = jnp.where(hit, 1.0, sel)
        cand = jnp.where(hit, neg, cand)
        e_out = jnp.where(lane == k, ik, e_out)
        w_out = jnp.where(lane == k, vk, w_out)
        idxs.append(ik)
    wsum = jnp.sum(w_out, axis=1, keepdims=True)
    w_ref[...] = w_out / wsum * ROUTE_SCALE
    e_ref[...] = e_out
    rowi = lax.broadcasted_iota(jnp.int32, (TM, 1), 0)
    sel = jnp.where(rowi < NT - i * TM, sel, 0.0)
    ri = lax.broadcasted_iota(jnp.int32, (TM, TM), 0)
    ci = lax.broadcasted_iota(jnp.int32, (TM, TM), 1)
    before = _dot((ri > ci).astype(BF16), sel.astype(BF16))
    rank = run[0:1, :] + before
    r_out = jnp.zeros((TM, LANE), F32)
    for k in range(TOP_K):
        rk = jnp.sum(jnp.where(lane == idxs[k], rank, 0.0), axis=1, keepdims=True)
        r_out = jnp.where(lane == k, rk, r_out)
    r_ref[...] = r_out.astype(jnp.int32)
    run[...] = jnp.broadcast_to(rank[TM - 1:TM, :] + sel[TM - 1:TM, :], run.shape)

    @pl.when(i == pl.num_programs(0) - 1)
    def _():
        cnt_ref[...] = run[...]


def router(scores, bias_row):
    return pl.pallas_call(
        _router_kernel,
        grid=(N_TILES,),
        in_specs=[pl.BlockSpec((TM, LANE), lambda i: (i, 0)), _full((1, LANE))],
        out_specs=[pl.BlockSpec((TM, LANE), lambda i: (i, 0)),
                   pl.BlockSpec((TM, LANE), lambda i: (i, 0)),
                   pl.BlockSpec((TM, LANE), lambda i: (i, 0)),
                   _full((8, LANE))],
        out_shape=[jax.ShapeDtypeStruct((NT, LANE), jnp.int32),
                   jax.ShapeDtypeStruct((NT, LANE), jnp.int32),
                   jax.ShapeDtypeStruct((NT, LANE), F32),
                   jax.ShapeDtypeStruct((8, LANE), F32)],
        scratch_shapes=[pltpu.VMEM((8, LANE), F32)],
        compiler_params=_cparams(("arbitrary",)),
        name="router",
    )(scores, bias_row)


def _dispatch_kernel(cnt_ref, pst_ref, dest_ref, x_hbm, zero_hbm, xg_hbm, sem, zsem):
    i = pl.program_id(0)
    n = jnp.minimum(TM, NT - i * TM)

    def row_copy(src_row, dst_row):
        return pltpu.make_async_copy(x_hbm.at[pl.ds(src_row, 1)], xg_hbm.at[pl.ds(dst_row, 1)], sem)

    def zero_copy(dst_row):
        return pltpu.make_async_copy(zero_hbm.at[pl.ds(0, 1)], xg_hbm.at[pl.ds(dst_row, 1)], zsem)

    @pl.when(i == 0)
    def _():
        def per_expert(e, carry):
            c = cnt_ref[e]
            npad = (MOE_T - c % MOE_T) % MOE_T
            base = pst_ref[e] + c

            def start(r, cc):
                zero_copy(base + r).start()
                return cc

            lax.fori_loop(0, npad, start, 0)

            def wait(r, cc):
                zero_copy(base + r).wait()
                return cc

            lax.fori_loop(0, npad, wait, 0)
            return carry

        lax.fori_loop(0, N_EXPERTS, per_expert, 0)

    def start(t, carry):
        for s in range(TOP_K):
            row_copy(i * TM + t, dest_ref[t * TOP_K + s]).start()
        return carry

    lax.fori_loop(0, n, start, 0)

    def wait(t, carry):
        for s in range(TOP_K):
            row_copy(i * TM + t, dest_ref[t * TOP_K + s]).wait()
        return carry

    lax.fori_loop(0, n, wait, 0)


def dispatch(counts, pstarts, dest_flat, x1p, zero_row):
    grid_spec = pltpu.PrefetchScalarGridSpec(
        num_scalar_prefetch=2,
        grid=(N_TILES,),
        in_specs=[
            pl.BlockSpec((TM * TOP_K,), lambda i, c, p: (i,), memory_space=pltpu.SMEM),
            pl.BlockSpec(memory_space=pl.ANY),
            pl.BlockSpec(memory_space=pl.ANY),
        ],
        out_specs=pl.BlockSpec(memory_space=pl.ANY),
        scratch_shapes=[pltpu.SemaphoreType.DMA(()), pltpu.SemaphoreType.DMA(())],
    )
    return pl.pallas_call(
        _dispatch_kernel,
        grid_spec=grid_spec,
        out_shape=jax.ShapeDtypeStruct((MOE_NB * MOE_T, HALF), jnp.uint32),
        compiler_params=_cparams(("arbitrary",)),
        name="dispatch",
    )(counts, pstarts, dest_flat, x1p, zero_row)


def _moe_kernel(blk_e_ref, nused_ref, x_ref, wg_ref, wu_ref, wd_ref, o_ref, wg_bf, wu_bf, wd_bf):
    i = pl.program_id(0)
    changed = jnp.logical_or(i == 0, blk_e_ref[i] != blk_e_ref[jnp.maximum(i - 1, 0)])

    @pl.when(jnp.logical_and(changed, i < nused_ref[0]))
    def _():
        wg_bf[...] = wg_ref[...].astype(BF16)
        wu_bf[...] = wu_ref[...].astype(BF16)
        wd_bf[...] = wd_ref[...].astype(BF16)

    @pl.when(i < nused_ref[0])
    def _():
        lo, hi = _unpack_bf16_pairs(x_ref[...])
        gate = _dot(lo, wg_bf[0:HALF, :]) + _dot(hi, wg_bf[HALF:D_MODEL, :])
        up = _dot(lo, wu_bf[0:HALF, :]) + _dot(hi, wu_bf[HALF:D_MODEL, :])
        hb = jax.nn.silu(gate) * up
        o_ref[...] = _dot(hb.astype(BF16), wd_bf[...])

    @pl.when(i >= nused_ref[0])
    def _():
        o_ref[...] = jnp.zeros_like(o_ref)


def moe_experts(blk_e, nused, xg, wg, wu, wd, layer):
    grid_spec = pltpu.PrefetchScalarGridSpec(
        num_scalar_prefetch=2,
        grid=(MOE_NB,),
        in_specs=[
            pl.BlockSpec((MOE_T, HALF), lambda i, be, nu: (jnp.minimum(i, nu[0] - 1), 0)),
            pl.BlockSpec((None, None, D_MODEL, D_EXPERT), lambda i, be, nu: (layer, be[i], 0, 0)),
            pl.BlockSpec((None, None, D_MODEL, D_EXPERT), lambda i, be, nu: (layer, be[i], 0, 0)),
            pl.BlockSpec((None, None, D_EXPERT, D_MODEL), lambda i, be, nu: (layer, be[i], 0, 0)),
        ],
        out_specs=pl.BlockSpec((MOE_T, D_MODEL), lambda i, be, nu: (i, 0)),
        scratch_shapes=[pltpu.VMEM((D_MODEL, D_EXPERT), BF16),
                        pltpu.VMEM((D_MODEL, D_EXPERT), BF16),
                        pltpu.VMEM((D_EXPERT, D_MODEL), BF16)],
    )
    return pl.pallas_call(
        _moe_kernel,
        grid_spec=grid_spec,
        out_shape=jax.ShapeDtypeStruct((MOE_NB * MOE_T, D_MODEL), F32),
        compiler_params=_cparams(("arbitrary",)),
        name="moe_experts",
    )(blk_e, nused, xg, wg, wu, wd)


TMC = 256
NC_TILES = -(-NT // TMC)
DEST_LEN = max(N_TILES * TM, NC_TILES * TMC) * TOP_K


def _combine_kernel(dest_ref, x1_ref, w_ref, yb_hbm, wg_ref, wu_ref, wd_ref, g_ref, b_ref, o_ref,
                    ybuf, sem):
    i = pl.program_id(0)
    n = jnp.minimum(TMC, NT - i * TMC)

    def row_copy(t, s):
        return pltpu.make_async_copy(yb_hbm.at[pl.ds(dest_ref[t * TOP_K + s], 1)],
                                     ybuf.at[s, pl.ds(t, 1)], sem)

    def start(t, carry):
        for s in range(TOP_K):
            row_copy(t, s).start()
        return carry

    lax.fori_loop(0, n, start, 0)

    x1 = x1_ref[...]
    xb = x1.astype(BF16)
    hb = jax.nn.silu(_dot(xb, wg_ref[...])) * _dot(xb, wu_ref[...])
    acc = _dot(hb.astype(BF16), wd_ref[...])

    def wait(t, carry):
        for s in range(TOP_K):
            row_copy(t, s).wait()
        return carry

    lax.fori_loop(0, n, wait, 0)

    w = w_ref[...]
    for s in range(TOP_K):
        acc = acc + _col(w, s) * ybuf[s]
    o_ref[...] = _ln(ALPHA * x1 + acc, g_ref[...], b_ref[...])


def combine_shared_ln2(dest_flat, x1, w, yb, wg, wu, wd, g, b):
    grid_spec = pl.GridSpec(
        grid=(NC_TILES,),
        in_specs=[
            pl.BlockSpec((TMC * TOP_K,), lambda i: (i,), memory_space=pltpu.SMEM),
            pl.BlockSpec((TMC, D_MODEL), lambda i: (i, 0)),
            pl.BlockSpec((TMC, LANE), lambda i: (i, 0)),
            pl.BlockSpec(memory_space=pl.ANY),
            _full((D_MODEL, D_EXPERT)), _full((D_MODEL, D_EXPERT)), _full((D_EXPERT, D_MODEL)),
            _full((1, D_MODEL)), _full((1, D_MODEL)),
        ],
        out_specs=pl.BlockSpec((TMC, D_MODEL), lambda i: (i, 0)),
        scratch_shapes=[pltpu.VMEM((TOP_K, TMC, D_MODEL), F32), pltpu.SemaphoreType.DMA(())],
    )
    return pl.pallas_call(
        _combine_kernel,
        grid_spec=grid_spec,
        out_shape=jax.ShapeDtypeStruct((NT, D_MODEL), F32),
        compiler_params=_cparams(("arbitrary",)),
        name="combine_shared_ln2",
    )(dest_flat, x1, w, yb, wg, wu, wd, g, b)


def _schedule(counts, e_sel, rank_sel):
    padded = (counts + MOE_T - 1) // MOE_T * MOE_T
    pends = jnp.cumsum(padded)
    pstarts = pends - padded
    onehot = e_sel[:, :, None] == jnp.arange(N_EXPERTS, dtype=jnp.int32)
    dest = rank_sel + jnp.sum(jnp.where(onehot, pstarts, 0), axis=-1)
    dest_flat = jnp.pad(dest.reshape(NK), (0, DEST_LEN - NK))
    blk_e = jnp.minimum(
        jnp.searchsorted(pends, jnp.arange(MOE_NB, dtype=jnp.int32) * MOE_T, side='right'),
        N_EXPERTS - 1).astype(jnp.int32)
    nused = (pends[-1] // MOE_T).astype(jnp.int32).reshape(1)
    return pstarts.astype(jnp.int32), dest_flat.astype(jnp.int32), blk_e, nused


def _pad_rows(x, rows):
    b, r, c = x.shape
    return jnp.pad(x, ((0, 0), (rows - r, 0), (0, 0))).reshape(b * rows, c)


def _layer(h, st, lw):
    proj = in_proj(h, lw['w_in'])
    sinks = lw['sinks']
    zeros16 = jnp.zeros((BATCH * 16, GROUP_W), F32)
    zeros8 = jnp.zeros((BATCH * 8, C_CONV_CH), F32)

    meta_blk = lambda b, j=None: M_ROW0 // N_META + b
    ya_m = attn_call(sinks, proj, proj, (proj, proj), (proj, proj), nb=BATCH, length=N_META,
                     ch=N_META, nq=1, use_meta=False, chunk0=0, q_row0=M_ROW0,
                     kp_map=lambda b, j: 0, km_map=meta_blk,
                     kp_cols=(C_K // KV_W, C_V // KV_W), km_cols=(C_K // KV_W, C_V // KV_W))
    nq_p = 4
    ya_p = attn_call(sinks, proj, proj, (proj, proj), (proj, proj), nb=BATCH, length=SEQ,
                     ch=64, nq=nq_p, use_meta=True, chunk0=0, q_row0=0,
                     kp_map=lambda b, j: jnp.maximum(b * (SEQ // WINDOW) + j * (64 * nq_p // WINDOW) - 1, 0),
                     km_map=meta_blk,
                     kp_cols=(C_K // KV_W, C_V // KV_W), km_cols=(C_K // KV_W, C_V // KV_W))
    ya_s = attn_call(sinks, proj, proj, (st['win_k'], st['win_v']), (st['meta_k'], st['meta_v']),
                     nb=DEC_BATCH, length=DEC_SEQ, ch=64, nq=1, use_meta=True, chunk0=2,
                     q_row0=S_ROW0, kp_map=lambda b, j: b, km_map=lambda b, j: b,
                     kp_cols=(0, 0), km_cols=(0, 0))

    yb_m = pool_call(proj, zeros16, lambda b: b, 0, lw['pool_w'], lw['pool_scale'],
                     nb=BATCH, length=N_META, tb=N_META, row0=M_ROW0, ramp=True)
    yb_p = pool_call(proj, proj, lambda b: M_ROW0 // 16 + b, C_U // GROUP_W, lw['pool_w'],
                     lw['pool_scale'], nb=BATCH, length=SEQ, tb=512, row0=0, ramp=False)
    yb_s = pool_call(proj, st['pool'], lambda b: b, 0, lw['pool_w'], lw['pool_scale'],
                     nb=DEC_BATCH, length=DEC_SEQ, tb=DEC_SEQ, row0=S_ROW0, ramp=False)

    ssd_w = lw['ssd']
    h0z = jnp.zeros((BATCH, 512, C_STATE), F32)
    yc_m, hc_m = ssd_call(proj, zeros8, lambda b: b, (0, 2, 3), h0z, ssd_w,
                          nb=BATCH, length=N_META, q=N_META, row0=M_ROW0)
    meta_tail = lambda b: (M_ROW0 + 8) // 8 + 2 * b
    yc_p, hc_p = ssd_call(proj, proj, meta_tail, (C_XS // 512, C_B // 256, C_C // 256), hc_m, ssd_w,
                          nb=BATCH, length=SEQ, q=256, row0=0)
    yc_s, hc_s = ssd_call(proj, st['ssm_conv'], lambda b: b, (0, 2, 3), st['ssm'], ssd_w,
                          nb=DEC_BATCH, length=DEC_SEQ, q=DEC_SEQ, row0=S_ROW0)

    lru_w = lw['lru']
    l0z = jnp.zeros((BATCH, 1, GROUP_W), F32)
    yd_m, hd_m = lru_call(proj, zeros8, lambda b: b, 0, l0z, lru_w,
                          nb=BATCH, length=N_META, tb=N_META, row0=M_ROW0)
    yd_p, hd_p = lru_call(proj, proj, meta_tail, C_RX // GROUP_W, hd_m, lru_w,
                          nb=BATCH, length=SEQ, tb=256, row0=0)
    yd_s, hd_s = lru_call(proj, st['lru_conv'], lambda b: b, 0, st['lru'], lru_w,
                          nb=DEC_BATCH, length=DEC_SEQ, tb=DEC_SEQ, row0=S_ROW0)

    mix = jnp.concatenate([
        jnp.concatenate([ya_p, yb_p, yc_p, yd_p], axis=1),
        jnp.concatenate([ya_s, yb_s, yc_s, yd_s], axis=1),
        jnp.concatenate([ya_m, yb_m, yc_m, yd_m], axis=1)], axis=0)

    x1, x1p, sc = out_proj(mix, h, lw['w_out'], lw['ln1_g'], lw['ln1_b'], lw['rw_hi'], lw['rw_lo'])

    e_sel, rank_sel, w_sel, cnt = router(sc, lw['router_bias'])
    counts = cnt[0, :N_EXPERTS].astype(jnp.int32)
    pstarts, dest_flat, blk_e, nused = _schedule(counts, e_sel[:, :TOP_K], rank_sel[:, :TOP_K])
    xg = dispatch(counts, pstarts, dest_flat, x1p, jnp.zeros((8, HALF), jnp.uint32))
    yb = moe_experts(blk_e, nused, xg, lw['wg'], lw['wu'], lw['wd'], lw['layer'])
    h_new = combine_shared_ln2(dest_flat, x1, w_sel, yb, lw['sh_wg'], lw['sh_wu'], lw['sh_wd'],
                               lw['ln2_g'], lw['ln2_b'])

    def tail(row0, nb, length, nrows, c0, width):
        return jnp.stack([proj[row0 + (b + 1) * length - nrows:row0 + (b + 1) * length, c0:c0 + width]
                          for b in range(nb)])

    kv4 = lambda x: x.reshape(x.shape[0], x.shape[1], A_KV_HEADS, HEAD_DIM)
    p_state = (
        kv4(tail(M_ROW0, BATCH, N_META, N_META, C_K, KV_W)),
        kv4(tail(M_ROW0, BATCH, N_META, N_META, C_V, KV_W)),
        kv4(tail(0, BATCH, SEQ, WINDOW, C_K, KV_W)), kv4(tail(0, BATCH, SEQ, WINDOW, C_V, KV_W)),
        tail(0, BATCH, SEQ, POOL_STATE, C_U, GROUP_W),
        tail(0, BATCH, SEQ, CONV_W - 1, C_XS, C_CONV_CH),
        hc_p.reshape(BATCH, C_HEADS, 64, C_STATE),
        tail(0, BATCH, SEQ, CONV_W - 1, C_RX, GROUP_W),
        hd_p.reshape(BATCH, GROUP_W),
    )
    s_state = (
        kv4(tail(S_ROW0, DEC_BATCH, DEC_SEQ, DEC_SEQ, C_K, KV_W)),
        kv4(tail(S_ROW0, DEC_BATCH, DEC_SEQ, DEC_SEQ, C_V, KV_W)),
        tail(S_ROW0, DEC_BATCH, DEC_SEQ, POOL_STATE, C_U, GROUP_W),
        tail(S_ROW0, DEC_BATCH, DEC_SEQ, CONV_W - 1, C_XS, C_CONV_CH),
        hc_s.reshape(DEC_BATCH, C_HEADS, 64, C_STATE),
        tail(S_ROW0, DEC_BATCH, DEC_SEQ, CONV_W - 1, C_RX, GROUP_W),
        hd_s.reshape(DEC_BATCH, GROUP_W),
    )
    return h_new, p_state, s_state


def _block_diag(w):
    z = jnp.zeros((D_BLOCK_W, D_BLOCK_W), w.dtype)
    return jnp.stack([jnp.block([[w[2 * s], z], [z, w[2 * s + 1]]]) for s in range(4)])


def _pad_lanes(v, width=LANE):
    return jnp.pad(v, (0, width - v.shape[0])).reshape(1, width)


def kernel(x_prompt, x_sample, cache_attn_meta_k, cache_attn_meta_v, cache_attn_k, cache_attn_v, state_pool, state_ssm_conv, state_ssm, state_lru_conv, state_lru, meta_tokens, ln_in_g, ln_in_b, w_in, w_out, attn_sinks, pool_w, pool_scale, ssm_conv_w, ssm_conv_b, ssm_dt_bias, ssm_a_log, ssm_d, ssm_norm_g, lru_conv_w, lru_conv_b, lru_wr, lru_br, lru_wi, lru_bi, lru_lambda, ln1_g, ln1_b, ln2_g, ln2_b, router_w, router_bias, exp_w_gate, exp_w_up, exp_w_down, sh_w_gate, sh_w_up, sh_w_down):
    row = lambda v: v.reshape(1, -1).astype(F32)
    h = ln_in(x_prompt.reshape(P_ROWS, D_MODEL), x_sample.reshape(S_ROWS, D_MODEL),
              meta_tokens.astype(F32), row(ln_in_g), row(ln_in_b))
    p_states, s_states = [], []
    for i in range(DEPTH):
        wi = w_in[i]
        s0 = 0
        parts = {}
        for name, size in zip(('q', 'k', 'v', 'u', 'z', 'xbc', 'dt', 'rx', 'rg'),
                              (512, 128, 128, 512, 512, 1024, 8, 512, 512)):
            parts[name] = wi[:, s0:s0 + size]
            s0 += size
        w_in_p = jnp.concatenate(
            [parts[n] for n in ('q', 'u', 'z', 'rx', 'rg', 'xbc', 'k', 'v', 'dt')]
            + [jnp.zeros((D_MODEL, PROJ_P - C_DT - C_HEADS), F32)], axis=1).astype(BF16)
        rw = jnp.pad(router_w[i].astype(F32), ((0, 0), (0, LANE - N_EXPERTS)))
        rw_hi = rw.astype(BF16)
        rw_lo = (rw - rw_hi.astype(F32)).astype(BF16)
        cw = ssm_conv_w[i].astype(F32)
        cb = ssm_conv_b[i].astype(F32)
        lw = dict(
            w_in=w_in_p, w_out=w_out[i].astype(BF16), sinks=attn_sinks[i].astype(F32),
            pool_w=pool_w[i].astype(BF16), pool_scale=row(pool_scale[i]),
            ssd=dict(cw_x=cw[:, :512], cw_b=cw[:, 512:768], cw_c=cw[:, 768:],
                     cb_x=row(cb[:512]), cb_b=row(cb[512:768]), cb_c=row(cb[768:]),
                     dt_bias=_pad_lanes(ssm_dt_bias[i].astype(F32)),
                     a_log=_pad_lanes(ssm_a_log[i].astype(F32)),
                     d_skip=row(jnp.repeat(ssm_d[i].astype(F32), 64)),
                     norm_g=row(ssm_norm_g[i])),
            lru=dict(cw=lru_conv_w[i].astype(F32), cb=row(lru_conv_b[i]),
                     wr=_block_diag(lru_wr[i]).astype(BF16), wi=_block_diag(lru_wi[i]).astype(BF16),
                     br=row(lru_br[i]), bi=row(lru_bi[i]), lam=row(lru_lambda[i])),
            ln1_g=row(ln1_g[i]), ln1_b=row(ln1_b[i]), ln2_g=row(ln2_g[i]), ln2_b=row(ln2_b[i]),
            rw_hi=rw_hi, rw_lo=rw_lo, router_bias=_pad_lanes(router_bias[i].astype(F32)),
            wg=exp_w_gate, wu=exp_w_up, wd=exp_w_down, layer=i,
            sh_wg=sh_w_gate[i].astype(BF16), sh_wu=sh_w_up[i].astype(BF16),
            sh_wd=sh_w_down[i].astype(BF16),
        )
        st = dict(
            meta_k=cache_attn_meta_k[i].reshape(DEC_BATCH * N_META, KV_W),
            meta_v=cache_attn_meta_v[i].reshape(DEC_BATCH * N_META, KV_W),
            win_k=cache_attn_k[i].reshape(DEC_BATCH * WINDOW, KV_W),
            win_v=cache_attn_v[i].reshape(DEC_BATCH * WINDOW, KV_W),
            pool=_pad_rows(state_pool[i], 16),
            ssm_conv=_pad_rows(state_ssm_conv[i], 8),
            ssm=state_ssm[i].reshape(DEC_BATCH, 512, C_STATE),
            lru_conv=_pad_rows(state_lru_conv[i], 8),
            lru=state_lru[i].reshape(DEC_BATCH, 1, GROUP_W),
        )
        h, ps, ss = _layer(h, st, lw)
        p_states.append(ps)
        s_states.append(ss)
    stk = lambda sts, j: jnp.stack([s[j] for s in sts])
    y_prompt = h[:P_ROWS].reshape(BATCH, SEQ, D_MODEL)
    y_sample = h[S_ROW0:S_ROW0 + S_ROWS].reshape(DEC_BATCH, DEC_SEQ, D_MODEL)
    return ((y_prompt, y_sample)
            + tuple(stk(p_states, j) for j in range(9))
            + tuple(stk(s_states, j) for j in range(7)))
```

```python
import functools
import math

import jax
import jax.numpy as jnp
from jax import lax
from jax.experimental import pallas as pl
from jax.experimental.pallas import tpu as pltpu

F32 = jnp.float32
BF16 = jnp.bfloat16

D_MODEL = 2048
BATCH = 4
SEQ = 4096
DEPTH = 4
DEC_BATCH = 16
DEC_SEQ = 64
N_META = 16
GROUP_W = 512
HEAD_DIM = 64
A_HEADS = 8
A_KV_HEADS = 2
A_GROUP = 4
KV_W = 128
WINDOW = 128
POOL_SIZES = (2, 4, 8, 16)
POOL_GW = 128
POOL_STATE = 15
C_HEADS = 8
C_STATE = 128
C_CONV_CH = 1024
CONV_W = 4
D_BLOCKS = 8
D_BLOCK_W = 64
LRU_C = 8.0
N_EXPERTS = 64
TOP_K = 8
N_GROUP = 8
TOPK_GROUP = 4
D_EXPERT = 512
ROUTE_SCALE = 2.5
ALPHA = (2 * DEPTH) ** 0.25
LN_EPS = 1e-5

LANE = 128
SUBLANE = 8
VMEM_LIMIT = 56 * 1024 * 1024

P_ROWS = BATCH * SEQ
S_ROWS = DEC_BATCH * DEC_SEQ
M_ROWS = BATCH * N_META
S_ROW0 = P_ROWS
M_ROW0 = P_ROWS + S_ROWS
NT = P_ROWS + S_ROWS + M_ROWS
TM = 512
N_TILES = -(-NT // TM)

C_Q, C_U, C_Z, C_RX, C_RG, C_XS, C_B, C_C, C_K, C_V, C_DT = (
    0, 512, 1024, 1536, 2048, 2560, 3072, 3328, 3584, 3712, 3840)
PROJ_P = 4096
PROJ_TN = 1024

MOE_T = 256
NK = NT * TOP_K
MOE_NB = (NK + N_EXPERTS * (MOE_T - 1) + MOE_T - 1) // MOE_T


def _cparams(sem):
    return pltpu.CompilerParams(dimension_semantics=sem, vmem_limit_bytes=VMEM_LIMIT)


def _ln(x, g, b):
    mu = jnp.mean(x, axis=-1, keepdims=True)
    xc = x - mu
    var = jnp.mean(xc * xc, axis=-1, keepdims=True)
    return xc * lax.rsqrt(var + LN_EPS) * g + b


def _dot(a, b):
    return jnp.dot(a, b, preferred_element_type=F32)


def _dot_nt(a, b, precision=None):
    return lax.dot_general(a, b, (((1,), (1,)), ((), ())), precision=precision,
                           preferred_element_type=F32)


def _full(shape):
    nd = len(shape)
    return pl.BlockSpec(shape, lambda *_: (0,) * nd)


MIX_W = 4 * GROUP_W


def _skip_ref(kern, idx, *refs):
    return kern(*refs[:idx], *refs[idx + 1:])


def _mixer_call(kern, grid, in_specs, args, y_spec, more_out_specs, more_out_shapes, scratch,
                mix, mix_rows, name):
    in_specs = list(in_specs)
    args = tuple(args)
    aliases = {}
    if mix is not None:
        n_in = len(in_specs)
        kern = functools.partial(_skip_ref, kern, n_in)
        in_specs.append(pl.BlockSpec(memory_space=pl.ANY))
        args = args + (mix,)
        aliases = {n_in: 0}
    return pl.pallas_call(
        kern,
        grid=grid,
        in_specs=in_specs,
        out_specs=[y_spec] + list(more_out_specs),
        out_shape=[jax.ShapeDtypeStruct((mix_rows, MIX_W), F32)] + list(more_out_shapes),
        scratch_shapes=scratch,
        input_output_aliases=aliases,
        compiler_params=_cparams(("arbitrary",) * len(grid)),
        name=name,
    )(*args)


def _ln_in_kernel(xp_ref, xs_ref, meta_ref, g_ref, b_ref, o_ref):
    i = pl.program_id(0)
    g = g_ref[...]
    b = b_ref[...]
    n_p = P_ROWS // TM
    n_s = S_ROWS // TM

    @pl.when(i < n_p)
    def _():
        o_ref[...] = _ln(xp_ref[...], g, b)

    @pl.when((i >= n_p) & (i < n_p + n_s))
    def _():
        o_ref[...] = _ln(xs_ref[...], g, b)

    @pl.when(i == n_p + n_s)
    def _():
        m = _ln(meta_ref[...], g, b)
        for r in range(BATCH):
            o_ref[N_META * r:N_META * (r + 1), :] = m


def ln_in(xp2, xs2, meta, g, b):
    n_p = P_ROWS // TM
    n_s = S_ROWS // TM
    return pl.pallas_call(
        _ln_in_kernel,
        grid=(N_TILES,),
        in_specs=[
            pl.BlockSpec((TM, D_MODEL), lambda i: (jnp.minimum(i, n_p - 1), 0)),
            pl.BlockSpec((TM, D_MODEL), lambda i: (jnp.clip(i - n_p, 0, n_s - 1), 0)),
            _full((N_META, D_MODEL)),
            _full((1, D_MODEL)),
            _full((1, D_MODEL)),
        ],
        out_specs=pl.BlockSpec((TM, D_MODEL), lambda i: (i, 0)),
        out_shape=jax.ShapeDtypeStruct((NT, D_MODEL), F32),
        compiler_params=_cparams(("arbitrary",)),
        name="ln_in",
    )(xp2, xs2, meta, g, b)


def _in_proj_kernel(x_ref, w_ref, o_ref):
    o_ref[...] = _dot(x_ref[...].astype(BF16), w_ref[...])


def in_proj(h, w_bf):
    return pl.pallas_call(
        _in_proj_kernel,
        grid=(N_TILES, PROJ_P // PROJ_TN),
        in_specs=[
            pl.BlockSpec((TM, D_MODEL), lambda i, n: (i, 0)),
            pl.BlockSpec((D_MODEL, PROJ_TN), lambda i, n: (0, n)),
        ],
        out_specs=pl.BlockSpec((TM, PROJ_TN), lambda i, n: (i, n)),
        out_shape=jax.ShapeDtypeStruct((NT, PROJ_P), F32),
        compiler_params=_cparams(("arbitrary", "arbitrary")),
        name="in_proj",
    )(h, w_bf)


ATT_PAD = 64
ATT_WIN = ATT_PAD + WINDOW + 64


def _attn_kernel(sink_ref, q_ref, kc_ref, vc_ref, kp_ref, vp_ref, km_ref, vm_ref, o_ref,
                 kbuf, vbuf, *, ch, nq, use_meta, chunk0):
    j = pl.program_id(1)
    tq = ch * nq
    zpad = jnp.zeros((ATT_PAD - N_META, KV_W), BF16)
    kbuf[0:N_META, :] = km_ref[...].astype(BF16)
    kbuf[N_META:ATT_PAD, :] = zpad
    vbuf[0:N_META, :] = vm_ref[...].astype(BF16)
    vbuf[N_META:ATT_PAD, :] = zpad
    kbuf[ATT_PAD:ATT_PAD + WINDOW, :] = kp_ref[...].astype(BF16)
    vbuf[ATT_PAD:ATT_PAD + WINDOW, :] = vp_ref[...].astype(BF16)
    kbuf[ATT_PAD + WINDOW:ATT_PAD + WINDOW + tq, :] = kc_ref[...].astype(BF16)
    vbuf[ATT_PAD + WINDOW:ATT_PAD + WINDOW + tq, :] = vc_ref[...].astype(BF16)
    if ch < 64:
        zc = jnp.zeros((64 - ch, KV_W), BF16)
        kbuf[ATT_PAD + WINDOW + tq:ATT_PAD + WINDOW + tq + 64 - ch, :] = zc
        vbuf[ATT_PAD + WINDOW + tq:ATT_PAD + WINDOW + tq + 64 - ch, :] = zc

    rows = A_GROUP * ch
    col = lax.broadcasted_iota(jnp.int32, (rows, ATT_WIN), 1)
    row = lax.broadcasted_iota(jnp.int32, (rows, 1), 0)
    lane = lax.broadcasted_iota(jnp.int32, (ch, LANE), 1)
    lo = lane < HEAD_DIM

    for i in range(nq):
        c = chunk0 + j * nq + i
        first_band = ATT_PAD + 64 * jnp.maximum(2 - c, 0)
        valid = (col >= first_band) & (col < ATT_PAD + WINDOW + ch)
        if use_meta:
            valid = valid | (col < N_META)
        kcat = jnp.concatenate(
            [kbuf[0:ATT_PAD, :], kbuf[ATT_PAD + ch * i:ATT_PAD + ch * i + WINDOW + 64, :]], axis=0)
        vcat = jnp.concatenate(
            [vbuf[0:ATT_PAD, :], vbuf[ATT_PAD + ch * i:ATT_PAD + ch * i + WINDOW + 64, :]], axis=0)
        qi = q_ref[ch * i:ch * (i + 1), :] * (HEAD_DIM ** -0.5)
        tiles = [qi[:, LANE * t:LANE * (t + 1)] for t in range(A_HEADS // 2)]
        out_tiles = [None] * (A_HEADS // 2)
        for kh in range(A_KV_HEADS):
            qs = []
            for r in range(A_GROUP):
                h = A_GROUP * kh + r
                t = tiles[h // 2]
                if h % 2 != kh:
                    t = pltpu.roll(t, HEAD_DIM, axis=1)
                keep = lo if kh == 0 else jnp.logical_not(lo)
                qs.append(jnp.where(keep, t, 0.0).astype(BF16))
            qz = jnp.concatenate(qs, axis=0)
            s = _dot_nt(qz, kcat)
            s = jnp.where(valid, s, -jnp.inf)
            sink = jnp.zeros((rows, 1), F32)
            for r in range(A_GROUP):
                sink = jnp.where((row >= r * ch) & (row < (r + 1) * ch),
                                 sink_ref[A_GROUP * kh + r], sink)
            m = jnp.maximum(jnp.max(s, axis=-1, keepdims=True), sink)
            p = jnp.exp(s - m)
            den = jnp.sum(p, axis=-1, keepdims=True) + jnp.exp(sink - m)
            probs = (p / den).astype(BF16)
            o = _dot(probs, vcat)
            for r in range(A_GROUP):
                h = A_GROUP * kh + r
                oh = o[r * ch:(r + 1) * ch, :]
                if h % 2 != kh:
                    oh = pltpu.roll(oh, HEAD_DIM, axis=1)
                keep = lo if h % 2 == 0 else jnp.logical_not(lo)
                prev = out_tiles[h // 2]
                out_tiles[h // 2] = jnp.where(keep, oh, 0.0 if prev is None else prev)
        o_ref[ch * i:ch * (i + 1), :] = jnp.concatenate(out_tiles, axis=1)


def attn_call(sinks, q_src, kc_src, kp_src, km_src, *, nb, length, ch, nq, use_meta, chunk0,
              q_row0, kp_map, km_map, kp_cols, km_cols, mix, mix_rows):
    tq = ch * nq
    nj = length // tq
    qb0 = q_row0 // tq
    kern = functools.partial(_attn_kernel, ch=ch, nq=nq, use_meta=use_meta, chunk0=chunk0)
    in_specs = [
        pl.BlockSpec(memory_space=pltpu.SMEM),
        pl.BlockSpec((tq, GROUP_W), lambda b, j: (qb0 + b * nj + j, C_Q // GROUP_W)),
        pl.BlockSpec((tq, KV_W), lambda b, j: (qb0 + b * nj + j, C_K // KV_W)),
        pl.BlockSpec((tq, KV_W), lambda b, j: (qb0 + b * nj + j, C_V // KV_W)),
        pl.BlockSpec((WINDOW, KV_W), lambda b, j: (kp_map(b, j), kp_cols[0])),
        pl.BlockSpec((WINDOW, KV_W), lambda b, j: (kp_map(b, j), kp_cols[1])),
        pl.BlockSpec((N_META, KV_W), lambda b, j: (km_map(b, j), km_cols[0])),
        pl.BlockSpec((N_META, KV_W), lambda b, j: (km_map(b, j), km_cols[1])),
    ]
    return _mixer_call(
        kern, (nb, nj), in_specs,
        (sinks, q_src, kc_src, kc_src, kp_src[0], kp_src[1], km_src[0], km_src[1]),
        pl.BlockSpec((tq, GROUP_W), lambda b, j: (qb0 + b * nj + j, 0)), [], [],
        [pltpu.VMEM((ATT_PAD + WINDOW + tq + 64, KV_W), BF16),
         pltpu.VMEM((ATT_PAD + WINDOW + tq + 64, KV_W), BF16)],
        mix, mix_rows, "attn")[0]


def _pool_kernel(u_ref, prev_ref, w_ref, scale_ref, o_ref, buf, *, tb, ramp):
    j = pl.program_id(1)

    @pl.when(j == 0)
    def _():
        buf[0:16, :] = prev_ref[...]

    buf[16:16 + tb, :] = u_ref[...]
    pos = j * tb + lax.broadcasted_iota(jnp.int32, (tb, 1), 0)
    outs = []
    for g, win in enumerate(POOL_SIZES):
        sl = slice(g * POOL_GW, (g + 1) * POOL_GW)
        tot = buf[16:16 + tb, sl]
        for k in range(1, win):
            tot = tot + buf[16 - k:16 - k + tb, sl]
        if ramp:
            cnt = jnp.minimum(win, pos + 1).astype(F32)
            mean = tot / cnt
        else:
            mean = tot * (1.0 / win)
        d = mean - buf[16:16 + tb, sl]
        outs.append(_dot(d.astype(BF16), w_ref[g]))
    o_ref[...] = jnp.concatenate(outs, axis=1) * scale_ref[...]
    buf[0:16, :] = buf[tb:tb + 16, :]


def pool_call(proj, prev_src, prev_map, prev_col, w_bf, scale, *, nb, length, tb, row0, ramp,
              mix, mix_rows):
    nj = length // tb
    rb0 = row0 // tb
    kern = functools.partial(_pool_kernel, tb=tb, ramp=ramp)
    in_specs = [
        pl.BlockSpec((tb, GROUP_W), lambda b, j: (rb0 + b * nj + j, C_U // GROUP_W)),
        pl.BlockSpec((16, GROUP_W), lambda b, j: (prev_map(b), prev_col)),
        _full((4, POOL_GW, POOL_GW)),
        _full((1, GROUP_W)),
    ]
    return _mixer_call(
        kern, (nb, nj), in_specs, (proj, prev_src, w_bf, scale),
        pl.BlockSpec((tb, GROUP_W), lambda b, j: (rb0 + b * nj + j, 1)), [], [],
        [pltpu.VMEM((tb + 16, GROUP_W), F32)], mix, mix_rows, "pool")[0]


def _conv_block(buf, x_ref, w_ref, b_ref, tb):
    buf[8:8 + tb, :] = x_ref[...]
    acc = b_ref[...] + buf[5:5 + tb, :] * w_ref[0:1, :]
    for k in range(1, CONV_W):
        acc = acc + buf[5 + k:5 + k + tb, :] * w_ref[k:k + 1, :]
    return acc


def _conv_carry(buf, tb):
    buf[0:8, :] = buf[tb:tb + 8, :]


def _col(x, h):
    lane = lax.broadcasted_iota(jnp.int32, x.shape, 1)
    return jnp.sum(jnp.where(lane == h, x, 0.0), axis=1, keepdims=True)


def _ssd_kernel(xs_ref, bm_ref, cm_ref, dt_ref, z_ref, px_ref, pb_ref, pc_ref, h0_ref,
                wx_ref, wb_ref, wc_ref, bx_ref, bb_ref, bc_ref,
                dtb_ref, alog_ref, dskip_ref, ng_ref,
                y_ref, hout_ref, bufx, bufb, bufc, hst, *, q):
    j = pl.program_id(1)
    nj = pl.num_programs(1)
    hi = lax.Precision.HIGHEST

    @pl.when(j == 0)
    def _():
        bufx[0:8, :] = px_ref[...]
        bufb[0:8, :] = pb_ref[...]
        bufc[0:8, :] = pc_ref[...]
        hst[...] = h0_ref[...]

    xs = jax.nn.silu(_conv_block(bufx, xs_ref, wx_ref, bx_ref, q))
    bm = jax.nn.silu(_conv_block(bufb, bm_ref, wb_ref, bb_ref, q))
    cm = jax.nn.silu(_conv_block(bufc, cm_ref, wc_ref, bc_ref, q))
    _conv_carry(bufx, q)
    _conv_carry(bufb, q)
    _conv_carry(bufc, q)

    lane1 = lax.broadcasted_iota(jnp.int32, (1, LANE), 1)
    hmask = lane1 < C_HEADS
    dt = jnp.where(hmask, jax.nn.softplus(dt_ref[...] + dtb_ref[...]), 0.0)
    a = jnp.where(hmask, -jnp.exp(alog_ref[...]), 0.0)
    dta = dt * a
    ri = lax.broadcasted_iota(jnp.int32, (q, q), 0)
    ci = lax.broadcasted_iota(jnp.int32, (q, q), 1)
    tri = ri >= ci
    cum = jnp.dot(tri.astype(F32), dta, precision=hi, preferred_element_type=F32)
    eye = (lax.broadcasted_iota(jnp.int32, (LANE, LANE), 0)
           == lax.broadcasted_iota(jnp.int32, (LANE, LANE), 1)).astype(F32)
    cum_t = _dot_nt(eye, cum, precision=hi)
    dt_t = _dot_nt(eye, dt, precision=hi)
    ecum = jnp.exp(cum)
    cum_last = cum[q - 1:q, :]
    te = jnp.exp(cum_last - cum) * dt

    lane = lax.broadcasted_iota(jnp.int32, (q, LANE), 1)
    lo = lane < 64
    bm_bf = bm.astype(BF16)
    cm_bf = cm.astype(BF16)
    cb = [_dot_nt(cm_bf[:, LANE * g:LANE * (g + 1)], bm_bf[:, LANE * g:LANE * (g + 1)])
          for g in range(2)]
    yoff = [_dot_nt(cm_bf[:, LANE * g:LANE * (g + 1)], hst[256 * g:256 * (g + 1), :].astype(BF16))
            for g in range(2)]

    y_tiles = []
    xw_tiles = []
    for k in range(C_HEADS // 2):
        g = k // 2
        x_pair = xs[:, LANE * k:LANE * (k + 1)]
        ydiag = None
        for par in range(2):
            h = 2 * k + par
            seg = _col(cum, h) - cum_t[h:h + 1, :]
            lm = jnp.exp(jnp.where(tri, seg, -jnp.inf))
            mm = (cb[g] * lm * dt_t[h:h + 1, :]).astype(BF16)
            xm = jnp.where(lo if par == 0 else jnp.logical_not(lo), x_pair, 0.0).astype(BF16)
            part = _dot(mm, xm)
            ydiag = part if ydiag is None else ydiag + part
        e_pair = jnp.where(lo, _col(ecum, 2 * k), _col(ecum, 2 * k + 1))
        te_pair = jnp.where(lo, _col(te, 2 * k), _col(te, 2 * k + 1))
        kk = k % 2
        y_tiles.append(ydiag + yoff[g][:, LANE * kk:LANE * (kk + 1)] * e_pair
                       + dskip_ref[:, LANE * k:LANE * (k + 1)] * x_pair)
        xw_tiles.append((x_pair * te_pair).astype(BF16))

    eye2 = (lax.broadcasted_iota(jnp.int32, (256, 256), 0)
            == lax.broadcasted_iota(jnp.int32, (256, 256), 1)).astype(BF16)
    for g in range(2):
        xw = jnp.concatenate(xw_tiles[2 * g:2 * g + 2], axis=1)
        xw_t = _dot_nt(eye2, xw).astype(BF16)
        s_new = _dot(xw_t, bm_bf[:, LANE * g:LANE * (g + 1)])
        dec = jnp.concatenate(
            [jnp.broadcast_to(jnp.exp(cum_t[4 * g + r:4 * g + r + 1, q - 1:q]), (64, LANE))
             for r in range(4)], axis=0)
        hst[256 * g:256 * (g + 1), :] = dec * hst[256 * g:256 * (g + 1), :] + s_new

    y = jnp.concatenate(y_tiles, axis=1) * jax.nn.silu(z_ref[...])
    y = y * lax.rsqrt(jnp.mean(y * y, axis=-1, keepdims=True) + 1e-6) * ng_ref[...]
    y_ref[...] = y

    @pl.when(j == nj - 1)
    def _():
        hout_ref[...] = hst[...]


def ssd_call(proj, prev_src, prev_map, prev_cols, h0, lw, *, nb, length, q, row0, mix, mix_rows):
    nj = length // q
    rb0 = row0 // q
    kern = functools.partial(_ssd_kernel, q=q)
    blk = lambda width, col: pl.BlockSpec((q, width), lambda b, j: (rb0 + b * nj + j, col // width))
    pblk = lambda width, col: pl.BlockSpec((8, width), lambda b, j: (prev_map(b), col))
    in_specs = [
        blk(512, C_XS), blk(256, C_B), blk(256, C_C), blk(LANE, C_DT), blk(512, C_Z),
        pblk(512, prev_cols[0]), pblk(256, prev_cols[1]), pblk(256, prev_cols[2]),
        pl.BlockSpec((None, 512, C_STATE), lambda b, j: (b, 0, 0)),
        _full((CONV_W, 512)), _full((CONV_W, 256)), _full((CONV_W, 256)),
        _full((1, 512)), _full((1, 256)), _full((1, 256)),
        _full((1, LANE)), _full((1, LANE)), _full((1, 512)), _full((1, 512)),
    ]
    args = (proj, proj, proj, proj, proj, prev_src, prev_src, prev_src, h0,
            lw['cw_x'], lw['cw_b'], lw['cw_c'], lw['cb_x'], lw['cb_b'], lw['cb_c'],
            lw['dt_bias'], lw['a_log'], lw['d_skip'], lw['norm_g'])
    return _mixer_call(
        kern, (nb, nj), in_specs, args,
        pl.BlockSpec((q, GROUP_W), lambda b, j: (rb0 + b * nj + j, 2)),
        [pl.BlockSpec((None, 512, C_STATE), lambda b, j: (b, 0, 0))],
        [jax.ShapeDtypeStruct((nb, 512, C_STATE), F32)],
        [pltpu.VMEM((q + 8, 512), F32), pltpu.VMEM((q + 8, 256), F32),
         pltpu.VMEM((q + 8, 256), F32), pltpu.VMEM((512, C_STATE), F32)],
        mix, mix_rows, "ssd")


def _lru_kernel(rx_ref, rg_ref, prev_ref, h0_ref, cw_ref, cb_ref, wr_ref, wi_ref,
                br_ref, bi_ref, lam_ref, y_ref, hout_ref, buf, hc, *, tb):
    j = pl.program_id(1)
    nj = pl.num_programs(1)

    @pl.when(j == 0)
    def _():
        buf[0:8, :] = prev_ref[...]
        hc[...] = jnp.broadcast_to(h0_ref[...], hc.shape)

    xc = _conv_block(buf, rx_ref, cw_ref, cb_ref, tb)
    _conv_carry(buf, tb)
    rs, gs = [], []
    for s in range(GROUP_W // LANE):
        xb = xc[:, LANE * s:LANE * (s + 1)].astype(BF16)
        rs.append(_dot(xb, wr_ref[s]))
        gs.append(_dot(xb, wi_ref[s]))
    r = jax.nn.sigmoid(jnp.concatenate(rs, axis=1) + br_ref[...])
    gi = jax.nn.sigmoid(jnp.concatenate(gs, axis=1) + bi_ref[...])
    log_a = -LRU_C * r * jax.nn.softplus(-lam_ref[...])
    a = jnp.exp(log_a)
    u = jnp.sqrt(jnp.maximum(1.0 - jnp.exp(2.0 * log_a), 0.0)) * (gi * xc)
    t = lax.broadcasted_iota(jnp.int32, (tb, 1), 0)
    d = 1
    while d < tb:
        a_sh = jnp.where(t >= d, pltpu.roll(a, d, axis=0), 1.0)
        u_sh = jnp.where(t >= d, pltpu.roll(u, d, axis=0), 0.0)
        u = a * u_sh + u
        a = a * a_sh
        d *= 2
    h = u + a * hc[0:1, :]
    y_ref[...] = h * jax.nn.gelu(rg_ref[...])
    hc[...] = jnp.broadcast_to(h[tb - 1:tb, :], hc.shape)

    @pl.when(j == nj - 1)
    def _():
        hout_ref[...] = h[tb - 1:tb, :]


def lru_call(proj, prev_src, prev_map, prev_col, h0, lw, *, nb, length, tb, row0, mix, mix_rows):
    nj = length // tb
    rb0 = row0 // tb
    kern = functools.partial(_lru_kernel, tb=tb)
    in_specs = [
        pl.BlockSpec((tb, GROUP_W), lambda b, j: (rb0 + b * nj + j, C_RX // GROUP_W)),
        pl.BlockSpec((tb, GROUP_W), lambda b, j: (rb0 + b * nj + j, C_RG // GROUP_W)),
        pl.BlockSpec((8, GROUP_W), lambda b, j: (prev_map(b), prev_col)),
        pl.BlockSpec((None, 1, GROUP_W), lambda b, j: (b, 0, 0)),
        _full((CONV_W, GROUP_W)), _full((1, GROUP_W)),
        _full((4, LANE, LANE)), _full((4, LANE, LANE)),
        _full((1, GROUP_W)), _full((1, GROUP_W)), _full((1, GROUP_W)),
    ]
    args = (proj, proj, prev_src, h0, lw['cw'], lw['cb'], lw['wr'], lw['wi'],
            lw['br'], lw['bi'], lw['lam'])
    return _mixer_call(
        kern, (nb, nj), in_specs, args,
        pl.BlockSpec((tb, GROUP_W), lambda b, j: (rb0 + b * nj + j, 3)),
        [pl.BlockSpec((None, 1, GROUP_W), lambda b, j: (b, 0, 0))],
        [jax.ShapeDtypeStruct((nb, 1, GROUP_W), F32)],
        [pltpu.VMEM((tb + 8, GROUP_W), F32), pltpu.VMEM((8, GROUP_W), F32)],
        mix, mix_rows, "lru")


HALF = D_MODEL // 2


def _pack_bf16_pairs(x):
    bits = lax.bitcast_convert_type(x.astype(BF16).astype(F32), jnp.uint32)
    return (bits[:, :HALF] >> 16) | (bits[:, HALF:] & jnp.uint32(0xFFFF0000))


def _unpack_bf16_pairs(w):
    lo = lax.bitcast_convert_type(w << 16, F32).astype(BF16)
    hi = lax.bitcast_convert_type(w & jnp.uint32(0xFFFF0000), F32).astype(BF16)
    return lo, hi


def _out_proj_kernel(mix_ref, h_ref, w_ref, g_ref, b_ref, rwh_ref, rwl_ref, x1_ref, xp_ref, sc_ref):
    y = _dot(mix_ref[...].astype(BF16), w_ref[...])
    x1 = _ln(ALPHA * h_ref[...] + y, g_ref[...], b_ref[...])
    x1_ref[...] = x1
    xp_ref[...] = _pack_bf16_pairs(x1)
    xh = x1.astype(BF16)
    xl = (x1 - xh.astype(F32)).astype(BF16)
    logits = _dot(xh, rwh_ref[...]) + (_dot(xl, rwh_ref[...]) + _dot(xh, rwl_ref[...]))
    sc_ref[...] = jax.nn.sigmoid(logits)


def out_proj(mix, h, w_bf, g, b, rw_hi, rw_lo):
    return pl.pallas_call(
        _out_proj_kernel,
        grid=(N_TILES,),
        in_specs=[
            pl.BlockSpec((TM, D_MODEL), lambda i: (i, 0)),
            pl.BlockSpec((TM, D_MODEL), lambda i: (i, 0)),
            _full((D_MODEL, D_MODEL)),
            _full((1, D_MODEL)), _full((1, D_MODEL)),
            _full((D_MODEL, LANE)), _full((D_MODEL, LANE)),
        ],
        out_specs=[pl.BlockSpec((TM, D_MODEL), lambda i: (i, 0)),
                   pl.BlockSpec((TM, HALF), lambda i: (i, 0)),
                   pl.BlockSpec((TM, LANE), lambda i: (i, 0))],
        out_shape=[jax.ShapeDtypeStruct((NT, D_MODEL), F32),
                   jax.ShapeDtypeStruct((NT, HALF), jnp.uint32),
                   jax.ShapeDtypeStruct((NT, LANE), F32)],
        compiler_params=_cparams(("arbitrary",)),
        name="out_proj",
    )(mix, h, w_bf, g, b, rw_hi, rw_lo)


_BIG = 4096


def _group_allreduce(x, lane, op):
    for sh in (1, 2, 4):
        up = pltpu.roll(x, sh, axis=1)
        dn = pltpu.roll(x, LANE - sh, axis=1)
        x = op(x, jnp.where((lane & sh) != 0, up, dn))
    return x


def _router_kernel(sc_ref, bias_ref, e_ref, r_ref, w_ref, cnt_ref, run):
    i = pl.program_id(0)

    @pl.when(i == 0)
    def _():
        run[...] = jnp.zeros_like(run)

    sc = sc_ref[...]
    lane = lax.broadcasted_iota(jnp.int32, (TM, LANE), 1)
    valid = lane < N_EXPERTS
    neg = -jnp.inf
    biased = jnp.where(valid, sc + bias_ref[...], neg)
    gmax = _group_allreduce(biased, lane, jnp.maximum)
    first = _group_allreduce(jnp.where(biased == gmax, lane, _BIG), lane, jnp.minimum)
    second = _group_allreduce(jnp.where(lane == first, neg, biased), lane, jnp.maximum)
    gs = jnp.where(valid, gmax + second, neg)
    grp = lane >> 3
    cand = jnp.full((TM, LANE), neg, F32)
    for _ in range(TOPK_GROUP):
        m = jnp.max(gs, axis=1, keepdims=True)
        g1 = jnp.min(jnp.where(gs == m, grp, _BIG), axis=1, keepdims=True)
        hit = grp == g1
        cand = jnp.where(hit, biased, cand)
        gs = jnp.where(hit, neg, gs)
    sel = jnp.zeros((TM, LANE), F32)
    e_out = jnp.zeros((TM, LANE), jnp.int32)
    w_out = jnp.zeros((TM, LANE), F32)
    idxs = []
    for k in range(TOP_K):
        m = jnp.max(cand, axis=1, keepdims=True)
        ik = jnp.min(jnp.where(cand == m, lane, _BIG), axis=1, keepdims=True)
        hit = lane == ik
        vk = jnp.sum(jnp.where(hit, sc, 0.0), axis=1, keepdims=True)
        sel = jnp.where(hit, 1.0, sel)
        cand = jnp.where(hit, neg, cand)
        e_out = jnp.where(lane == k, ik, e_out)
        w_out = jnp.where(lane == k, vk, w_out)
        idxs.append(ik)
    wsum = jnp.sum(w_out, axis=1, keepdims=True)
    w_ref[...] = w_out / wsum * ROUTE_SCALE
    e_ref[...] = e_out
    rowi = lax.broadcasted_iota(jnp.int32, (TM, 1), 0)
    sel = jnp.where(rowi < NT - i * TM, sel, 0.0)
    ri = lax.broadcasted_iota(jnp.int32, (TM, TM), 0)
    ci = lax.broadcasted_iota(jnp.int32, (TM, TM), 1)
    before = _dot((ri > ci).astype(BF16), sel.astype(BF16))
    rank = run[0:1, :] + before
    r_out = jnp.zeros((TM, LANE), F32)
    for k in range(TOP_K):
        rk = jnp.sum(jnp.where(lane == idxs[k], rank, 0.0), axis=1, keepdims=True)
        r_out = jnp.where(lane == k, rk, r_out)
    r_ref[...] = r_out.astype(jnp.int32)
    run[...] = jnp.broadcast_to(rank[TM - 1:TM, :] + sel[TM - 1:TM, :], run.shape)

    @pl.when(i == pl.num_programs(0) - 1)
    def _():
        cnt_ref[...] = run[...]


def router(scores, bias_row):
    return pl.pallas_call(
        _router_kernel,
        grid=(N_TILES,),
        in_specs=[pl.BlockSpec((TM, LANE), lambda i: (i, 0)), _full((1, LANE))],
        out_specs=[pl.BlockSpec((TM, LANE), lambda i: (i, 0)),
                   pl.BlockSpec((TM, LANE), lambda i: (i, 0)),
                   pl.BlockSpec((TM, LANE), lambda i: (i, 0)),
                   _full((8, LANE))],
        out_shape=[jax.ShapeDtypeStruct((NT, LANE), jnp.int32),
                   jax.ShapeDtypeStruct((NT, LANE), jnp.int32),
                   jax.ShapeDtypeStruct((NT, LANE), F32),
                   jax.ShapeDtypeStruct((8, LANE), F32)],
        scratch_shapes=[pltpu.VMEM((8, LANE), F32)],
        compiler_params=_cparams(("arbitrary",)),
        name="router",
    )(scores, bias_row)


def _dispatch_kernel(cnt_ref, pst_ref, dest_ref, x_ref, xg_hbm, zbuf, sem, zsem):
    i = pl.program_id(0)
    n = jnp.minimum(TM, NT - i * TM)

    def row_copy(src_row, dst_row):
        return pltpu.make_async_copy(x_ref.at[pl.ds(src_row, 1)], xg_hbm.at[pl.ds(dst_row, 1)], sem)

    def zero_copy(dst_row):
        return pltpu.make_async_copy(zbuf.at[pl.ds(0, 1)], xg_hbm.at[pl.ds(dst_row, 1)], zsem)

    @pl.when(i == 0)
    def _():
        zbuf[...] = jnp.zeros_like(zbuf)

        def per_expert(e, carry):
            c = cnt_ref[e]
            npad = (MOE_T - c % MOE_T) % MOE_T
            base = pst_ref[e] + c

            def start(r, cc):
                zero_copy(base + r).start()
                return cc

            lax.fori_loop(0, npad, start, 0)

            def wait(r, cc):
                zero_copy(base + r).wait()
                return cc

            lax.fori_loop(0, npad, wait, 0)
            return carry

        lax.fori_loop(0, N_EXPERTS, per_expert, 0)

    def start(t, carry):
        for s in range(TOP_K):
            row_copy(t, dest_ref[t * TOP_K + s]).start()
        return carry

    lax.fori_loop(0, n, start, 0)

    @pl.when(n == TM)
    def _():
        for s in range(TOP_K):
            pltpu.make_async_copy(x_ref, x_ref, sem).wait()

    @pl.when(n < TM)
    def _():
        def wait(t, carry):
            for s in range(TOP_K):
                row_copy(t, dest_ref[t * TOP_K + s]).wait()
            return carry

        lax.fori_loop(0, n, wait, 0)


def dispatch(counts, pstarts, dest_flat, x1p):
    grid_spec = pltpu.PrefetchScalarGridSpec(
        num_scalar_prefetch=2,
        grid=(N_TILES,),
        in_specs=[
            pl.BlockSpec((TM * TOP_K,), lambda i, c, p: (i,), memory_space=pltpu.SMEM),
            pl.BlockSpec((TM, HALF), lambda i, c, p: (i, 0)),
        ],
        out_specs=pl.BlockSpec(memory_space=pl.ANY),
        scratch_shapes=[pltpu.VMEM((8, HALF), jnp.uint32),
                        pltpu.SemaphoreType.DMA(()), pltpu.SemaphoreType.DMA(())],
    )
    return pl.pallas_call(
        _dispatch_kernel,
        grid_spec=grid_spec,
        out_shape=jax.ShapeDtypeStruct((MOE_NB * MOE_T, HALF), jnp.uint32),
        compiler_params=_cparams(("arbitrary",)),
        name="dispatch",
    )(counts, pstarts, dest_flat, x1p)


def _moe_kernel(blk_e_ref, nused_ref, x_ref, wg_ref, wu_ref, wd_ref, o_ref, wg_bf, wu_bf, wd_bf):
    i = pl.program_id(0)
    changed = jnp.logical_or(i == 0, blk_e_ref[i] != blk_e_ref[jnp.maximum(i - 1, 0)])

    @pl.when(jnp.logical_and(changed, i < nused_ref[0]))
    def _():
        wg_bf[...] = wg_ref[...].astype(BF16)
        wu_bf[...] = wu_ref[...].astype(BF16)
        wd_bf[...] = wd_ref[...].astype(BF16)

    @pl.when(i < nused_ref[0])
    def _():
        lo, hi = _unpack_bf16_pairs(x_ref[...])
        gate = _dot(lo, wg_bf[0:HALF, :]) + _dot(hi, wg_bf[HALF:D_MODEL, :])
        up = _dot(lo, wu_bf[0:HALF, :]) + _dot(hi, wu_bf[HALF:D_MODEL, :])
        hb = jax.nn.silu(gate) * up
        o_ref[...] = _dot(hb.astype(BF16), wd_bf[...])

    @pl.when(i >= nused_ref[0])
    def _():
        o_ref[...] = jnp.zeros_like(o_ref)


def moe_experts(blk_e, nused, xg, wg, wu, wd, layer):
    grid_spec = pltpu.PrefetchScalarGridSpec(
        num_scalar_prefetch=2,
        grid=(MOE_NB,),
        in_specs=[
            pl.BlockSpec((MOE_T, HALF), lambda i, be, nu: (jnp.minimum(i, nu[0] - 1), 0)),
            pl.BlockSpec((None, None, D_MODEL, D_EXPERT), lambda i, be, nu: (layer, be[i], 0, 0)),
            pl.BlockSpec((None, None, D_MODEL, D_EXPERT), lambda i, be, nu: (layer, be[i], 0, 0)),
            pl.BlockSpec((None, None, D_EXPERT, D_MODEL), lambda i, be, nu: (layer, be[i], 0, 0)),
        ],
        out_specs=pl.BlockSpec((MOE_T, D_MODEL), lambda i, be, nu: (i, 0)),
        scratch_shapes=[pltpu.VMEM((D_MODEL, D_EXPERT), BF16),
                        pltpu.VMEM((D_MODEL, D_EXPERT), BF16),
                        pltpu.VMEM((D_EXPERT, D_MODEL), BF16)],
    )
    return pl.pallas_call(
        _moe_kernel,
        grid_spec=grid_spec,
        out_shape=jax.ShapeDtypeStruct((MOE_NB * MOE_T, D_MODEL), F32),
        compiler_params=_cparams(("arbitrary",)),
        name="moe_experts",
    )(blk_e, nused, xg, wg, wu, wd)


TMC = 256
NC_TILES = -(-NT // TMC)
DEST_LEN = max(N_TILES * TM, NC_TILES * TMC) * TOP_K


def _combine_kernel(dest_ref, x1_ref, w_ref, yb_hbm, wg_ref, wu_ref, wd_ref, g_ref, b_ref, o_ref,
                    ybuf, sem):
    i = pl.program_id(0)
    n = jnp.minimum(TMC, NT - i * TMC)

    def row_copy(t, s):
        return pltpu.make_async_copy(yb_hbm.at[pl.ds(dest_ref[t * TOP_K + s], 1)],
                                     ybuf.at[s, pl.ds(t, 1)], sem)

    def start(t, carry):
        for s in range(TOP_K):
            row_copy(t, s).start()
        return carry

    lax.fori_loop(0, n, start, 0)

    x1 = x1_ref[...]
    xb = x1.astype(BF16)
    hb = jax.nn.silu(_dot(xb, wg_ref[...])) * _dot(xb, wu_ref[...])
    acc = _dot(hb.astype(BF16), wd_ref[...])

    @pl.when(n == TMC)
    def _():
        pltpu.make_async_copy(ybuf, ybuf, sem).wait()

    @pl.when(n < TMC)
    def _():
        def wait(t, carry):
            for s in range(TOP_K):
                row_copy(t, s).wait()
            return carry

        lax.fori_loop(0, n, wait, 0)

    w = w_ref[...]
    for s in range(TOP_K):
        acc = acc + _col(w, s) * ybuf[s]
    o_ref[...] = _ln(ALPHA * x1 + acc, g_ref[...], b_ref[...])


def combine_shared_ln2(dest_flat, x1, w, yb, wg, wu, wd, g, b):
    grid_spec = pl.GridSpec(
        grid=(NC_TILES,),
        in_specs=[
            pl.BlockSpec((TMC * TOP_K,), lambda i: (i,), memory_space=pltpu.SMEM),
            pl.BlockSpec((TMC, D_MODEL), lambda i: (i, 0)),
            pl.BlockSpec((TMC, LANE), lambda i: (i, 0)),
            pl.BlockSpec(memory_space=pl.ANY),
            _full((D_MODEL, D_EXPERT)), _full((D_MODEL, D_EXPERT)), _full((D_EXPERT, D_MODEL)),
            _full((1, D_MODEL)), _full((1, D_MODEL)),
        ],
        out_specs=pl.BlockSpec((TMC, D_MODEL), lambda i: (i, 0)),
        scratch_shapes=[pltpu.VMEM((TOP_K, TMC, D_MODEL), F32), pltpu.SemaphoreType.DMA(())],
    )
    return pl.pallas_call(
        _combine_kernel,
        grid_spec=grid_spec,
        out_shape=jax.ShapeDtypeStruct((NT, D_MODEL), F32),
        compiler_params=_cparams(("arbitrary",)),
        name="combine_shared_ln2",
    )(dest_flat, x1, w, yb, wg, wu, wd, g, b)


def _schedule(counts, e_sel, rank_sel):
    padded = (counts + MOE_T - 1) // MOE_T * MOE_T
    pends = jnp.cumsum(padded)
    pstarts = pends - padded
    onehot = e_sel[:, :, None] == jnp.arange(N_EXPERTS, dtype=jnp.int32)
    dest = rank_sel + jnp.sum(jnp.where(onehot, pstarts, 0), axis=-1)
    dest_flat = jnp.pad(dest.reshape(NK), (0, DEST_LEN - NK))
    blk_row0 = jnp.arange(MOE_NB, dtype=jnp.int32) * MOE_T
    blk_e = jnp.minimum(jnp.sum((pends[None, :] <= blk_row0[:, None]).astype(jnp.int32), axis=1),
                        N_EXPERTS - 1).astype(jnp.int32)
    nused = (pends[-1] // MOE_T).astype(jnp.int32).reshape(1)
    return pstarts.astype(jnp.int32), dest_flat.astype(jnp.int32), blk_e, nused


def _pad_rows(x, rows):
    b, r, c = x.shape
    return jnp.pad(x, ((0, 0), (rows - r, 0), (0, 0))).reshape(b * rows, c)


def _layer(h, st, lw):
    proj = in_proj(h, lw['w_in'])
    sinks = lw['sinks']
    zeros16 = jnp.zeros((BATCH * 16, GROUP_W), F32)
    zeros8 = jnp.zeros((BATCH * 8, C_CONV_CH), F32)

    meta_blk = lambda b, j=None: M_ROW0 // N_META + b
    mix = attn_call(sinks, proj, proj, (proj, proj), (proj, proj), nb=BATCH, length=N_META,
                    ch=N_META, nq=1, use_meta=False, chunk0=0, q_row0=M_ROW0,
                    kp_map=lambda b, j: 0, km_map=meta_blk,
                    kp_cols=(C_K // KV_W, C_V // KV_W), km_cols=(C_K // KV_W, C_V // KV_W),
                    mix=None, mix_rows=NT)
    nq_p = 4
    mix = attn_call(sinks, proj, proj, (proj, proj), (proj, proj), nb=BATCH, length=SEQ,
                    ch=64, nq=nq_p, use_meta=True, chunk0=0, q_row0=0,
                    kp_map=lambda b, j: jnp.maximum(b * (SEQ // WINDOW) + j * (64 * nq_p // WINDOW) - 1, 0),
                    km_map=meta_blk,
                    kp_cols=(C_K // KV_W, C_V // KV_W), km_cols=(C_K // KV_W, C_V // KV_W),
                    mix=mix, mix_rows=NT)
    mix = attn_call(sinks, proj, proj, (st['win_k'], st['win_v']), (st['meta_k'], st['meta_v']),
                    nb=DEC_BATCH, length=DEC_SEQ, ch=64, nq=1, use_meta=True, chunk0=2,
                    q_row0=S_ROW0, kp_map=lambda b, j: b, km_map=lambda b, j: b,
                    kp_cols=(0, 0), km_cols=(0, 0), mix=mix, mix_rows=NT)

    mix = pool_call(proj, zeros16, lambda b: b, 0, lw['pool_w'], lw['pool_scale'],
                    nb=BATCH, length=N_META, tb=N_META, row0=M_ROW0, ramp=True, mix=mix, mix_rows=NT)
    mix = pool_call(proj, proj, lambda b: M_ROW0 // 16 + b, C_U // GROUP_W, lw['pool_w'],
                    lw['pool_scale'], nb=BATCH, length=SEQ, tb=512, row0=0, ramp=False,
                    mix=mix, mix_rows=NT)
    mix = pool_call(proj, st['pool'], lambda b: b, 0, lw['pool_w'], lw['pool_scale'],
                    nb=DEC_BATCH, length=DEC_SEQ, tb=DEC_SEQ, row0=S_ROW0, ramp=False,
                    mix=mix, mix_rows=NT)

    ssd_w = lw['ssd']
    h0z = jnp.zeros((BATCH, 512, C_STATE), F32)
    mix, hc_m = ssd_call(proj, zeros8, lambda b: b, (0, 2, 3), h0z, ssd_w,
                         nb=BATCH, length=N_META, q=N_META, row0=M_ROW0, mix=mix, mix_rows=NT)
    meta_tail = lambda b: (M_ROW0 + 8) // 8 + 2 * b
    mix, hc_p = ssd_call(proj, proj, meta_tail, (C_XS // 512, C_B // 256, C_C // 256), hc_m, ssd_w,
                         nb=BATCH, length=SEQ, q=256, row0=0, mix=mix, mix_rows=NT)
    mix, hc_s = ssd_call(proj, st['ssm_conv'], lambda b: b, (0, 2, 3), st['ssm'], ssd_w,
                         nb=DEC_BATCH, length=DEC_SEQ, q=DEC_SEQ, row0=S_ROW0, mix=mix, mix_rows=NT)

    lru_w = lw['lru']
    l0z = jnp.zeros((BATCH, 1, GROUP_W), F32)
    mix, hd_m = lru_call(proj, zeros8, lambda b: b, 0, l0z, lru_w,
                         nb=BATCH, length=N_META, tb=N_META, row0=M_ROW0, mix=mix, mix_rows=NT)
    mix, hd_p = lru_call(proj, proj, meta_tail, C_RX // GROUP_W, hd_m, lru_w,
                         nb=BATCH, length=SEQ, tb=256, row0=0, mix=mix, mix_rows=NT)
    mix, hd_s = lru_call(proj, st['lru_conv'], lambda b: b, 0, st['lru'], lru_w,
                         nb=DEC_BATCH, length=DEC_SEQ, tb=DEC_SEQ, row0=S_ROW0, mix=mix, mix_rows=NT)

    x1, x1p, sc = out_proj(mix, h, lw['w_out'], lw['ln1_g'], lw['ln1_b'], lw['rw_hi'], lw['rw_lo'])

    e_sel, rank_sel, w_sel, cnt = router(sc, lw['router_bias'])
    counts = cnt[0, :N_EXPERTS].astype(jnp.int32)
    pstarts, dest_flat, blk_e, nused = _schedule(counts, e_sel[:, :TOP_K], rank_sel[:, :TOP_K])
    xg = dispatch(counts, pstarts, dest_flat, x1p)
    yb = moe_experts(blk_e, nused, xg, lw['wg'], lw['wu'], lw['wd'], lw['layer'])
    h_new = combine_shared_ln2(dest_flat, x1, w_sel, yb, lw['sh_wg'], lw['sh_wu'], lw['sh_wd'],
                               lw['ln2_g'], lw['ln2_b'])

    def tail(row0, nb, length, nrows, c0, width):
        return jnp.stack([proj[row0 + (b + 1) * length - nrows:row0 + (b + 1) * length, c0:c0 + width]
                          for b in range(nb)])

    kv4 = lambda x: x.reshape(x.shape[0], x.shape[1], A_KV_HEADS, HEAD_DIM)
    p_state = (
        kv4(tail(M_ROW0, BATCH, N_META, N_META, C_K, KV_W)),
        kv4(tail(M_ROW0, BATCH, N_META, N_META, C_V, KV_W)),
        kv4(tail(0, BATCH, SEQ, WINDOW, C_K, KV_W)), kv4(tail(0, BATCH, SEQ, WINDOW, C_V, KV_W)),
        tail(0, BATCH, SEQ, POOL_STATE, C_U, GROUP_W),
        tail(0, BATCH, SEQ, CONV_W - 1, C_XS, C_CONV_CH),
        hc_p.reshape(BATCH, C_HEADS, 64, C_STATE),
        tail(0, BATCH, SEQ, CONV_W - 1, C_RX, GROUP_W),
        hd_p.reshape(BATCH, GROUP_W),
    )
    s_state = (
        kv4(tail(S_ROW0, DEC_BATCH, DEC_SEQ, DEC_SEQ, C_K, KV_W)),
        kv4(tail(S_ROW0, DEC_BATCH, DEC_SEQ, DEC_SEQ, C_V, KV_W)),
        tail(S_ROW0, DEC_BATCH, DEC_SEQ, POOL_STATE, C_U, GROUP_W),
        tail(S_ROW0, DEC_BATCH, DEC_SEQ, CONV_W - 1, C_XS, C_CONV_CH),
        hc_s.reshape(DEC_BATCH, C_HEADS, 64, C_STATE),
        tail(S_ROW0, DEC_BATCH, DEC_SEQ, CONV_W - 1, C_RX, GROUP_W),
        hd_s.reshape(DEC_BATCH, GROUP_W),
    )
    return h_new, p_state, s_state


def _block_diag(w):
    z = jnp.zeros((D_BLOCK_W, D_BLOCK_W), w.dtype)
    return jnp.stack([jnp.block([[w[2 * s], z], [z, w[2 * s + 1]]]) for s in range(4)])


def _pad_lanes(v, width=LANE):
    return jnp.pad(v, (0, width - v.shape[0])).reshape(1, width)


def kernel(x_prompt, x_sample, cache_attn_meta_k, cache_attn_meta_v, cache_attn_k, cache_attn_v, state_pool, state_ssm_conv, state_ssm, state_lru_conv, state_lru, meta_tokens, ln_in_g, ln_in_b, w_in, w_out, attn_sinks, pool_w, pool_scale, ssm_conv_w, ssm_conv_b, ssm_dt_bias, ssm_a_log, ssm_d, ssm_norm_g, lru_conv_w, lru_conv_b, lru_wr, lru_br, lru_wi, lru_bi, lru_lambda, ln1_g, ln1_b, ln2_g, ln2_b, router_w, router_bias, exp_w_gate, exp_w_up, exp_w_down, sh_w_gate, sh_w_up, sh_w_down):
    row = lambda v: v.reshape(1, -1).astype(F32)
    h = ln_in(x_prompt.reshape(P_ROWS, D_MODEL), x_sample.reshape(S_ROWS, D_MODEL),
              meta_tokens.astype(F32), row(ln_in_g), row(ln_in_b))
    p_states, s_states = [], []
    for i in range(DEPTH):
        wi = w_in[i]
        s0 = 0
        parts = {}
        for name, size in zip(('q', 'k', 'v', 'u', 'z', 'xbc', 'dt', 'rx', 'rg'),
                              (512, 128, 128, 512, 512, 1024, 8, 512, 512)):
            parts[name] = wi[:, s0:s0 + size]
            s0 += size
        w_in_p = jnp.concatenate(
            [parts[n] for n in ('q', 'u', 'z', 'rx', 'rg', 'xbc', 'k', 'v', 'dt')]
            + [jnp.zeros((D_MODEL, PROJ_P - C_DT - C_HEADS), F32)], axis=1).astype(BF16)
        rw = jnp.pad(router_w[i].astype(F32), ((0, 0), (0, LANE - N_EXPERTS)))
        rw_hi = rw.astype(BF16)
        rw_lo = (rw - rw_hi.astype(F32)).astype(BF16)
        cw = ssm_conv_w[i].astype(F32)
        cb = ssm_conv_b[i].astype(F32)
        lw = dict(
            w_in=w_in_p, w_out=w_out[i].astype(BF16), sinks=attn_sinks[i].astype(F32),
            pool_w=pool_w[i].astype(BF16), pool_scale=row(pool_scale[i]),
            ssd=dict(cw_x=cw[:, :512], cw_b=cw[:, 512:768], cw_c=cw[:, 768:],
                     cb_x=row(cb[:512]), cb_b=row(cb[512:768]), cb_c=row(cb[768:]),
                     dt_bias=_pad_lanes(ssm_dt_bias[i].astype(F32)),
                     a_log=_pad_lanes(ssm_a_log[i].astype(F32)),
                     d_skip=row(jnp.repeat(ssm_d[i].astype(F32), 64)),
                     norm_g=row(ssm_norm_g[i])),
            lru=dict(cw=lru_conv_w[i].astype(F32), cb=row(lru_conv_b[i]),
                     wr=_block_diag(lru_wr[i]).astype(BF16), wi=_block_diag(lru_wi[i]).astype(BF16),
                     br=row(lru_br[i]), bi=row(lru_bi[i]), lam=row(lru_lambda[i])),
            ln1_g=row(ln1_g[i]), ln1_b=row(ln1_b[i]), ln2_g=row(ln2_g[i]), ln2_b=row(ln2_b[i]),
            rw_hi=rw_hi, rw_lo=rw_lo, router_bias=_pad_lanes(router_bias[i].astype(F32)),
            wg=exp_w_gate, wu=exp_w_up, wd=exp_w_down, layer=i,
            sh_wg=sh_w_gate[i].astype(BF16), sh_wu=sh_w_up[i].astype(BF16),
            sh_wd=sh_w_down[i].astype(BF16),
        )
        st = dict(
            meta_k=cache_attn_meta_k[i].reshape(DEC_BATCH * N_META, KV_W),
            meta_v=cache_attn_meta_v[i].reshape(DEC_BATCH * N_META, KV_W),
            win_k=cache_attn_k[i].reshape(DEC_BATCH * WINDOW, KV_W),
            win_v=cache_attn_v[i].reshape(DEC_BATCH * WINDOW, KV_W),
            pool=_pad_rows(state_pool[i], 16),
            ssm_conv=_pad_rows(state_ssm_conv[i], 8),
            ssm=state_ssm[i].reshape(DEC_BATCH, 512, C_STATE),
            lru_conv=_pad_rows(state_lru_conv[i], 8),
            lru=state_lru[i].reshape(DEC_BATCH, 1, GROUP_W),
        )
        h, ps, ss = _layer(h, st, lw)
        p_states.append(ps)
        s_states.append(ss)
    stk = lambda sts, j: jnp.stack([s[j] for s in sts])
    y_prompt = h[:P_ROWS].reshape(BATCH, SEQ, D_MODEL)
    y_sample = h[S_ROW0:S_ROW0 + S_ROWS].reshape(DEC_BATCH, DEC_SEQ, D_MODEL)
    return ((y_prompt, y_sample)
            + tuple(stk(p_states, j) for j in range(9))
            + tuple(stk(s_states, j) for j in range(7)))
```

```python
import functools
import math

import jax
import jax.numpy as jnp
from jax import lax
from jax.experimental import pallas as pl
from jax.experimental.pallas import tpu as pltpu

F32 = jnp.float32
BF16 = jnp.bfloat16

D_MODEL = 2048
BATCH = 4
SEQ = 4096
DEPTH = 4
DEC_BATCH = 16
DEC_SEQ = 64
N_META = 16
GROUP_W = 512
HEAD_DIM = 64
A_HEADS = 8
A_KV_HEADS = 2
A_GROUP = 4
KV_W = 128
WINDOW = 128
POOL_SIZES = (2, 4, 8, 16)
POOL_GW = 128
POOL_STATE = 15
C_HEADS = 8
C_STATE = 128
C_CONV_CH = 1024
CONV_W = 4
D_BLOCKS = 8
D_BLOCK_W = 64
LRU_C = 8.0
N_EXPERTS = 64
TOP_K = 8
N_GROUP = 8
TOPK_GROUP = 4
D_EXPERT = 512
ROUTE_SCALE = 2.5
ALPHA = (2 * DEPTH) ** 0.25
LN_EPS = 1e-5

LANE = 128
SUBLANE = 8
VMEM_LIMIT = 56 * 1024 * 1024

P_ROWS = BATCH * SEQ
S_ROWS = DEC_BATCH * DEC_SEQ
M_ROWS = BATCH * N_META
S_ROW0 = P_ROWS
M_ROW0 = P_ROWS + S_ROWS
NT = P_ROWS + S_ROWS + M_ROWS
TM = 512
N_TILES = -(-NT // TM)

C_Q, C_U, C_Z, C_RX, C_RG, C_XS, C_B, C_C, C_K, C_V, C_DT = (
    0, 512, 1024, 1536, 2048, 2560, 3072, 3328, 3584, 3712, 3840)
PROJ_P = 4096
PROJ_TN = 1024

MOE_T = 512
NK = NT * TOP_K
MOE_NB = (NK + N_EXPERTS * (MOE_T - 1) + MOE_T - 1) // MOE_T


def _cparams(sem):
    return pltpu.CompilerParams(dimension_semantics=sem, vmem_limit_bytes=VMEM_LIMIT)


def _ln(x, g, b):
    mu = jnp.mean(x, axis=-1, keepdims=True)
    xc = x - mu
    var = jnp.mean(xc * xc, axis=-1, keepdims=True)
    return xc * lax.rsqrt(var + LN_EPS) * g + b


def _dot(a, b):
    return jnp.dot(a, b, preferred_element_type=F32)


def _dot_nt(a, b, precision=None):
    return lax.dot_general(a, b, (((1,), (1,)), ((), ())), precision=precision,
                           preferred_element_type=F32)


def _full(shape):
    nd = len(shape)
    return pl.BlockSpec(shape, lambda *_: (0,) * nd)


MIX_W = 4 * GROUP_W


def _skip_ref(kern, idx, *refs):
    return kern(*refs[:idx], *refs[idx + 1:])


def _mixer_call(kern, grid, in_specs, args, y_spec, more_out_specs, more_out_shapes, scratch,
                mix, mix_rows, name):
    in_specs = list(in_specs)
    args = tuple(args)
    aliases = {}
    if mix is not None:
        n_in = len(in_specs)
        kern = functools.partial(_skip_ref, kern, n_in)
        in_specs.append(pl.BlockSpec(memory_space=pl.ANY))
        args = args + (mix,)
        aliases = {n_in: 0}
    return pl.pallas_call(
        kern,
        grid=grid,
        in_specs=in_specs,
        out_specs=[y_spec] + list(more_out_specs),
        out_shape=[jax.ShapeDtypeStruct((mix_rows, MIX_W), F32)] + list(more_out_shapes),
        scratch_shapes=scratch,
        input_output_aliases=aliases,
        compiler_params=_cparams(("arbitrary",) * len(grid)),
        name=name,
    )(*args)


def _ln_in_kernel(xp_ref, xs_ref, meta_ref, g_ref, b_ref, o_ref):
    i = pl.program_id(0)
    g = g_ref[...]
    b = b_ref[...]
    n_p = P_ROWS // TM
    n_s = S_ROWS // TM

    @pl.when(i < n_p)
    def _():
        o_ref[...] = _ln(xp_ref[...], g, b)

    @pl.when((i >= n_p) & (i < n_p + n_s))
    def _():
        o_ref[...] = _ln(xs_ref[...], g, b)

    @pl.when(i == n_p + n_s)
    def _():
        m = _ln(meta_ref[...], g, b)
        for r in range(BATCH):
            o_ref[N_META * r:N_META * (r + 1), :] = m


def ln_in(xp2, xs2, meta, g, b):
    n_p = P_ROWS // TM
    n_s = S_ROWS // TM
    return pl.pallas_call(
        _ln_in_kernel,
        grid=(N_TILES,),
        in_specs=[
            pl.BlockSpec((TM, D_MODEL), lambda i: (jnp.minimum(i, n_p - 1), 0)),
            pl.BlockSpec((TM, D_MODEL), lambda i: (jnp.clip(i - n_p, 0, n_s - 1), 0)),
            _full((N_META, D_MODEL)),
            _full((1, D_MODEL)),
            _full((1, D_MODEL)),
        ],
        out_specs=pl.BlockSpec((TM, D_MODEL), lambda i: (i, 0)),
        out_shape=jax.ShapeDtypeStruct((NT, D_MODEL), F32),
        compiler_params=_cparams(("arbitrary",)),
        name="ln_in",
    )(xp2, xs2, meta, g, b)


def _in_proj_kernel(x_ref, w_ref, o_ref):
    o_ref[...] = _dot(x_ref[...].astype(BF16), w_ref[...])


PROJ_TM = 1024


def in_proj(h, w_bf):
    return pl.pallas_call(
        _in_proj_kernel,
        grid=(-(-NT // PROJ_TM), PROJ_P // PROJ_TN),
        in_specs=[
            pl.BlockSpec((PROJ_TM, D_MODEL), lambda i, n: (i, 0)),
            pl.BlockSpec((D_MODEL, PROJ_TN), lambda i, n: (0, n)),
        ],
        out_specs=pl.BlockSpec((PROJ_TM, PROJ_TN), lambda i, n: (i, n)),
        out_shape=jax.ShapeDtypeStruct((NT, PROJ_P), F32),
        compiler_params=_cparams(("arbitrary", "arbitrary")),
        name="in_proj",
    )(h, w_bf)


ATT_PAD = 64
ATT_WIN = ATT_PAD + WINDOW + 64


def _attn_kernel(sink_ref, q_ref, kc_ref, vc_ref, kp_ref, vp_ref, km_ref, vm_ref, o_ref,
                 kbuf, vbuf, *, ch, nq, use_meta, chunk0):
    j = pl.program_id(1)
    tq = ch * nq
    zpad = jnp.zeros((ATT_PAD - N_META, KV_W), BF16)
    kbuf[0:N_META, :] = km_ref[...].astype(BF16)
    kbuf[N_META:ATT_PAD, :] = zpad
    vbuf[0:N_META, :] = vm_ref[...].astype(BF16)
    vbuf[N_META:ATT_PAD, :] = zpad
    kbuf[ATT_PAD:ATT_PAD + WINDOW, :] = kp_ref[...].astype(BF16)
    vbuf[ATT_PAD:ATT_PAD + WINDOW, :] = vp_ref[...].astype(BF16)
    kbuf[ATT_PAD + WINDOW:ATT_PAD + WINDOW + tq, :] = kc_ref[...].astype(BF16)
    vbuf[ATT_PAD + WINDOW:ATT_PAD + WINDOW + tq, :] = vc_ref[...].astype(BF16)
    if ch < 64:
        zc = jnp.zeros((64 - ch, KV_W), BF16)
        kbuf[ATT_PAD + WINDOW + tq:ATT_PAD + WINDOW + tq + 64 - ch, :] = zc
        vbuf[ATT_PAD + WINDOW + tq:ATT_PAD + WINDOW + tq + 64 - ch, :] = zc

    rows = A_GROUP * ch
    col = lax.broadcasted_iota(jnp.int32, (rows, ATT_WIN), 1)
    row = lax.broadcasted_iota(jnp.int32, (rows, 1), 0)
    lane = lax.broadcasted_iota(jnp.int32, (ch, LANE), 1)
    lo = lane < HEAD_DIM

    for i in range(nq):
        c = chunk0 + j * nq + i
        first_band = ATT_PAD + 64 * jnp.maximum(2 - c, 0)
        valid = (col >= first_band) & (col < ATT_PAD + WINDOW + ch)
        if use_meta:
            valid = valid | (col < N_META)
        kcat = jnp.concatenate(
            [kbuf[0:ATT_PAD, :], kbuf[ATT_PAD + ch * i:ATT_PAD + ch * i + WINDOW + 64, :]], axis=0)
        vcat = jnp.concatenate(
            [vbuf[0:ATT_PAD, :], vbuf[ATT_PAD + ch * i:ATT_PAD + ch * i + WINDOW + 64, :]], axis=0)
        qi = q_ref[ch * i:ch * (i + 1), :] * (HEAD_DIM ** -0.5)
        tiles = [qi[:, LANE * t:LANE * (t + 1)] for t in range(A_HEADS // 2)]
        out_tiles = [None] * (A_HEADS // 2)
        for kh in range(A_KV_HEADS):
            qs = []
            for r in range(A_GROUP):
                h = A_GROUP * kh + r
                t = tiles[h // 2]
                if h % 2 != kh:
                    t = pltpu.roll(t, HEAD_DIM, axis=1)
                keep = lo if kh == 0 else jnp.logical_not(lo)
                qs.append(jnp.where(keep, t, 0.0).astype(BF16))
            qz = jnp.concatenate(qs, axis=0)
            s = _dot_nt(qz, kcat)
            s = jnp.where(valid, s, -jnp.inf)
            sink = jnp.zeros((rows, 1), F32)
            for r in range(A_GROUP):
                sink = jnp.where((row >= r * ch) & (row < (r + 1) * ch),
                                 sink_ref[A_GROUP * kh + r], sink)
            m = jnp.maximum(jnp.max(s, axis=-1, keepdims=True), sink)
            p = jnp.exp(s - m)
            den = jnp.sum(p, axis=-1, keepdims=True) + jnp.exp(sink - m)
            probs = (p / den).astype(BF16)
            o = _dot(probs, vcat)
            for r in range(A_GROUP):
                h = A_GROUP * kh + r
                oh = o[r * ch:(r + 1) * ch, :]
                if h % 2 != kh:
                    oh = pltpu.roll(oh, HEAD_DIM, axis=1)
                keep = lo if h % 2 == 0 else jnp.logical_not(lo)
                prev = out_tiles[h // 2]
                out_tiles[h // 2] = jnp.where(keep, oh, 0.0 if prev is None else prev)
        o_ref[ch * i:ch * (i + 1), :] = jnp.concatenate(out_tiles, axis=1)


def attn_call(sinks, q_src, kc_src, kp_src, km_src, *, nb, length, ch, nq, use_meta, chunk0,
              q_row0, kp_map, km_map, kp_cols, km_cols, mix, mix_rows):
    tq = ch * nq
    nj = length // tq
    qb0 = q_row0 // tq
    kern = functools.partial(_attn_kernel, ch=ch, nq=nq, use_meta=use_meta, chunk0=chunk0)
    in_specs = [
        pl.BlockSpec(memory_space=pltpu.SMEM),
        pl.BlockSpec((tq, GROUP_W), lambda b, j: (qb0 + b * nj + j, C_Q // GROUP_W)),
        pl.BlockSpec((tq, KV_W), lambda b, j: (qb0 + b * nj + j, C_K // KV_W)),
        pl.BlockSpec((tq, KV_W), lambda b, j: (qb0 + b * nj + j, C_V // KV_W)),
        pl.BlockSpec((WINDOW, KV_W), lambda b, j: (kp_map(b, j), kp_cols[0])),
        pl.BlockSpec((WINDOW, KV_W), lambda b, j: (kp_map(b, j), kp_cols[1])),
        pl.BlockSpec((N_META, KV_W), lambda b, j: (km_map(b, j), km_cols[0])),
        pl.BlockSpec((N_META, KV_W), lambda b, j: (km_map(b, j), km_cols[1])),
    ]
    return _mixer_call(
        kern, (nb, nj), in_specs,
        (sinks, q_src, kc_src, kc_src, kp_src[0], kp_src[1], km_src[0], km_src[1]),
        pl.BlockSpec((tq, GROUP_W), lambda b, j: (qb0 + b * nj + j, 0)), [], [],
        [pltpu.VMEM((ATT_PAD + WINDOW + tq + 64, KV_W), BF16),
         pltpu.VMEM((ATT_PAD + WINDOW + tq + 64, KV_W), BF16)],
        mix, mix_rows, "attn")[0]


def _pool_kernel(u_ref, prev_ref, w_ref, scale_ref, o_ref, buf, *, tb, ramp):
    j = pl.program_id(1)

    @pl.when(j == 0)
    def _():
        buf[0:16, :] = prev_ref[...]

    buf[16:16 + tb, :] = u_ref[...]
    pos = j * tb + lax.broadcasted_iota(jnp.int32, (tb, 1), 0)
    outs = []
    for g, win in enumerate(POOL_SIZES):
        sl = slice(g * POOL_GW, (g + 1) * POOL_GW)
        tot = buf[16:16 + tb, sl]
        for k in range(1, win):
            tot = tot + buf[16 - k:16 - k + tb, sl]
        if ramp:
            cnt = jnp.minimum(win, pos + 1).astype(F32)
            mean = tot / cnt
        else:
            mean = tot * (1.0 / win)
        d = mean - buf[16:16 + tb, sl]
        outs.append(_dot(d.astype(BF16), w_ref[g]))
    o_ref[...] = jnp.concatenate(outs, axis=1) * scale_ref[...]
    buf[0:16, :] = buf[tb:tb + 16, :]


def pool_call(proj, prev_src, prev_map, prev_col, w_bf, scale, *, nb, length, tb, row0, ramp,
              mix, mix_rows):
    nj = length // tb
    rb0 = row0 // tb
    kern = functools.partial(_pool_kernel, tb=tb, ramp=ramp)
    in_specs = [
        pl.BlockSpec((tb, GROUP_W), lambda b, j: (rb0 + b * nj + j, C_U // GROUP_W)),
        pl.BlockSpec((16, GROUP_W), lambda b, j: (prev_map(b), prev_col)),
        _full((4, POOL_GW, POOL_GW)),
        _full((1, GROUP_W)),
    ]
    return _mixer_call(
        kern, (nb, nj), in_specs, (proj, prev_src, w_bf, scale),
        pl.BlockSpec((tb, GROUP_W), lambda b, j: (rb0 + b * nj + j, 1)), [], [],
        [pltpu.VMEM((tb + 16, GROUP_W), F32)], mix, mix_rows, "pool")[0]


def _conv_block(buf, x_ref, w_ref, b_ref, tb):
    buf[8:8 + tb, :] = x_ref[...]
    acc = b_ref[...] + buf[5:5 + tb, :] * w_ref[0:1, :]
    for k in range(1, CONV_W):
        acc = acc + buf[5 + k:5 + k + tb, :] * w_ref[k:k + 1, :]
    return acc


def _conv_carry(buf, tb):
    buf[0:8, :] = buf[tb:tb + 8, :]


def _col(x, h):
    lane = lax.broadcasted_iota(jnp.int32, x.shape, 1)
    return jnp.sum(jnp.where(lane == h, x, 0.0), axis=1, keepdims=True)


def _ssd_kernel(xs_ref, bm_ref, cm_ref, dt_ref, z_ref, px_ref, pb_ref, pc_ref, h0_ref,
                wx_ref, wb_ref, wc_ref, bx_ref, bb_ref, bc_ref,
                dtb_ref, alog_ref, dskip_ref, ng_ref,
                y_ref, hout_ref, bufx, bufb, bufc, hst, *, q):
    j = pl.program_id(1)
    nj = pl.num_programs(1)
    hi = lax.Precision.HIGHEST

    @pl.when(j == 0)
    def _():
        bufx[0:8, :] = px_ref[...]
        bufb[0:8, :] = pb_ref[...]
        bufc[0:8, :] = pc_ref[...]
        hst[...] = h0_ref[...]

    xs = jax.nn.silu(_conv_block(bufx, xs_ref, wx_ref, bx_ref, q))
    bm = jax.nn.silu(_conv_block(bufb, bm_ref, wb_ref, bb_ref, q))
    cm = jax.nn.silu(_conv_block(bufc, cm_ref, wc_ref, bc_ref, q))
    _conv_carry(bufx, q)
    _conv_carry(bufb, q)
    _conv_carry(bufc, q)

    lane1 = lax.broadcasted_iota(jnp.int32, (1, LANE), 1)
    hmask = lane1 < C_HEADS
    dt = jnp.where(hmask, jax.nn.softplus(dt_ref[...] + dtb_ref[...]), 0.0)
    a = jnp.where(hmask, -jnp.exp(alog_ref[...]), 0.0)
    dta = dt * a
    ri = lax.broadcasted_iota(jnp.int32, (q, q), 0)
    ci = lax.broadcasted_iota(jnp.int32, (q, q), 1)
    tri = ri >= ci
    cum = jnp.dot(tri.astype(F32), dta, precision=hi, preferred_element_type=F32)
    eye = (lax.broadcasted_iota(jnp.int32, (LANE, LANE), 0)
           == lax.broadcasted_iota(jnp.int32, (LANE, LANE), 1)).astype(F32)
    cum_t = _dot_nt(eye, cum, precision=hi)
    dt_t = _dot_nt(eye, dt, precision=hi)
    ecum = jnp.exp(cum)
    cum_last = cum[q - 1:q, :]
    te = jnp.exp(cum_last - cum) * dt

    lane = lax.broadcasted_iota(jnp.int32, (q, LANE), 1)
    lo = lane < 64
    bm_bf = bm.astype(BF16)
    cm_bf = cm.astype(BF16)
    cb = [_dot_nt(cm_bf[:, LANE * g:LANE * (g + 1)], bm_bf[:, LANE * g:LANE * (g + 1)])
          for g in range(2)]
    yoff = [_dot_nt(cm_bf[:, LANE * g:LANE * (g + 1)], hst[256 * g:256 * (g + 1), :].astype(BF16))
            for g in range(2)]

    y_tiles = []
    xw_tiles = []
    for k in range(C_HEADS // 2):
        g = k // 2
        x_pair = xs[:, LANE * k:LANE * (k + 1)]
        ydiag = None
        for par in range(2):
            h = 2 * k + par
            seg = _col(cum, h) - cum_t[h:h + 1, :]
            lm = jnp.exp(jnp.where(tri, seg, -jnp.inf))
            mm = (cb[g] * lm * dt_t[h:h + 1, :]).astype(BF16)
            xm = jnp.where(lo if par == 0 else jnp.logical_not(lo), x_pair, 0.0).astype(BF16)
            part = _dot(mm, xm)
            ydiag = part if ydiag is None else ydiag + part
        e_pair = jnp.where(lo, _col(ecum, 2 * k), _col(ecum, 2 * k + 1))
        te_pair = jnp.where(lo, _col(te, 2 * k), _col(te, 2 * k + 1))
        kk = k % 2
        y_tiles.append(ydiag + yoff[g][:, LANE * kk:LANE * (kk + 1)] * e_pair
                       + dskip_ref[:, LANE * k:LANE * (k + 1)] * x_pair)
        xw_tiles.append((x_pair * te_pair).astype(BF16))

    eye2 = (lax.broadcasted_iota(jnp.int32, (256, 256), 0)
            == lax.broadcasted_iota(jnp.int32, (256, 256), 1)).astype(BF16)
    for g in range(2):
        xw = jnp.concatenate(xw_tiles[2 * g:2 * g + 2], axis=1)
        xw_t = _dot_nt(eye2, xw).astype(BF16)
        s_new = _dot(xw_t, bm_bf[:, LANE * g:LANE * (g + 1)])
        dec = jnp.concatenate(
            [jnp.broadcast_to(jnp.exp(cum_t[4 * g + r:4 * g + r + 1, q - 1:q]), (64, LANE))
             for r in range(4)], axis=0)
        hst[256 * g:256 * (g + 1), :] = dec * hst[256 * g:256 * (g + 1), :] + s_new

    y = jnp.concatenate(y_tiles, axis=1) * jax.nn.silu(z_ref[...])
    y = y * lax.rsqrt(jnp.mean(y * y, axis=-1, keepdims=True) + 1e-6) * ng_ref[...]
    y_ref[...] = y

    @pl.when(j == nj - 1)
    def _():
        hout_ref[...] = hst[...]


def ssd_call(proj, prev_src, prev_map, prev_cols, h0, lw, *, nb, length, q, row0, mix, mix_rows):
    nj = length // q
    rb0 = row0 // q
    kern = functools.partial(_ssd_kernel, q=q)
    blk = lambda width, col: pl.BlockSpec((q, width), lambda b, j: (rb0 + b * nj + j, col // width))
    pblk = lambda width, col: pl.BlockSpec((8, width), lambda b, j: (prev_map(b), col))
    in_specs = [
        blk(512, C_XS), blk(256, C_B), blk(256, C_C), blk(LANE, C_DT), blk(512, C_Z),
        pblk(512, prev_cols[0]), pblk(256, prev_cols[1]), pblk(256, prev_cols[2]),
        pl.BlockSpec((None, 512, C_STATE), lambda b, j: (b, 0, 0)),
        _full((CONV_W, 512)), _full((CONV_W, 256)), _full((CONV_W, 256)),
        _full((1, 512)), _full((1, 256)), _full((1, 256)),
        _full((1, LANE)), _full((1, LANE)), _full((1, 512)), _full((1, 512)),
    ]
    args = (proj, proj, proj, proj, proj, prev_src, prev_src, prev_src, h0,
            lw['cw_x'], lw['cw_b'], lw['cw_c'], lw['cb_x'], lw['cb_b'], lw['cb_c'],
            lw['dt_bias'], lw['a_log'], lw['d_skip'], lw['norm_g'])
    return _mixer_call(
        kern, (nb, nj), in_specs, args,
        pl.BlockSpec((q, GROUP_W), lambda b, j: (rb0 + b * nj + j, 2)),
        [pl.BlockSpec((None, 512, C_STATE), lambda b, j: (b, 0, 0))],
        [jax.ShapeDtypeStruct((nb, 512, C_STATE), F32)],
        [pltpu.VMEM((q + 8, 512), F32), pltpu.VMEM((q + 8, 256), F32),
         pltpu.VMEM((q + 8, 256), F32), pltpu.VMEM((512, C_STATE), F32)],
        mix, mix_rows, "ssd")


def _lru_kernel(rx_ref, rg_ref, prev_ref, h0_ref, cw_ref, cb_ref, wr_ref, wi_ref,
                br_ref, bi_ref, lam_ref, y_ref, hout_ref, buf, hc, *, tb):
    j = pl.program_id(1)
    nj = pl.num_programs(1)

    @pl.when(j == 0)
    def _():
        buf[0:8, :] = prev_ref[...]
        hc[...] = jnp.broadcast_to(h0_ref[...], hc.shape)

    xc = _conv_block(buf, rx_ref, cw_ref, cb_ref, tb)
    _conv_carry(buf, tb)
    rs, gs = [], []
    for s in range(GROUP_W // LANE):
        xb = xc[:, LANE * s:LANE * (s + 1)].astype(BF16)
        rs.append(_dot(xb, wr_ref[s]))
        gs.append(_dot(xb, wi_ref[s]))
    r = jax.nn.sigmoid(jnp.concatenate(rs, axis=1) + br_ref[...])
    gi = jax.nn.sigmoid(jnp.concatenate(gs, axis=1) + bi_ref[...])
    log_a = -LRU_C * r * jax.nn.softplus(-lam_ref[...])
    a = jnp.exp(log_a)
    u = jnp.sqrt(jnp.maximum(1.0 - jnp.exp(2.0 * log_a), 0.0)) * (gi * xc)
    t = lax.broadcasted_iota(jnp.int32, (tb, 1), 0)
    d = 1
    while d < tb:
        a_sh = jnp.where(t >= d, pltpu.roll(a, d, axis=0), 1.0)
        u_sh = jnp.where(t >= d, pltpu.roll(u, d, axis=0), 0.0)
        u = a * u_sh + u
        a = a * a_sh
        d *= 2
    h = u + a * hc[0:1, :]
    y_ref[...] = h * jax.nn.gelu(rg_ref[...])
    hc[...] = jnp.broadcast_to(h[tb - 1:tb, :], hc.shape)

    @pl.when(j == nj - 1)
    def _():
        hout_ref[...] = h[tb - 1:tb, :]


def lru_call(proj, prev_src, prev_map, prev_col, h0, lw, *, nb, length, tb, row0, mix, mix_rows):
    nj = length // tb
    rb0 = row0 // tb
    kern = functools.partial(_lru_kernel, tb=tb)
    in_specs = [
        pl.BlockSpec((tb, GROUP_W), lambda b, j: (rb0 + b * nj + j, C_RX // GROUP_W)),
        pl.BlockSpec((tb, GROUP_W), lambda b, j: (rb0 + b * nj + j, C_RG // GROUP_W)),
        pl.BlockSpec((8, GROUP_W), lambda b, j: (prev_map(b), prev_col)),
        pl.BlockSpec((None, 1, GROUP_W), lambda b, j: (b, 0, 0)),
        _full((CONV_W, GROUP_W)), _full((1, GROUP_W)),
        _full((4, LANE, LANE)), _full((4, LANE, LANE)),
        _full((1, GROUP_W)), _full((1, GROUP_W)), _full((1, GROUP_W)),
    ]
    args = (proj, proj, prev_src, h0, lw['cw'], lw['cb'], lw['wr'], lw['wi'],
            lw['br'], lw['bi'], lw['lam'])
    return _mixer_call(
        kern, (nb, nj), in_specs, args,
        pl.BlockSpec((tb, GROUP_W), lambda b, j: (rb0 + b * nj + j, 3)),
        [pl.BlockSpec((None, 1, GROUP_W), lambda b, j: (b, 0, 0))],
        [jax.ShapeDtypeStruct((nb, 1, GROUP_W), F32)],
        [pltpu.VMEM((tb + 8, GROUP_W), F32), pltpu.VMEM((8, GROUP_W), F32)],
        mix, mix_rows, "lru")


HALF = D_MODEL // 2


def _pack_bf16_pairs(x):
    bits = lax.bitcast_convert_type(x.astype(BF16).astype(F32), jnp.uint32)
    return (bits[:, :HALF] >> 16) | (bits[:, HALF:] & jnp.uint32(0xFFFF0000))


def _unpack_bf16_pairs(w):
    lo = lax.bitcast_convert_type(w << 16, F32).astype(BF16)
    hi = lax.bitcast_convert_type(w & jnp.uint32(0xFFFF0000), F32).astype(BF16)
    return lo, hi


OUT_SUB = 256


def _out_proj_kernel(mix_ref, h_ref, w_ref, g_ref, b_ref, rw_ref, x1_ref, xp_ref, sc_ref):
    for r in range(TM // OUT_SUB):
        rows = slice(r * OUT_SUB, (r + 1) * OUT_SUB)
        y = _dot(mix_ref[rows, :].astype(BF16), w_ref[...])
        x1 = _ln(ALPHA * h_ref[rows, :] + y, g_ref[...], b_ref[...])
        x1_ref[rows, :] = x1
        xp_ref[rows, :] = _pack_bf16_pairs(x1)
        xh = x1.astype(BF16)
        xl = (x1 - xh.astype(F32)).astype(BF16)
        hh = _dot(xh, rw_ref[...])
        logits = hh[:, :LANE] + (hh[:, LANE:] + _dot(xl, rw_ref[:, :LANE]))
        sc_ref[rows, :] = jax.nn.sigmoid(logits)


def out_proj(mix, h, w_bf, g, b, rw_hi, rw_lo):
    rw = jnp.concatenate([rw_hi, rw_lo], axis=1)
    return pl.pallas_call(
        _out_proj_kernel,
        grid=(N_TILES,),
        in_specs=[
            pl.BlockSpec((TM, D_MODEL), lambda i: (i, 0)),
            pl.BlockSpec((TM, D_MODEL), lambda i: (i, 0)),
            _full((D_MODEL, D_MODEL)),
            _full((1, D_MODEL)), _full((1, D_MODEL)),
            _full((D_MODEL, 2 * LANE)),
        ],
        out_specs=[pl.BlockSpec((TM, D_MODEL), lambda i: (i, 0)),
                   pl.BlockSpec((TM, HALF), lambda i: (i, 0)),
                   pl.BlockSpec((TM, LANE), lambda i: (i, 0))],
        out_shape=[jax.ShapeDtypeStruct((NT, D_MODEL), F32),
                   jax.ShapeDtypeStruct((NT, HALF), jnp.uint32),
                   jax.ShapeDtypeStruct((NT, LANE), F32)],
        compiler_params=_cparams(("arbitrary",)),
        name="out_proj",
    )(mix, h, w_bf, g, b, rw)


_BIG = 4096


def _group_allreduce(x, lane, op):
    for sh in (1, 2, 4):
        up = pltpu.roll(x, sh, axis=1)
        dn = pltpu.roll(x, LANE - sh, axis=1)
        x = op(x, jnp.where((lane & sh) != 0, up, dn))
    return x


def _router_kernel(sc_ref, bias_ref, e_ref, r_ref, w_ref, cnt_ref, run):
    i = pl.program_id(0)

    @pl.when(i == 0)
    def _():
        run[...] = jnp.zeros_like(run)

    sc = sc_ref[...]
    lane = lax.broadcasted_iota(jnp.int32, (TM, LANE), 1)
    valid = lane < N_EXPERTS
    neg = -jnp.inf
    biased = jnp.where(valid, sc + bias_ref[...], neg)
    gmax = _group_allreduce(biased, lane, jnp.maximum)
    first = _group_allreduce(jnp.where(biased == gmax, lane, _BIG), lane, jnp.minimum)
    second = _group_allreduce(jnp.where(lane == first, neg, biased), lane, jnp.maximum)
    gs = jnp.where(valid, gmax + second, neg)
    grp = lane >> 3
    cand = jnp.full((TM, LANE), neg, F32)
    for _ in range(TOPK_GROUP):
        m = jnp.max(gs, axis=1, keepdims=True)
        g1 = jnp.min(jnp.where(gs == m, grp, _BIG), axis=1, keepdims=True)
        hit = grp == g1
        cand = jnp.where(hit, biased, cand)
        gs = jnp.where(hit, neg, gs)
    sel = jnp.zeros((TM, LANE), F32)
    e_out = jnp.zeros((TM, LANE), jnp.int32)
    w_out = jnp.zeros((TM, LANE), F32)
    idxs = []
    for k in range(TOP_K):
        m = jnp.max(cand, axis=1, keepdims=True)
        ik = jnp.min(jnp.where(cand == m, lane, _BIG), axis=1, keepdims=True)
        hit = lane == ik
        vk = jnp.sum(jnp.where(hit, sc, 0.0), axis=1, keepdims=True)
        sel = jnp.where(hit, 1.0, sel)
        cand = jnp.where(hit, neg, cand)
        e_out = jnp.where(lane == k, ik, e_out)
        w_out = jnp.where(lane == k, vk, w_out)
        idxs.append(ik)
    wsum = jnp.sum(w_out, axis=1, keepdims=True)
    w_ref[...] = w_out / wsum * ROUTE_SCALE
    e_ref[...] = e_out
    rowi = lax.broadcasted_iota(jnp.int32, (TM, 1), 0)
    sel = jnp.where(rowi < NT - i * TM, sel, 0.0)
    ri = lax.broadcasted_iota(jnp.int32, (TM, TM), 0)
    ci = lax.broadcasted_iota(jnp.int32, (TM, TM), 1)
    before = _dot((ri > ci).astype(BF16), sel.astype(BF16))
    rank = run[0:1, :] + before
    r_out = jnp.zeros((TM, LANE), F32)
    for k in range(TOP_K):
        rk = jnp.sum(jnp.where(lane == idxs[k], rank, 0.0), axis=1, keepdims=True)
        r_out = jnp.where(lane == k, rk, r_out)
    r_ref[...] = r_out.astype(jnp.int32)
    run[...] = jnp.broadcast_to(rank[TM - 1:TM, :] + sel[TM - 1:TM, :], run.shape)

    @pl.when(i == pl.num_programs(0) - 1)
    def _():
        cnt_ref[...] = run[...]


def router(scores, bias_row):
    return pl.pallas_call(
        _router_kernel,
        grid=(N_TILES,),
        in_specs=[pl.BlockSpec((TM, LANE), lambda i: (i, 0)), _full((1, LANE))],
        out_specs=[pl.BlockSpec((TM, LANE), lambda i: (i, 0)),
                   pl.BlockSpec((TM, LANE), lambda i: (i, 0)),
                   pl.BlockSpec((TM, LANE), lambda i: (i, 0)),
                   _full((8, LANE))],
        out_shape=[jax.ShapeDtypeStruct((NT, LANE), jnp.int32),
                   jax.ShapeDtypeStruct((NT, LANE), jnp.int32),
                   jax.ShapeDtypeStruct((NT, LANE), F32),
                   jax.ShapeDtypeStruct((8, LANE), F32)],
        scratch_shapes=[pltpu.VMEM((8, LANE), F32)],
        compiler_params=_cparams(("arbitrary",)),
        name="router",
    )(scores, bias_row)


def _dispatch_kernel(cnt_ref, pst_ref, dest_ref, x_ref, xg_hbm, zbuf, sem, zsem):
    i = pl.program_id(0)
    n = jnp.minimum(TM, NT - i * TM)

    def row_copy(src_row, dst_row):
        return pltpu.make_async_copy(x_ref.at[pl.ds(src_row, 1)], xg_hbm.at[pl.ds(dst_row, 1)], sem)

    def zero_copy(dst_row):
        return pltpu.make_async_copy(zbuf.at[pl.ds(0, 1)], xg_hbm.at[pl.ds(dst_row, 1)], zsem)

    @pl.when(i == 0)
    def _():
        zbuf[...] = jnp.zeros_like(zbuf)

        def per_expert(e, carry):
            c = cnt_ref[e]
            npad = (MOE_T - c % MOE_T) % MOE_T
            base = pst_ref[e] + c

            def start(r, cc):
                zero_copy(base + r).start()
                return cc

            lax.fori_loop(0, npad, start, 0)

            def wait(r, cc):
                zero_copy(base + r).wait()
                return cc

            lax.fori_loop(0, npad, wait, 0)
            return carry

        lax.fori_loop(0, N_EXPERTS, per_expert, 0)

    def start(t, carry):
        for s in range(TOP_K):
            row_copy(t, dest_ref[t * TOP_K + s]).start(priority=s % 2)
        return carry

    lax.fori_loop(0, n, start, 0)

    @pl.when(n == TM)
    def _():
        for s in range(TOP_K):
            pltpu.make_async_copy(x_ref, x_ref, sem).wait()

    @pl.when(n < TM)
    def _():
        def wait(t, carry):
            for s in range(TOP_K):
                row_copy(t, dest_ref[t * TOP_K + s]).wait()
            return carry

        lax.fori_loop(0, n, wait, 0)


def dispatch(counts, pstarts, dest_flat, x1p):
    grid_spec = pltpu.PrefetchScalarGridSpec(
        num_scalar_prefetch=2,
        grid=(N_TILES,),
        in_specs=[
            pl.BlockSpec((TM * TOP_K,), lambda i, c, p: (i,), memory_space=pltpu.SMEM),
            pl.BlockSpec((TM, HALF), lambda i, c, p: (i, 0)),
        ],
        out_specs=pl.BlockSpec(memory_space=pl.ANY),
        scratch_shapes=[pltpu.VMEM((8, HALF), jnp.uint32),
                        pltpu.SemaphoreType.DMA(()), pltpu.SemaphoreType.DMA(())],
    )
    return pl.pallas_call(
        _dispatch_kernel,
        grid_spec=grid_spec,
        out_shape=jax.ShapeDtypeStruct((MOE_NB * MOE_T, HALF), jnp.uint32),
        compiler_params=_cparams(("arbitrary",)),
        name="dispatch",
    )(counts, pstarts, dest_flat, x1p)


def _moe_kernel(blk_e_ref, nused_ref, x_ref, wg_ref, wu_ref, wd_ref, o_ref, wg_bf, wu_bf, wd_bf):
    i = pl.program_id(0)
    changed = jnp.logical_or(i == 0, blk_e_ref[i] != blk_e_ref[jnp.maximum(i - 1, 0)])

    @pl.when(jnp.logical_and(changed, i < nused_ref[0]))
    def _():
        wg_bf[...] = wg_ref[...].astype(BF16)
        wu_bf[...] = wu_ref[...].astype(BF16)
        wd_bf[...] = wd_ref[...].astype(BF16)

    @pl.when(i < nused_ref[0])
    def _():
        lo, hi = _unpack_bf16_pairs(x_ref[...])
        gate = _dot(lo, wg_bf[0:HALF, :]) + _dot(hi, wg_bf[HALF:D_MODEL, :])
        up = _dot(lo, wu_bf[0:HALF, :]) + _dot(hi, wu_bf[HALF:D_MODEL, :])
        hb = jax.nn.silu(gate) * up
        o_ref[...] = _dot(hb.astype(BF16), wd_bf[...])

    @pl.when(i >= nused_ref[0])
    def _():
        o_ref[...] = jnp.zeros_like(o_ref)


def moe_experts(blk_e, nused, xg, wg, wu, wd, layer):
    grid_spec = pltpu.PrefetchScalarGridSpec(
        num_scalar_prefetch=2,
        grid=(MOE_NB,),
        in_specs=[
            pl.BlockSpec((MOE_T, HALF), lambda i, be, nu: (jnp.minimum(i, nu[0] - 1), 0)),
            pl.BlockSpec((None, None, D_MODEL, D_EXPERT), lambda i, be, nu: (layer, be[i], 0, 0)),
            pl.BlockSpec((None, None, D_MODEL, D_EXPERT), lambda i, be, nu: (layer, be[i], 0, 0)),
            pl.BlockSpec((None, None, D_EXPERT, D_MODEL), lambda i, be, nu: (layer, be[i], 0, 0)),
        ],
        out_specs=pl.BlockSpec((MOE_T, D_MODEL), lambda i, be, nu: (i, 0)),
        scratch_shapes=[pltpu.VMEM((D_MODEL, D_EXPERT), BF16),
                        pltpu.VMEM((D_MODEL, D_EXPERT), BF16),
                        pltpu.VMEM((D_EXPERT, D_MODEL), BF16)],
    )
    return pl.pallas_call(
        _moe_kernel,
        grid_spec=grid_spec,
        out_shape=jax.ShapeDtypeStruct((MOE_NB * MOE_T, D_MODEL), F32),
        compiler_params=_cparams(("arbitrary",)),
        name="moe_experts",
    )(blk_e, nused, xg, wg, wu, wd)


TMC = 256
NC_TILES = -(-NT // TMC)
DEST_LEN = max(N_TILES * TM, NC_TILES * TMC) * TOP_K


def _combine_kernel(dest_ref, x1_ref, w_ref, yb_hbm, wg_ref, wu_ref, wd_ref, g_ref, b_ref, o_ref,
                    ybuf, sem):
    i = pl.program_id(0)
    n = jnp.minimum(TMC, NT - i * TMC)

    def row_copy(t, s):
        return pltpu.make_async_copy(yb_hbm.at[pl.ds(dest_ref[t * TOP_K + s], 1)],
                                     ybuf.at[s, pl.ds(t, 1)], sem)

    def start(t, carry):
        for s in range(TOP_K):
            row_copy(t, s).start(priority=s % 2)
        return carry

    lax.fori_loop(0, n, start, 0)

    x1 = x1_ref[...]
    xb = x1.astype(BF16)
    hb = jax.nn.silu(_dot(xb, wg_ref[...])) * _dot(xb, wu_ref[...])
    acc = _dot(hb.astype(BF16), wd_ref[...])

    @pl.when(n == TMC)
    def _():
        pltpu.make_async_copy(ybuf, ybuf, sem).wait()

    @pl.when(n < TMC)
    def _():
        def wait(t, carry):
            for s in range(TOP_K):
                row_copy(t, s).wait()
            return carry

        lax.fori_loop(0, n, wait, 0)

    w = w_ref[...]
    for s in range(TOP_K):
        acc = acc + _col(w, s) * ybuf[s]
    o_ref[...] = _ln(ALPHA * x1 + acc, g_ref[...], b_ref[...])


def combine_shared_ln2(dest_flat, x1, w, yb, wg, wu, wd, g, b):
    grid_spec = pl.GridSpec(
        grid=(NC_TILES,),
        in_specs=[
            pl.BlockSpec((TMC * TOP_K,), lambda i: (i,), memory_space=pltpu.SMEM),
            pl.BlockSpec((TMC, D_MODEL), lambda i: (i, 0)),
            pl.BlockSpec((TMC, LANE), lambda i: (i, 0)),
            pl.BlockSpec(memory_space=pl.ANY),
            _full((D_MODEL, D_EXPERT)), _full((D_MODEL, D_EXPERT)), _full((D_EXPERT, D_MODEL)),
            _full((1, D_MODEL)), _full((1, D_MODEL)),
        ],
        out_specs=pl.BlockSpec((TMC, D_MODEL), lambda i: (i, 0)),
        scratch_shapes=[pltpu.VMEM((TOP_K, TMC, D_MODEL), F32), pltpu.SemaphoreType.DMA(())],
    )
    return pl.pallas_call(
        _combine_kernel,
        grid_spec=grid_spec,
        out_shape=jax.ShapeDtypeStruct((NT, D_MODEL), F32),
        compiler_params=_cparams(("arbitrary",)),
        name="combine_shared_ln2",
    )(dest_flat, x1, w, yb, wg, wu, wd, g, b)


def _schedule(counts, e_sel, rank_sel):
    padded = (counts + MOE_T - 1) // MOE_T * MOE_T
    pends = jnp.cumsum(padded)
    pstarts = pends - padded
    onehot = e_sel[:, :, None] == jnp.arange(N_EXPERTS, dtype=jnp.int32)
    dest = rank_sel + jnp.sum(jnp.where(onehot, pstarts, 0), axis=-1)
    dest_flat = jnp.pad(dest.reshape(NK), (0, DEST_LEN - NK))
    blk_row0 = jnp.arange(MOE_NB, dtype=jnp.int32) * MOE_T
    blk_e = jnp.minimum(jnp.sum((pends[None, :] <= blk_row0[:, None]).astype(jnp.int32), axis=1),
                        N_EXPERTS - 1).astype(jnp.int32)
    nused = (pends[-1] // MOE_T).astype(jnp.int32).reshape(1)
    return pstarts.astype(jnp.int32), dest_flat.astype(jnp.int32), blk_e, nused


def _pad_rows(x, rows):
    b, r, c = x.shape
    return jnp.pad(x, ((0, 0), (rows - r, 0), (0, 0))).reshape(b * rows, c)


def _layer(h, st, lw):
    proj = in_proj(h, lw['w_in'])
    sinks = lw['sinks']
    zeros16 = jnp.zeros((BATCH * 16, GROUP_W), F32)
    zeros8 = jnp.zeros((BATCH * 8, C_CONV_CH), F32)

    meta_blk = lambda b, j=None: M_ROW0 // N_META + b
    mix = attn_call(sinks, proj, proj, (proj, proj), (proj, proj), nb=BATCH, length=N_META,
                    ch=N_META, nq=1, use_meta=False, chunk0=0, q_row0=M_ROW0,
                    kp_map=lambda b, j: 0, km_map=meta_blk,
                    kp_cols=(C_K // KV_W, C_V // KV_W), km_cols=(C_K // KV_W, C_V // KV_W),
                    mix=None, mix_rows=NT)
    nq_p = 4
    mix = attn_call(sinks, proj, proj, (proj, proj), (proj, proj), nb=BATCH, length=SEQ,
                    ch=64, nq=nq_p, use_meta=True, chunk0=0, q_row0=0,
                    kp_map=lambda b, j: jnp.maximum(b * (SEQ // WINDOW) + j * (64 * nq_p // WINDOW) - 1, 0),
                    km_map=meta_blk,
                    kp_cols=(C_K // KV_W, C_V // KV_W), km_cols=(C_K // KV_W, C_V // KV_W),
                    mix=mix, mix_rows=NT)
    mix = attn_call(sinks, proj, proj, (st['win_k'], st['win_v']), (st['meta_k'], st['meta_v']),
                    nb=DEC_BATCH, length=DEC_SEQ, ch=64, nq=1, use_meta=True, chunk0=2,
                    q_row0=S_ROW0, kp_map=lambda b, j: b, km_map=lambda b, j: b,
                    kp_cols=(0, 0), km_cols=(0, 0), mix=mix, mix_rows=NT)

    mix = pool_call(proj, zeros16, lambda b: b, 0, lw['pool_w'], lw['pool_scale'],
                    nb=BATCH, length=N_META, tb=N_META, row0=M_ROW0, ramp=True, mix=mix, mix_rows=NT)
    mix = pool_call(proj, proj, lambda b: M_ROW0 // 16 + b, C_U // GROUP_W, lw['pool_w'],
                    lw['pool_scale'], nb=BATCH, length=SEQ, tb=512, row0=0, ramp=False,
                    mix=mix, mix_rows=NT)
    mix = pool_call(proj, st['pool'], lambda b: b, 0, lw['pool_w'], lw['pool_scale'],
                    nb=DEC_BATCH, length=DEC_SEQ, tb=DEC_SEQ, row0=S_ROW0, ramp=False,
                    mix=mix, mix_rows=NT)

    ssd_w = lw['ssd']
    h0z = jnp.zeros((BATCH, 512, C_STATE), F32)
    mix, hc_m = ssd_call(proj, zeros8, lambda b: b, (0, 2, 3), h0z, ssd_w,
                         nb=BATCH, length=N_META, q=N_META, row0=M_ROW0, mix=mix, mix_rows=NT)
    meta_tail = lambda b: (M_ROW0 + 8) // 8 + 2 * b
    mix, hc_p = ssd_call(proj, proj, meta_tail, (C_XS // 512, C_B // 256, C_C // 256), hc_m, ssd_w,
                         nb=BATCH, length=SEQ, q=256, row0=0, mix=mix, mix_rows=NT)
    mix, hc_s = ssd_call(proj, st['ssm_conv'], lambda b: b, (0, 2, 3), st['ssm'], ssd_w,
                         nb=DEC_BATCH, length=DEC_SEQ, q=DEC_SEQ, row0=S_ROW0, mix=mix, mix_rows=NT)

    lru_w = lw['lru']
    l0z = jnp.zeros((BATCH, 1, GROUP_W), F32)
    mix, hd_m = lru_call(proj, zeros8, lambda b: b, 0, l0z, lru_w,
                         nb=BATCH, length=N_META, tb=N_META, row0=M_ROW0, mix=mix, mix_rows=NT)
    mix, hd_p = lru_call(proj, proj, meta_tail, C_RX // GROUP_W, hd_m, lru_w,
                         nb=BATCH, length=SEQ, tb=256, row0=0, mix=mix, mix_rows=NT)
    mix, hd_s = lru_call(proj, st['lru_conv'], lambda b: b, 0, st['lru'], lru_w,
                         nb=DEC_BATCH, length=DEC_SEQ, tb=DEC_SEQ, row0=S_ROW0, mix=mix, mix_rows=NT)

    x1, x1p, sc = out_proj(mix, h, lw['w_out'], lw['ln1_g'], lw['ln1_b'], lw['rw_hi'], lw['rw_lo'])

    e_sel, rank_sel, w_sel, cnt = router(sc, lw['router_bias'])
    counts = cnt[0, :N_EXPERTS].astype(jnp.int32)
    pstarts, dest_flat, blk_e, nused = _schedule(counts, e_sel[:, :TOP_K], rank_sel[:, :TOP_K])
    xg = dispatch(counts, pstarts, dest_flat, x1p)
    yb = moe_experts(blk_e, nused, xg, lw['wg'], lw['wu'], lw['wd'], lw['layer'])
    h_new = combine_shared_ln2(dest_flat, x1, w_sel, yb, lw['sh_wg'], lw['sh_wu'], lw['sh_wd'],
                               lw['ln2_g'], lw['ln2_b'])

    def tail(row0, nb, length, nrows, c0, width):
        return jnp.stack([proj[row0 + (b + 1) * length - nrows:row0 + (b + 1) * length, c0:c0 + width]
                          for b in range(nb)])

    kv4 = lambda x: x.reshape(x.shape[0], x.shape[1], A_KV_HEADS, HEAD_DIM)
    p_state = (
        kv4(tail(M_ROW0, BATCH, N_META, N_META, C_K, KV_W)),
        kv4(tail(M_ROW0, BATCH, N_META, N_META, C_V, KV_W)),
        kv4(tail(0, BATCH, SEQ, WINDOW, C_K, KV_W)), kv4(tail(0, BATCH, SEQ, WINDOW, C_V, KV_W)),
        tail(0, BATCH, SEQ, POOL_STATE, C_U, GROUP_W),
        tail(0, BATCH, SEQ, CONV_W - 1, C_XS, C_CONV_CH),
        hc_p.reshape(BATCH, C_HEADS, 64, C_STATE),
        tail(0, BATCH, SEQ, CONV_W - 1, C_RX, GROUP_W),
        hd_p.reshape(BATCH, GROUP_W),
    )
    s_state = (
        kv4(tail(S_ROW0, DEC_BATCH, DEC_SEQ, DEC_SEQ, C_K, KV_W)),
        kv4(tail(S_ROW0, DEC_BATCH, DEC_SEQ, DEC_SEQ, C_V, KV_W)),
        tail(S_ROW0, DEC_BATCH, DEC_SEQ, POOL_STATE, C_U, GROUP_W),
        tail(S_ROW0, DEC_BATCH, DEC_SEQ, CONV_W - 1, C_XS, C_CONV_CH),
        hc_s.reshape(DEC_BATCH, C_HEADS, 64, C_STATE),
        tail(S_ROW0, DEC_BATCH, DEC_SEQ, CONV_W - 1, C_RX, GROUP_W),
        hd_s.reshape(DEC_BATCH, GROUP_W),
    )
    return h_new, p_state, s_state


def _block_diag(w):
    z = jnp.zeros((D_BLOCK_W, D_BLOCK_W), w.dtype)
    return jnp.stack([jnp.block([[w[2 * s], z], [z, w[2 * s + 1]]]) for s in range(4)])


def _pad_lanes(v, width=LANE):
    return jnp.pad(v, (0, width - v.shape[0])).reshape(1, width)


def kernel(x_prompt, x_sample, cache_attn_meta_k, cache_attn_meta_v, cache_attn_k, cache_attn_v, state_pool, state_ssm_conv, state_ssm, state_lru_conv, state_lru, meta_tokens, ln_in_g, ln_in_b, w_in, w_out, attn_sinks, pool_w, pool_scale, ssm_conv_w, ssm_conv_b, ssm_dt_bias, ssm_a_log, ssm_d, ssm_norm_g, lru_conv_w, lru_conv_b, lru_wr, lru_br, lru_wi, lru_bi, lru_lambda, ln1_g, ln1_b, ln2_g, ln2_b, router_w, router_bias, exp_w_gate, exp_w_up, exp_w_down, sh_w_gate, sh_w_up, sh_w_down):
    row = lambda v: v.reshape(1, -1).astype(F32)
    h = ln_in(x_prompt.reshape(P_ROWS, D_MODEL), x_sample.reshape(S_ROWS, D_MODEL),
              meta_tokens.astype(F32), row(ln_in_g), row(ln_in_b))
    p_states, s_states = [], []
    for i in range(DEPTH):
        wi = w_in[i]
        s0 = 0
        parts = {}
        for name, size in zip(('q', 'k', 'v', 'u', 'z', 'xbc', 'dt', 'rx', 'rg'),
                              (512, 128, 128, 512, 512, 1024, 8, 512, 512)):
            parts[name] = wi[:, s0:s0 + size]
            s0 += size
        w_in_p = jnp.concatenate(
            [parts[n] for n in ('q', 'u', 'z', 'rx', 'rg', 'xbc', 'k', 'v', 'dt')]
            + [jnp.zeros((D_MODEL, PROJ_P - C_DT - C_HEADS), F32)], axis=1).astype(BF16)
        rw = jnp.pad(router_w[i].astype(F32), ((0, 0), (0, LANE - N_EXPERTS)))
        rw_hi = rw.astype(BF16)
        rw_lo = (rw - rw_hi.astype(F32)).astype(BF16)
        cw = ssm_conv_w[i].astype(F32)
        cb = ssm_conv_b[i].astype(F32)
        lw = dict(
            w_in=w_in_p, w_out=w_out[i].astype(BF16), sinks=attn_sinks[i].astype(F32),
            pool_w=pool_w[i].astype(BF16), pool_scale=row(pool_scale[i]),
            ssd=dict(cw_x=cw[:, :512], cw_b=cw[:, 512:768], cw_c=cw[:, 768:],
                     cb_x=row(cb[:512]), cb_b=row(cb[512:768]), cb_c=row(cb[768:]),
                     dt_bias=_pad_lanes(ssm_dt_bias[i].astype(F32)),
                     a_log=_pad_lanes(ssm_a_log[i].astype(F32)),
                     d_skip=row(jnp.repeat(ssm_d[i].astype(F32), 64)),
                     norm_g=row(ssm_norm_g[i])),
            lru=dict(cw=lru_conv_w[i].astype(F32), cb=row(lru_conv_b[i]),
                     wr=_block_diag(lru_wr[i]).astype(BF16), wi=_block_diag(lru_wi[i]).astype(BF16),
                     br=row(lru_br[i]), bi=row(lru_bi[i]), lam=row(lru_lambda[i])),
            ln1_g=row(ln1_g[i]), ln1_b=row(ln1_b[i]), ln2_g=row(ln2_g[i]), ln2_b=row(ln2_b[i]),
            rw_hi=rw_hi, rw_lo=rw_lo, router_bias=_pad_lanes(router_bias[i].astype(F32)),
            wg=exp_w_gate, wu=exp_w_up, wd=exp_w_down, layer=i,
            sh_wg=sh_w_gate[i].astype(BF16), sh_wu=sh_w_up[i].astype(BF16),
            sh_wd=sh_w_down[i].astype(BF16),
        )
        st = dict(
            meta_k=cache_attn_meta_k[i].reshape(DEC_BATCH * N_META, KV_W),
            meta_v=cache_attn_meta_v[i].reshape(DEC_BATCH * N_META, KV_W),
            win_k=cache_attn_k[i].reshape(DEC_BATCH * WINDOW, KV_W),
            win_v=cache_attn_v[i].reshape(DEC_BATCH * WINDOW, KV_W),
            pool=_pad_rows(state_pool[i], 16),
            ssm_conv=_pad_rows(state_ssm_conv[i], 8),
            ssm=state_ssm[i].reshape(DEC_BATCH, 512, C_STATE),
            lru_conv=_pad_rows(state_lru_conv[i], 8),
            lru=state_lru[i].reshape(DEC_BATCH, 1, GROUP_W),
        )
        h, ps, ss = _layer(h, st, lw)
        p_states.append(ps)
        s_states.append(ss)
    stk = lambda sts, j: jnp.stack([s[j] for s in sts])
    y_prompt = h[:P_ROWS].reshape(BATCH, SEQ, D_MODEL)
    y_sample = h[S_ROW0:S_ROW0 + S_ROWS].reshape(DEC_BATCH, DEC_SEQ, D_MODEL)
    return ((y_prompt, y_sample)
            + tuple(stk(p_states, j) for j in range(9))
            + tuple(stk(s_states, j) for j in range(7)))
```

```python
import functools
import math

import jax
import jax.numpy as jnp
from jax import lax
from jax.experimental import pallas as pl
from jax.experimental.pallas import tpu as pltpu

F32 = jnp.float32
BF16 = jnp.bfloat16

D_MODEL = 2048
BATCH = 4
SEQ = 4096
DEPTH = 4
DEC_BATCH = 16
DEC_SEQ = 64
N_META = 16
GROUP_W = 512
HEAD_DIM = 64
A_HEADS = 8
A_KV_HEADS = 2
A_GROUP = 4
KV_W = 128
WINDOW = 128
POOL_SIZES = (2, 4, 8, 16)
POOL_GW = 128
POOL_STATE = 15
C_HEADS = 8
C_STATE = 128
C_CONV_CH = 1024
CONV_W = 4
D_BLOCKS = 8
D_BLOCK_W = 64
LRU_C = 8.0
N_EXPERTS = 64
TOP_K = 8
N_GROUP = 8
TOPK_GROUP = 4
D_EXPERT = 512
ROUTE_SCALE = 2.5
ALPHA = (2 * DEPTH) ** 0.25
LN_EPS = 1e-5

LANE = 128
SUBLANE = 8
VMEM_LIMIT = 56 * 1024 * 1024

P_ROWS = BATCH * SEQ
S_ROWS = DEC_BATCH * DEC_SEQ
M_ROWS = BATCH * N_META
S_ROW0 = P_ROWS
M_ROW0 = P_ROWS + S_ROWS
NT = P_ROWS + S_ROWS + M_ROWS
TM = 512
N_TILES = -(-NT // TM)

C_Q, C_U, C_Z, C_RX, C_RG, C_XS, C_B, C_C, C_K, C_V, C_DT = (
    0, 512, 1024, 1536, 2048, 2560, 3072, 3328, 3584, 3712, 3840)
PROJ_P = 4096
PROJ_TN = 1024

MOE_T = 512
NK = NT * TOP_K
MOE_NB = (NK + N_EXPERTS * (MOE_T - 1) + MOE_T - 1) // MOE_T


def _cparams(sem):
    return pltpu.CompilerParams(dimension_semantics=sem, vmem_limit_bytes=VMEM_LIMIT)


def _ln(x, g, b):
    mu = jnp.mean(x, axis=-1, keepdims=True)
    xc = x - mu
    var = jnp.mean(xc * xc, axis=-1, keepdims=True)
    return xc * lax.rsqrt(var + LN_EPS) * g + b


def _dot(a, b):
    return jnp.dot(a, b, preferred_element_type=F32)


def _dot_nt(a, b, precision=None):
    return lax.dot_general(a, b, (((1,), (1,)), ((), ())), precision=precision,
                           preferred_element_type=F32)


def _full(shape):
    nd = len(shape)
    return pl.BlockSpec(shape, lambda *_: (0,) * nd)


MIX_W = 4 * GROUP_W


def _skip_ref(kern, idx, *refs):
    return kern(*refs[:idx], *refs[idx + 1:])


def _mixer_call(kern, grid, in_specs, args, y_spec, more_out_specs, more_out_shapes, scratch,
                mix, mix_rows, name):
    in_specs = list(in_specs)
    args = tuple(args)
    aliases = {}
    if mix is not None:
        n_in = len(in_specs)
        kern = functools.partial(_skip_ref, kern, n_in)
        in_specs.append(pl.BlockSpec(memory_space=pl.ANY))
        args = args + (mix,)
        aliases = {n_in: 0}
    return pl.pallas_call(
        kern,
        grid=grid,
        in_specs=in_specs,
        out_specs=[y_spec] + list(more_out_specs),
        out_shape=[jax.ShapeDtypeStruct((mix_rows, MIX_W), F32)] + list(more_out_shapes),
        scratch_shapes=scratch,
        input_output_aliases=aliases,
        compiler_params=_cparams(("arbitrary",) * len(grid)),
        name=name,
    )(*args)


def _ln_in_kernel(xp_ref, xs_ref, meta_ref, g_ref, b_ref, o_ref):
    i = pl.program_id(0)
    g = g_ref[...]
    b = b_ref[...]
    n_p = P_ROWS // TM
    n_s = S_ROWS // TM

    @pl.when(i < n_p)
    def _():
        o_ref[...] = _ln(xp_ref[...], g, b)

    @pl.when((i >= n_p) & (i < n_p + n_s))
    def _():
        o_ref[...] = _ln(xs_ref[...], g, b)

    @pl.when(i == n_p + n_s)
    def _():
        m = _ln(meta_ref[...], g, b)
        for r in range(BATCH):
            o_ref[N_META * r:N_META * (r + 1), :] = m


def ln_in(xp2, xs2, meta, g, b):
    n_p = P_ROWS // TM
    n_s = S_ROWS // TM
    return pl.pallas_call(
        _ln_in_kernel,
        grid=(N_TILES,),
        in_specs=[
            pl.BlockSpec((TM, D_MODEL), lambda i: (jnp.minimum(i, n_p - 1), 0)),
            pl.BlockSpec((TM, D_MODEL), lambda i: (jnp.clip(i - n_p, 0, n_s - 1), 0)),
            _full((N_META, D_MODEL)),
            _full((1, D_MODEL)),
            _full((1, D_MODEL)),
        ],
        out_specs=pl.BlockSpec((TM, D_MODEL), lambda i: (i, 0)),
        out_shape=jax.ShapeDtypeStruct((NT, D_MODEL), F32),
        compiler_params=_cparams(("arbitrary",)),
        name="ln_in",
    )(xp2, xs2, meta, g, b)


def _in_proj_kernel(x_ref, w_ref, o_ref):
    o_ref[...] = _dot(x_ref[...].astype(BF16), w_ref[...])


PROJ_TM = 1024


def in_proj(h, w_bf):
    return pl.pallas_call(
        _in_proj_kernel,
        grid=(-(-NT // PROJ_TM), PROJ_P // PROJ_TN),
        in_specs=[
            pl.BlockSpec((PROJ_TM, D_MODEL), lambda i, n: (i, 0)),
            pl.BlockSpec((D_MODEL, PROJ_TN), lambda i, n: (0, n)),
        ],
        out_specs=pl.BlockSpec((PROJ_TM, PROJ_TN), lambda i, n: (i, n)),
        out_shape=jax.ShapeDtypeStruct((NT, PROJ_P), F32),
        compiler_params=_cparams(("arbitrary", "arbitrary")),
        name="in_proj",
    )(h, w_bf)


ATT_PAD = 64
ATT_WIN = ATT_PAD + WINDOW + 64


def _attn_kernel(sink_ref, q_ref, kc_ref, vc_ref, kp_ref, vp_ref, km_ref, vm_ref, o_ref,
                 kbuf, vbuf, *, ch, nq, use_meta, chunk0):
    j = pl.program_id(1)
    tq = ch * nq
    zpad = jnp.zeros((ATT_PAD - N_META, KV_W), BF16)
    kbuf[0:N_META, :] = km_ref[...].astype(BF16)
    kbuf[N_META:ATT_PAD, :] = zpad
    vbuf[0:N_META, :] = vm_ref[...].astype(BF16)
    vbuf[N_META:ATT_PAD, :] = zpad
    kbuf[ATT_PAD:ATT_PAD + WINDOW, :] = kp_ref[...].astype(BF16)
    vbuf[ATT_PAD:ATT_PAD + WINDOW, :] = vp_ref[...].astype(BF16)
    kbuf[ATT_PAD + WINDOW:ATT_PAD + WINDOW + tq, :] = kc_ref[...].astype(BF16)
    vbuf[ATT_PAD + WINDOW:ATT_PAD + WINDOW + tq, :] = vc_ref[...].astype(BF16)
    if ch < 64:
        zc = jnp.zeros((64 - ch, KV_W), BF16)
        kbuf[ATT_PAD + WINDOW + tq:ATT_PAD + WINDOW + tq + 64 - ch, :] = zc
        vbuf[ATT_PAD + WINDOW + tq:ATT_PAD + WINDOW + tq + 64 - ch, :] = zc

    rows = A_GROUP * ch
    col = lax.broadcasted_iota(jnp.int32, (rows, ATT_WIN), 1)
    row = lax.broadcasted_iota(jnp.int32, (rows, 1), 0)
    lane = lax.broadcasted_iota(jnp.int32, (ch, LANE), 1)
    lo = lane < HEAD_DIM

    for i in range(nq):
        c = chunk0 + j * nq + i
        first_band = ATT_PAD + 64 * jnp.maximum(2 - c, 0)
        valid = (col >= first_band) & (col < ATT_PAD + WINDOW + ch)
        if use_meta:
            valid = valid | (col < N_META)
        kcat = jnp.concatenate(
            [kbuf[0:ATT_PAD, :], kbuf[ATT_PAD + ch * i:ATT_PAD + ch * i + WINDOW + 64, :]], axis=0)
        vcat = jnp.concatenate(
            [vbuf[0:ATT_PAD, :], vbuf[ATT_PAD + ch * i:ATT_PAD + ch * i + WINDOW + 64, :]], axis=0)
        qi = q_ref[ch * i:ch * (i + 1), :] * (HEAD_DIM ** -0.5)
        tiles = [qi[:, LANE * t:LANE * (t + 1)] for t in range(A_HEADS // 2)]
        out_tiles = [None] * (A_HEADS // 2)
        for kh in range(A_KV_HEADS):
            qs = []
            for r in range(A_GROUP):
                h = A_GROUP * kh + r
                t = tiles[h // 2]
                if h % 2 != kh:
                    t = pltpu.roll(t, HEAD_DIM, axis=1)
                keep = lo if kh == 0 else jnp.logical_not(lo)
                qs.append(jnp.where(keep, t, 0.0).astype(BF16))
            qz = jnp.concatenate(qs, axis=0)
            s = _dot_nt(qz, kcat)
            s = jnp.where(valid, s, -jnp.inf)
            sink = jnp.zeros((rows, 1), F32)
            for r in range(A_GROUP):
                sink = jnp.where((row >= r * ch) & (row < (r + 1) * ch),
                                 sink_ref[A_GROUP * kh + r], sink)
            m = jnp.maximum(jnp.max(s, axis=-1, keepdims=True), sink)
            p = jnp.exp(s - m)
            den = jnp.sum(p, axis=-1, keepdims=True) + jnp.exp(sink - m)
            probs = (p / den).astype(BF16)
            o = _dot(probs, vcat)
            for r in range(A_GROUP):
                h = A_GROUP * kh + r
                oh = o[r * ch:(r + 1) * ch, :]
                if h % 2 != kh:
                    oh = pltpu.roll(oh, HEAD_DIM, axis=1)
                keep = lo if h % 2 == 0 else jnp.logical_not(lo)
                prev = out_tiles[h // 2]
                out_tiles[h // 2] = jnp.where(keep, oh, 0.0 if prev is None else prev)
        o_ref[ch * i:ch * (i + 1), :] = jnp.concatenate(out_tiles, axis=1)


def attn_call(sinks, q_src, kc_src, kp_src, km_src, *, nb, length, ch, nq, use_meta, chunk0,
              q_row0, kp_map, km_map, kp_cols, km_cols, mix, mix_rows):
    tq = ch * nq
    nj = length // tq
    qb0 = q_row0 // tq
    kern = functools.partial(_attn_kernel, ch=ch, nq=nq, use_meta=use_meta, chunk0=chunk0)
    in_specs = [
        pl.BlockSpec(memory_space=pltpu.SMEM),
        pl.BlockSpec((tq, GROUP_W), lambda b, j: (qb0 + b * nj + j, C_Q // GROUP_W)),
        pl.BlockSpec((tq, KV_W), lambda b, j: (qb0 + b * nj + j, C_K // KV_W)),
        pl.BlockSpec((tq, KV_W), lambda b, j: (qb0 + b * nj + j, C_V // KV_W)),
        pl.BlockSpec((WINDOW, KV_W), lambda b, j: (kp_map(b, j), kp_cols[0])),
        pl.BlockSpec((WINDOW, KV_W), lambda b, j: (kp_map(b, j), kp_cols[1])),
        pl.BlockSpec((N_META, KV_W), lambda b, j: (km_map(b, j), km_cols[0])),
        pl.BlockSpec((N_META, KV_W), lambda b, j: (km_map(b, j), km_cols[1])),
    ]
    return _mixer_call(
        kern, (nb, nj), in_specs,
        (sinks, q_src, kc_src, kc_src, kp_src[0], kp_src[1], km_src[0], km_src[1]),
        pl.BlockSpec((tq, GROUP_W), lambda b, j: (qb0 + b * nj + j, 0)), [], [],
        [pltpu.VMEM((ATT_PAD + WINDOW + tq + 64, KV_W), BF16),
         pltpu.VMEM((ATT_PAD + WINDOW + tq + 64, KV_W), BF16)],
        mix, mix_rows, "attn")[0]


def _pool_kernel(u_ref, prev_ref, w_ref, scale_ref, o_ref, buf, *, tb, ramp):
    j = pl.program_id(1)

    @pl.when(j == 0)
    def _():
        buf[0:16, :] = prev_ref[...]

    buf[16:16 + tb, :] = u_ref[...]
    pos = j * tb + lax.broadcasted_iota(jnp.int32, (tb, 1), 0)
    outs = []
    for g, win in enumerate(POOL_SIZES):
        sl = slice(g * POOL_GW, (g + 1) * POOL_GW)
        tot = buf[16:16 + tb, sl]
        for k in range(1, win):
            tot = tot + buf[16 - k:16 - k + tb, sl]
        if ramp:
            cnt = jnp.minimum(win, pos + 1).astype(F32)
            mean = tot / cnt
        else:
            mean = tot * (1.0 / win)
        d = mean - buf[16:16 + tb, sl]
        outs.append(_dot(d.astype(BF16), w_ref[g]))
    o_ref[...] = jnp.concatenate(outs, axis=1) * scale_ref[...]
    buf[0:16, :] = buf[tb:tb + 16, :]


def pool_call(proj, prev_src, prev_map, prev_col, w_bf, scale, *, nb, length, tb, row0, ramp,
              mix, mix_rows):
    nj = length // tb
    rb0 = row0 // tb
    kern = functools.partial(_pool_kernel, tb=tb, ramp=ramp)
    in_specs = [
        pl.BlockSpec((tb, GROUP_W), lambda b, j: (rb0 + b * nj + j, C_U // GROUP_W)),
        pl.BlockSpec((16, GROUP_W), lambda b, j: (prev_map(b), prev_col)),
        _full((4, POOL_GW, POOL_GW)),
        _full((1, GROUP_W)),
    ]
    return _mixer_call(
        kern, (nb, nj), in_specs, (proj, prev_src, w_bf, scale),
        pl.BlockSpec((tb, GROUP_W), lambda b, j: (rb0 + b * nj + j, 1)), [], [],
        [pltpu.VMEM((tb + 16, GROUP_W), F32)], mix, mix_rows, "pool")[0]


def _conv_block(buf, x_ref, w_ref, b_ref, tb):
    buf[8:8 + tb, :] = x_ref[...]
    acc = b_ref[...] + buf[5:5 + tb, :] * w_ref[0:1, :]
    for k in range(1, CONV_W):
        acc = acc + buf[5 + k:5 + k + tb, :] * w_ref[k:k + 1, :]
    return acc


def _conv_carry(buf, tb):
    buf[0:8, :] = buf[tb:tb + 8, :]


def _col(x, h):
    lane = lax.broadcasted_iota(jnp.int32, x.shape, 1)
    return jnp.sum(jnp.where(lane == h, x, 0.0), axis=1, keepdims=True)


def _ssd_kernel(xs_ref, bm_ref, cm_ref, dt_ref, z_ref, px_ref, pb_ref, pc_ref, h0_ref,
                wx_ref, wb_ref, wc_ref, bx_ref, bb_ref, bc_ref,
                dtb_ref, alog_ref, dskip_ref, ng_ref,
                y_ref, hout_ref, bufx, bufb, bufc, hst, *, q):
    j = pl.program_id(1)
    nj = pl.num_programs(1)
    hi = lax.Precision.HIGHEST

    @pl.when(j == 0)
    def _():
        bufx[0:8, :] = px_ref[...]
        bufb[0:8, :] = pb_ref[...]
        bufc[0:8, :] = pc_ref[...]
        hst[...] = h0_ref[...]

    xs = jax.nn.silu(_conv_block(bufx, xs_ref, wx_ref, bx_ref, q))
    bm = jax.nn.silu(_conv_block(bufb, bm_ref, wb_ref, bb_ref, q))
    cm = jax.nn.silu(_conv_block(bufc, cm_ref, wc_ref, bc_ref, q))
    _conv_carry(bufx, q)
    _conv_carry(bufb, q)
    _conv_carry(bufc, q)

    lane1 = lax.broadcasted_iota(jnp.int32, (1, LANE), 1)
    hmask = lane1 < C_HEADS
    dt = jnp.where(hmask, jax.nn.softplus(dt_ref[...] + dtb_ref[...]), 0.0)
    a = jnp.where(hmask, -jnp.exp(alog_ref[...]), 0.0)
    dta = dt * a
    ri = lax.broadcasted_iota(jnp.int32, (q, q), 0)
    ci = lax.broadcasted_iota(jnp.int32, (q, q), 1)
    tri = ri >= ci
    cum = jnp.dot(tri.astype(F32), dta, precision=hi, preferred_element_type=F32)
    eye = (lax.broadcasted_iota(jnp.int32, (LANE, LANE), 0)
           == lax.broadcasted_iota(jnp.int32, (LANE, LANE), 1)).astype(F32)
    cum_t = _dot_nt(eye, cum, precision=hi)
    dt_t = _dot_nt(eye, dt, precision=hi)
    ecum = jnp.exp(cum)
    cum_last = cum[q - 1:q, :]
    te = jnp.exp(cum_last - cum) * dt

    lane = lax.broadcasted_iota(jnp.int32, (q, LANE), 1)
    lo = lane < 64
    bm_bf = bm.astype(BF16)
    cm_bf = cm.astype(BF16)
    cb = [_dot_nt(cm_bf[:, LANE * g:LANE * (g + 1)], bm_bf[:, LANE * g:LANE * (g + 1)])
          for g in range(2)]
    yoff = [_dot_nt(cm_bf[:, LANE * g:LANE * (g + 1)], hst[256 * g:256 * (g + 1), :].astype(BF16))
            for g in range(2)]

    y_tiles = []
    xw_tiles = []
    for k in range(C_HEADS // 2):
        g = k // 2
        x_pair = xs[:, LANE * k:LANE * (k + 1)]
        ydiag = None
        for par in range(2):
            h = 2 * k + par
            seg = _col(cum, h) - cum_t[h:h + 1, :]
            lm = jnp.exp(jnp.where(tri, seg, -jnp.inf))
            mm = (cb[g] * lm * dt_t[h:h + 1, :]).astype(BF16)
            xm = jnp.where(lo if par == 0 else jnp.logical_not(lo), x_pair, 0.0).astype(BF16)
            part = _dot(mm, xm)
            ydiag = part if ydiag is None else ydiag + part
        e_pair = jnp.where(lo, _col(ecum, 2 * k), _col(ecum, 2 * k + 1))
        te_pair = jnp.where(lo, _col(te, 2 * k), _col(te, 2 * k + 1))
        kk = k % 2
        y_tiles.append(ydiag + yoff[g][:, LANE * kk:LANE * (kk + 1)] * e_pair
                       + dskip_ref[:, LANE * k:LANE * (k + 1)] * x_pair)
        xw_tiles.append((x_pair * te_pair).astype(BF16))

    eye2 = (lax.broadcasted_iota(jnp.int32, (256, 256), 0)
            == lax.broadcasted_iota(jnp.int32, (256, 256), 1)).astype(BF16)
    for g in range(2):
        xw = jnp.concatenate(xw_tiles[2 * g:2 * g + 2], axis=1)
        xw_t = _dot_nt(eye2, xw).astype(BF16)
        s_new = _dot(xw_t, bm_bf[:, LANE * g:LANE * (g + 1)])
        dec = jnp.concatenate(
            [jnp.broadcast_to(jnp.exp(cum_t[4 * g + r:4 * g + r + 1, q - 1:q]), (64, LANE))
             for r in range(4)], axis=0)
        hst[256 * g:256 * (g + 1), :] = dec * hst[256 * g:256 * (g + 1), :] + s_new

    y = jnp.concatenate(y_tiles, axis=1) * jax.nn.silu(z_ref[...])
    y = y * lax.rsqrt(jnp.mean(y * y, axis=-1, keepdims=True) + 1e-6) * ng_ref[...]
    y_ref[...] = y

    @pl.when(j == nj - 1)
    def _():
        hout_ref[...] = hst[...]


def ssd_call(proj, prev_src, prev_map, prev_cols, h0, lw, *, nb, length, q, row0, mix, mix_rows):
    nj = length // q
    rb0 = row0 // q
    kern = functools.partial(_ssd_kernel, q=q)
    blk = lambda width, col: pl.BlockSpec((q, width), lambda b, j: (rb0 + b * nj + j, col // width))
    pblk = lambda width, col: pl.BlockSpec((8, width), lambda b, j: (prev_map(b), col))
    in_specs = [
        blk(512, C_XS), blk(256, C_B), blk(256, C_C), blk(LANE, C_DT), blk(512, C_Z),
        pblk(512, prev_cols[0]), pblk(256, prev_cols[1]), pblk(256, prev_cols[2]),
        pl.BlockSpec((None, 512, C_STATE), lambda b, j: (b, 0, 0)),
        _full((CONV_W, 512)), _full((CONV_W, 256)), _full((CONV_W, 256)),
        _full((1, 512)), _full((1, 256)), _full((1, 256)),
        _full((1, LANE)), _full((1, LANE)), _full((1, 512)), _full((1, 512)),
    ]
    args = (proj, proj, proj, proj, proj, prev_src, prev_src, prev_src, h0,
            lw['cw_x'], lw['cw_b'], lw['cw_c'], lw['cb_x'], lw['cb_b'], lw['cb_c'],
            lw['dt_bias'], lw['a_log'], lw['d_skip'], lw['norm_g'])
    return _mixer_call(
        kern, (nb, nj), in_specs, args,
        pl.BlockSpec((q, GROUP_W), lambda b, j: (rb0 + b * nj + j, 2)),
        [pl.BlockSpec((None, 512, C_STATE), lambda b, j: (b, 0, 0))],
        [jax.ShapeDtypeStruct((nb, 512, C_STATE), F32)],
        [pltpu.VMEM((q + 8, 512), F32), pltpu.VMEM((q + 8, 256), F32),
         pltpu.VMEM((q + 8, 256), F32), pltpu.VMEM((512, C_STATE), F32)],
        mix, mix_rows, "ssd")


def _lru_kernel(rx_ref, rg_ref, prev_ref, h0_ref, cw_ref, cb_ref, wr_ref, wi_ref,
                br_ref, bi_ref, lam_ref, y_ref, hout_ref, buf, hc, *, tb):
    j = pl.program_id(1)
    nj = pl.num_programs(1)

    @pl.when(j == 0)
    def _():
        buf[0:8, :] = prev_ref[...]
        hc[...] = jnp.broadcast_to(h0_ref[...], hc.shape)

    xc = _conv_block(buf, rx_ref, cw_ref, cb_ref, tb)
    _conv_carry(buf, tb)
    rs, gs = [], []
    for s in range(GROUP_W // LANE):
        xb = xc[:, LANE * s:LANE * (s + 1)].astype(BF16)
        rs.append(_dot(xb, wr_ref[s]))
        gs.append(_dot(xb, wi_ref[s]))
    r = jax.nn.sigmoid(jnp.concatenate(rs, axis=1) + br_ref[...])
    gi = jax.nn.sigmoid(jnp.concatenate(gs, axis=1) + bi_ref[...])
    log_a = -LRU_C * r * jax.nn.softplus(-lam_ref[...])
    a = jnp.exp(log_a)
    u = jnp.sqrt(jnp.maximum(1.0 - jnp.exp(2.0 * log_a), 0.0)) * (gi * xc)
    t = lax.broadcasted_iota(jnp.int32, (tb, 1), 0)
    d = 1
    while d < tb:
        a_sh = jnp.where(t >= d, pltpu.roll(a, d, axis=0), 1.0)
        u_sh = jnp.where(t >= d, pltpu.roll(u, d, axis=0), 0.0)
        u = a * u_sh + u
        a = a * a_sh
        d *= 2
    h = u + a * hc[0:1, :]
    y_ref[...] = h * jax.nn.gelu(rg_ref[...])
    hc[...] = jnp.broadcast_to(h[tb - 1:tb, :], hc.shape)

    @pl.when(j == nj - 1)
    def _():
        hout_ref[...] = h[tb - 1:tb, :]


def lru_call(proj, prev_src, prev_map, prev_col, h0, lw, *, nb, length, tb, row0, mix, mix_rows):
    nj = length // tb
    rb0 = row0 // tb
    kern = functools.partial(_lru_kernel, tb=tb)
    in_specs = [
        pl.BlockSpec((tb, GROUP_W), lambda b, j: (rb0 + b * nj + j, C_RX // GROUP_W)),
        pl.BlockSpec((tb, GROUP_W), lambda b, j: (rb0 + b * nj + j, C_RG // GROUP_W)),
        pl.BlockSpec((8, GROUP_W), lambda b, j: (prev_map(b), prev_col)),
        pl.BlockSpec((None, 1, GROUP_W), lambda b, j: (b, 0, 0)),
        _full((CONV_W, GROUP_W)), _full((1, GROUP_W)),
        _full((4, LANE, LANE)), _full((4, LANE, LANE)),
        _full((1, GROUP_W)), _full((1, GROUP_W)), _full((1, GROUP_W)),
    ]
    args = (proj, proj, prev_src, h0, lw['cw'], lw['cb'], lw['wr'], lw['wi'],
            lw['br'], lw['bi'], lw['lam'])
    return _mixer_call(
        kern, (nb, nj), in_specs, args,
        pl.BlockSpec((tb, GROUP_W), lambda b, j: (rb0 + b * nj + j, 3)),
        [pl.BlockSpec((None, 1, GROUP_W), lambda b, j: (b, 0, 0))],
        [jax.ShapeDtypeStruct((nb, 1, GROUP_W), F32)],
        [pltpu.VMEM((tb + 8, GROUP_W), F32), pltpu.VMEM((8, GROUP_W), F32)],
        mix, mix_rows, "lru")


HALF = D_MODEL // 2


def _pack_bf16_pairs(x):
    bits = lax.bitcast_convert_type(x.astype(BF16).astype(F32), jnp.uint32)
    return (bits[:, :HALF] >> 16) | (bits[:, HALF:] & jnp.uint32(0xFFFF0000))


def _unpack_pairs_f32(w):
    lo = lax.bitcast_convert_type(w << 16, F32)
    hi = lax.bitcast_convert_type(w & jnp.uint32(0xFFFF0000), F32)
    return lo, hi


def _unpack_bf16_pairs(w):
    lo, hi = _unpack_pairs_f32(w)
    return lo.astype(BF16), hi.astype(BF16)


DMA_UNROLL = 8

ROW_TILE = HALF // LANE


def _store_tile_rows(ref, row0, mat):
    n = mat.shape[0]
    for j in range(ROW_TILE):
        ref[pl.ds(row0 * ROW_TILE + j, n, stride=ROW_TILE), :] = mat[:, LANE * j:LANE * (j + 1)]


def _load_tile_cols(ref, n, j):
    return ref[pl.ds(j, n, stride=ROW_TILE), :]


OUT_SUB = 256


def _out_proj_kernel(mix_ref, h_ref, w_ref, g_ref, b_ref, rw_ref, x1_ref, xp_ref, sc_ref):
    for r in range(TM // OUT_SUB):
        rows = slice(r * OUT_SUB, (r + 1) * OUT_SUB)
        y = _dot(mix_ref[rows, :].astype(BF16), w_ref[...])
        x1 = _ln(ALPHA * h_ref[rows, :] + y, g_ref[...], b_ref[...])
        x1_ref[rows, :] = x1
        _store_tile_rows(xp_ref, r * OUT_SUB, _pack_bf16_pairs(x1))
        logits = lax.dot_general(x1, rw_ref[...], (((1,), (0,)), ((), ())),
                                 preferred_element_type=F32)
        sc_ref[rows, :] = jax.nn.sigmoid(logits)


def out_proj(mix, h, w_bf, g, b, rw):
    return pl.pallas_call(
        _out_proj_kernel,
        grid=(N_TILES,),
        in_specs=[
            pl.BlockSpec((TM, D_MODEL), lambda i: (i, 0)),
            pl.BlockSpec((TM, D_MODEL), lambda i: (i, 0)),
            _full((D_MODEL, D_MODEL)),
            _full((1, D_MODEL)), _full((1, D_MODEL)),
            _full((D_MODEL, LANE)),
        ],
        out_specs=[pl.BlockSpec((TM, D_MODEL), lambda i: (i, 0)),
                   pl.BlockSpec((TM * ROW_TILE, LANE), lambda i: (i, 0)),
                   pl.BlockSpec((TM, LANE), lambda i: (i, 0))],
        out_shape=[jax.ShapeDtypeStruct((NT, D_MODEL), F32),
                   jax.ShapeDtypeStruct((NT * ROW_TILE, LANE), jnp.uint32),
                   jax.ShapeDtypeStruct((NT, LANE), F32)],
        compiler_params=_cparams(("arbitrary",)),
        name="out_proj",
    )(mix, h, w_bf, g, b, rw)


_BIG = 4096


def _group_allreduce(x, lane, op):
    for sh in (1, 2, 4):
        up = pltpu.roll(x, sh, axis=1)
        dn = pltpu.roll(x, LANE - sh, axis=1)
        x = op(x, jnp.where((lane & sh) != 0, up, dn))
    return x


def _router_kernel(sc_ref, bias_ref, e_ref, r_ref, w_ref, cnt_ref, run):
    i = pl.program_id(0)

    @pl.when(i == 0)
    def _():
        run[...] = jnp.zeros_like(run)

    sc = sc_ref[...]
    lane = lax.broadcasted_iota(jnp.int32, (TM, LANE), 1)
    valid = lane < N_EXPERTS
    neg = -jnp.inf
    biased = jnp.where(valid, sc + bias_ref[...], neg)
    gmax = _group_allreduce(biased, lane, jnp.maximum)
    first = _group_allreduce(jnp.where(biased == gmax, lane, _BIG), lane, jnp.minimum)
    second = _group_allreduce(jnp.where(lane == first, neg, biased), lane, jnp.maximum)
    gs = jnp.where(valid, gmax + second, neg)
    grp = lane >> 3
    cand = jnp.full((TM, LANE), neg, F32)
    for _ in range(TOPK_GROUP):
        m = jnp.max(gs, axis=1, keepdims=True)
        g1 = jnp.min(jnp.where(gs == m, grp, _BIG), axis=1, keepdims=True)
        hit = grp == g1
        cand = jnp.where(hit, biased, cand)
        gs = jnp.where(hit, neg, gs)
    sel = jnp.zeros((TM, LANE), F32)
    e_out = jnp.zeros((TM, LANE), jnp.int32)
    w_out = jnp.zeros((TM, LANE), F32)
    idxs = []
    for k in range(TOP_K):
        m = jnp.max(cand, axis=1, keepdims=True)
        ik = jnp.min(jnp.where(cand == m, lane, _BIG), axis=1, keepdims=True)
        hit = lane == ik
        vk = jnp.sum(jnp.where(hit, sc, 0.0), axis=1, keepdims=True)
        sel = jnp.where(hit, 1.0, sel)
        cand = jnp.where(hit, neg, cand)
        e_out = jnp.where(lane == k, ik, e_out)
        w_out = jnp.where(lane == k, vk, w_out)
        idxs.append(ik)
    wsum = jnp.sum(w_out, axis=1, keepdims=True)
    w_ref[...] = w_out / wsum * ROUTE_SCALE
    e_ref[...] = e_out
    rowi = lax.broadcasted_iota(jnp.int32, (TM, 1), 0)
    sel = jnp.where(rowi < NT - i * TM, sel, 0.0)
    ri = lax.broadcasted_iota(jnp.int32, (TM, TM), 0)
    ci = lax.broadcasted_iota(jnp.int32, (TM, TM), 1)
    before = _dot((ri > ci).astype(BF16), sel.astype(BF16))
    rank = run[0:1, :] + before
    r_out = jnp.zeros((TM, LANE), F32)
    for k in range(TOP_K):
        rk = jnp.sum(jnp.where(lane == idxs[k], rank, 0.0), axis=1, keepdims=True)
        r_out = jnp.where(lane == k, rk, r_out)
    r_ref[...] = r_out.astype(jnp.int32)
    run[...] = jnp.broadcast_to(rank[TM - 1:TM, :] + sel[TM - 1:TM, :], run.shape)

    @pl.when(i == pl.num_programs(0) - 1)
    def _():
        cnt_ref[...] = run[...]


def router(scores, bias_row):
    return pl.pallas_call(
        _router_kernel,
        grid=(N_TILES,),
        in_specs=[pl.BlockSpec((TM, LANE), lambda i: (i, 0)), _full((1, LANE))],
        out_specs=[pl.BlockSpec((TM, LANE), lambda i: (i, 0)),
                   pl.BlockSpec((TM, LANE), lambda i: (i, 0)),
                   pl.BlockSpec((TM, LANE), lambda i: (i, 0)),
                   _full((8, LANE))],
        out_shape=[jax.ShapeDtypeStruct((NT, LANE), jnp.int32),
                   jax.ShapeDtypeStruct((NT, LANE), jnp.int32),
                   jax.ShapeDtypeStruct((NT, LANE), F32),
                   jax.ShapeDtypeStruct((8, LANE), F32)],
        scratch_shapes=[pltpu.VMEM((8, LANE), F32)],
        compiler_params=_cparams(("arbitrary",)),
        name="router",
    )(scores, bias_row)


def _dispatch_kernel(cnt_ref, pst_ref, dest_ref, x_ref, xg_hbm, zbuf, sem, zsem):
    i = pl.program_id(0)

    def tile_rows(row):
        return pl.ds(pl.multiple_of(row * ROW_TILE, ROW_TILE), ROW_TILE)

    def row_copy(src_row, dst_tile_row):
        dst = pl.ds(pl.multiple_of(dst_tile_row, ROW_TILE), ROW_TILE)
        return pltpu.make_async_copy(x_ref.at[tile_rows(src_row)], xg_hbm.at[dst], sem)

    def zero_copy(dst_row):
        return pltpu.make_async_copy(zbuf, xg_hbm.at[tile_rows(dst_row)], zsem)

    @pl.when(i == 0)
    def _():
        zbuf[...] = jnp.zeros_like(zbuf)

        def per_expert(e, carry):
            c = cnt_ref[e]
            npad = (MOE_T - c % MOE_T) % MOE_T
            base = pst_ref[e] + c

            def start(r, cc):
                zero_copy(base + r).start()
                return cc

            lax.fori_loop(0, npad, start, 0)

            def wait(r, cc):
                zero_copy(base + r).wait()
                return cc

            lax.fori_loop(0, npad, wait, 0)
            return carry

        lax.fori_loop(0, N_EXPERTS, per_expert, 0)

    def start(t, carry):
        for s in range(TOP_K):
            row_copy(t, dest_ref[t * TOP_K + s]).start(priority=s % 2)
        return carry

    lax.fori_loop(0, TM, start, 0, unroll=DMA_UNROLL)

    for s in range(TOP_K):
        pltpu.make_async_copy(x_ref, x_ref, sem).wait()


def dispatch(counts, pstarts, dest_flat, x1p):
    grid_spec = pltpu.PrefetchScalarGridSpec(
        num_scalar_prefetch=2,
        grid=(N_TILES,),
        in_specs=[
            pl.BlockSpec((TM * TOP_K,), lambda i, c, p: (i,), memory_space=pltpu.SMEM),
            pl.BlockSpec((TM * ROW_TILE, LANE), lambda i, c, p: (i, 0)),
        ],
        out_specs=pl.BlockSpec(memory_space=pl.ANY),
        scratch_shapes=[pltpu.VMEM((ROW_TILE, LANE), jnp.uint32),
                        pltpu.SemaphoreType.DMA(()), pltpu.SemaphoreType.DMA(())],
    )
    return pl.pallas_call(
        _dispatch_kernel,
        grid_spec=grid_spec,
        out_shape=jax.ShapeDtypeStruct(((MOE_NB + 1) * MOE_T * ROW_TILE, LANE), jnp.uint32),
        compiler_params=_cparams(("arbitrary",)),
        name="dispatch",
    )(counts, pstarts, dest_flat, x1p)


def _moe_kernel(blk_e_ref, nused_ref, x_ref, wg_ref, wu_ref, wd_ref, o_ref, wg_bf, wu_bf, wd_bf):
    i = pl.program_id(0)
    changed = jnp.logical_or(i == 0, blk_e_ref[i] != blk_e_ref[jnp.maximum(i - 1, 0)])

    @pl.when(jnp.logical_and(changed, i < nused_ref[0]))
    def _():
        wg_bf[...] = wg_ref[...].astype(BF16)
        wu_bf[...] = wu_ref[...].astype(BF16)
        wd_bf[...] = wd_ref[...].astype(BF16)

    @pl.when(i < nused_ref[0])
    def _():
        halves = [_unpack_bf16_pairs(_load_tile_cols(x_ref, MOE_T, j)) for j in range(ROW_TILE)]
        lo = jnp.concatenate([p[0] for p in halves], axis=1)
        hi = jnp.concatenate([p[1] for p in halves], axis=1)
        gate = _dot(lo, wg_bf[0:HALF, :]) + _dot(hi, wg_bf[HALF:D_MODEL, :])
        up = _dot(lo, wu_bf[0:HALF, :]) + _dot(hi, wu_bf[HALF:D_MODEL, :])
        hb = jax.nn.silu(gate) * up
        _store_tile_rows(o_ref, 0, _pack_bf16_pairs(_dot(hb.astype(BF16), wd_bf[...])))

    @pl.when(i >= nused_ref[0])
    def _():
        o_ref[...] = jnp.zeros_like(o_ref)


def moe_experts(blk_e, nused, xg, wg, wu, wd, layer):
    grid_spec = pltpu.PrefetchScalarGridSpec(
        num_scalar_prefetch=2,
        grid=(MOE_NB,),
        in_specs=[
            pl.BlockSpec((MOE_T * ROW_TILE, LANE), lambda i, be, nu: (jnp.minimum(i, nu[0] - 1), 0)),
            pl.BlockSpec((None, None, D_MODEL, D_EXPERT), lambda i, be, nu: (layer, be[i], 0, 0)),
            pl.BlockSpec((None, None, D_MODEL, D_EXPERT), lambda i, be, nu: (layer, be[i], 0, 0)),
            pl.BlockSpec((None, None, D_EXPERT, D_MODEL), lambda i, be, nu: (layer, be[i], 0, 0)),
        ],
        out_specs=pl.BlockSpec((MOE_T * ROW_TILE, LANE), lambda i, be, nu: (i, 0)),
        scratch_shapes=[pltpu.VMEM((D_MODEL, D_EXPERT), BF16),
                        pltpu.VMEM((D_MODEL, D_EXPERT), BF16),
                        pltpu.VMEM((D_EXPERT, D_MODEL), BF16)],
    )
    return pl.pallas_call(
        _moe_kernel,
        grid_spec=grid_spec,
        out_shape=jax.ShapeDtypeStruct((MOE_NB * MOE_T * ROW_TILE, LANE), jnp.uint32),
        compiler_params=_cparams(("arbitrary",)),
        name="moe_experts",
    )(blk_e, nused, xg, wg, wu, wd)


TMC = 256
NC_TILES = -(-NT // TMC)
DEST_LEN = max(N_TILES * TM, NC_TILES * TMC) * TOP_K


def _combine_kernel(dest_ref, x1_ref, w_ref, yb_hbm, wg_ref, wu_ref, wd_ref, g_ref, b_ref, o_ref,
                    ybuf, sem):
    def row_copy(t, s):
        src = pl.ds(pl.multiple_of(dest_ref[t * TOP_K + s], ROW_TILE), ROW_TILE)
        dst = pl.ds(pl.multiple_of(t * ROW_TILE, ROW_TILE), ROW_TILE)
        return pltpu.make_async_copy(yb_hbm.at[src], ybuf.at[s, dst], sem)

    def start(t, carry):
        for s in range(TOP_K):
            row_copy(t, s).start(priority=s % 2)
        return carry

    lax.fori_loop(0, TMC, start, 0, unroll=DMA_UNROLL)

    x1 = x1_ref[...]
    xb = x1.astype(BF16)
    hb = jax.nn.silu(_dot(xb, wg_ref[...])) * _dot(xb, wu_ref[...])
    shared = _dot(hb.astype(BF16), wd_ref[...])

    pltpu.make_async_copy(ybuf, ybuf, sem).wait()

    w = w_ref[...]
    acc_lo = [shared[:, LANE * j:LANE * (j + 1)] for j in range(ROW_TILE)]
    acc_hi = [shared[:, HALF + LANE * j:HALF + LANE * (j + 1)] for j in range(ROW_TILE)]
    for s in range(TOP_K):
        ws = _col(w, s)
        for j in range(ROW_TILE):
            lo, hi = _unpack_pairs_f32(_load_tile_cols(ybuf.at[s], TMC, j))
            acc_lo[j] = acc_lo[j] + ws * lo
            acc_hi[j] = acc_hi[j] + ws * hi
    acc = jnp.concatenate(acc_lo + acc_hi, axis=1)
    o_ref[...] = _ln(ALPHA * x1 + acc, g_ref[...], b_ref[...])


def combine_shared_ln2(dest_flat, x1, w, yb, wg, wu, wd, g, b):
    grid_spec = pl.GridSpec(
        grid=(NC_TILES,),
        in_specs=[
            pl.BlockSpec((TMC * TOP_K,), lambda i: (i,), memory_space=pltpu.SMEM),
            pl.BlockSpec((TMC, D_MODEL), lambda i: (i, 0)),
            pl.BlockSpec((TMC, LANE), lambda i: (i, 0)),
            pl.BlockSpec(memory_space=pl.ANY),
            _full((D_MODEL, D_EXPERT)), _full((D_MODEL, D_EXPERT)), _full((D_EXPERT, D_MODEL)),
            _full((1, D_MODEL)), _full((1, D_MODEL)),
        ],
        out_specs=pl.BlockSpec((TMC, D_MODEL), lambda i: (i, 0)),
        scratch_shapes=[pltpu.VMEM((TOP_K, TMC * ROW_TILE, LANE), jnp.uint32),
                        pltpu.SemaphoreType.DMA(())],
    )
    return pl.pallas_call(
        _combine_kernel,
        grid_spec=grid_spec,
        out_shape=jax.ShapeDtypeStruct((NT, D_MODEL), F32),
        compiler_params=_cparams(("arbitrary",)),
        name="combine_shared_ln2",
    )(dest_flat, x1, w, yb, wg, wu, wd, g, b)


def _schedule(counts, e_sel, rank_sel):
    padded = (counts + MOE_T - 1) // MOE_T * MOE_T
    pends = jnp.cumsum(padded)
    pstarts = pends - padded
    onehot = e_sel[:, :, None] == jnp.arange(N_EXPERTS, dtype=jnp.int32)
    dest = rank_sel + jnp.sum(jnp.where(onehot, pstarts, 0), axis=-1)
    dest_flat = dest.reshape(NK).astype(jnp.int32) * ROW_TILE
    trash = (MOE_NB * MOE_T + jnp.arange(DEST_LEN - NK, dtype=jnp.int32) % MOE_T) * ROW_TILE
    dest_disp = jnp.concatenate([dest_flat, trash])
    dest_comb = jnp.pad(dest_flat, (0, DEST_LEN - NK))
    blk_row0 = jnp.arange(MOE_NB, dtype=jnp.int32) * MOE_T
    blk_e = jnp.minimum(jnp.sum((pends[None, :] <= blk_row0[:, None]).astype(jnp.int32), axis=1),
                        N_EXPERTS - 1).astype(jnp.int32)
    nused = (pends[-1] // MOE_T).astype(jnp.int32).reshape(1)
    return pstarts.astype(jnp.int32), dest_disp, dest_comb, blk_e, nused


def _pad_rows(x, rows):
    b, r, c = x.shape
    return jnp.pad(x, ((0, 0), (rows - r, 0), (0, 0))).reshape(b * rows, c)


def _layer(h, st, lw):
    proj = in_proj(h, lw['w_in'])
    sinks = lw['sinks']
    zeros16 = jnp.zeros((BATCH * 16, GROUP_W), F32)
    zeros8 = jnp.zeros((BATCH * 8, C_CONV_CH), F32)

    meta_blk = lambda b, j=None: M_ROW0 // N_META + b
    mix = attn_call(sinks, proj, proj, (proj, proj), (proj, proj), nb=BATCH, length=N_META,
                    ch=N_META, nq=1, use_meta=False, chunk0=0, q_row0=M_ROW0,
                    kp_map=lambda b, j: 0, km_map=meta_blk,
                    kp_cols=(C_K // KV_W, C_V // KV_W), km_cols=(C_K // KV_W, C_V // KV_W),
                    mix=None, mix_rows=NT)
    nq_p = 4
    mix = attn_call(sinks, proj, proj, (proj, proj), (proj, proj), nb=BATCH, length=SEQ,
                    ch=64, nq=nq_p, use_meta=True, chunk0=0, q_row0=0,
                    kp_map=lambda b, j: jnp.maximum(b * (SEQ // WINDOW) + j * (64 * nq_p // WINDOW) - 1, 0),
                    km_map=meta_blk,
                    kp_cols=(C_K // KV_W, C_V // KV_W), km_cols=(C_K // KV_W, C_V // KV_W),
                    mix=mix, mix_rows=NT)
    mix = attn_call(sinks, proj, proj, (st['win_k'], st['win_v']), (st['meta_k'], st['meta_v']),
                    nb=DEC_BATCH, length=DEC_SEQ, ch=64, nq=1, use_meta=True, chunk0=2,
                    q_row0=S_ROW0, kp_map=lambda b, j: b, km_map=lambda b, j: b,
                    kp_cols=(0, 0), km_cols=(0, 0), mix=mix, mix_rows=NT)

    mix = pool_call(proj, zeros16, lambda b: b, 0, lw['pool_w'], lw['pool_scale'],
                    nb=BATCH, length=N_META, tb=N_META, row0=M_ROW0, ramp=True, mix=mix, mix_rows=NT)
    mix = pool_call(proj, proj, lambda b: M_ROW0 // 16 + b, C_U // GROUP_W, lw['pool_w'],
                    lw['pool_scale'], nb=BATCH, length=SEQ, tb=512, row0=0, ramp=False,
                    mix=mix, mix_rows=NT)
    mix = pool_call(proj, st['pool'], lambda b: b, 0, lw['pool_w'], lw['pool_scale'],
                    nb=DEC_BATCH, length=DEC_SEQ, tb=DEC_SEQ, row0=S_ROW0, ramp=False,
                    mix=mix, mix_rows=NT)

    ssd_w = lw['ssd']
    h0z = jnp.zeros((BATCH, 512, C_STATE), F32)
    mix, hc_m = ssd_call(proj, zeros8, lambda b: b, (0, 2, 3), h0z, ssd_w,
                         nb=BATCH, length=N_META, q=N_META, row0=M_ROW0, mix=mix, mix_rows=NT)
    meta_tail = lambda b: (M_ROW0 + 8) // 8 + 2 * b
    mix, hc_p = ssd_call(proj, proj, meta_tail, (C_XS // 512, C_B // 256, C_C // 256), hc_m, ssd_w,
                         nb=BATCH, length=SEQ, q=256, row0=0, mix=mix, mix_rows=NT)
    mix, hc_s = ssd_call(proj, st['ssm_conv'], lambda b: b, (0, 2, 3), st['ssm'], ssd_w,
                         nb=DEC_BATCH, length=DEC_SEQ, q=DEC_SEQ, row0=S_ROW0, mix=mix, mix_rows=NT)

    lru_w = lw['lru']
    l0z = jnp.zeros((BATCH, 1, GROUP_W), F32)
    mix, hd_m = lru_call(proj, zeros8, lambda b: b, 0, l0z, lru_w,
                         nb=BATCH, length=N_META, tb=N_META, row0=M_ROW0, mix=mix, mix_rows=NT)
    mix, hd_p = lru_call(proj, proj, meta_tail, C_RX // GROUP_W, hd_m, lru_w,
                         nb=BATCH, length=SEQ, tb=256, row0=0, mix=mix, mix_rows=NT)
    mix, hd_s = lru_call(proj, st['lru_conv'], lambda b: b, 0, st['lru'], lru_w,
                         nb=DEC_BATCH, length=DEC_SEQ, tb=DEC_SEQ, row0=S_ROW0, mix=mix, mix_rows=NT)

    x1, x1p, sc = out_proj(mix, h, lw['w_out'], lw['ln1_g'], lw['ln1_b'], lw['rw'])

    e_sel, rank_sel, w_sel, cnt = router(sc, lw['router_bias'])
    counts = cnt[0, :N_EXPERTS].astype(jnp.int32)
    pstarts, dest_disp, dest_comb, blk_e, nused = _schedule(counts, e_sel[:, :TOP_K], rank_sel[:, :TOP_K])
    xg = dispatch(counts, pstarts, dest_disp, x1p)
    yb = moe_experts(blk_e, nused, xg, lw['wg'], lw['wu'], lw['wd'], lw['layer'])
    h_new = combine_shared_ln2(dest_comb, x1, w_sel, yb, lw['sh_wg'], lw['sh_wu'], lw['sh_wd'],
                               lw['ln2_g'], lw['ln2_b'])

    def tail(row0, nb, length, nrows, c0, width):
        return jnp.stack([proj[row0 + (b + 1) * length - nrows:row0 + (b + 1) * length, c0:c0 + width]
                          for b in range(nb)])

    kv4 = lambda x: x.reshape(x.shape[0], x.shape[1], A_KV_HEADS, HEAD_DIM)
    p_state = (
        kv4(tail(M_ROW0, BATCH, N_META, N_META, C_K, KV_W)),
        kv4(tail(M_ROW0, BATCH, N_META, N_META, C_V, KV_W)),
        kv4(tail(0, BATCH, SEQ, WINDOW, C_K, KV_W)), kv4(tail(0, BATCH, SEQ, WINDOW, C_V, KV_W)),
        tail(0, BATCH, SEQ, POOL_STATE, C_U, GROUP_W),
        tail(0, BATCH, SEQ, CONV_W - 1, C_XS, C_CONV_CH),
        hc_p.reshape(BATCH, C_HEADS, 64, C_STATE),
        tail(0, BATCH, SEQ, CONV_W - 1, C_RX, GROUP_W),
        hd_p.reshape(BATCH, GROUP_W),
    )
    s_state = (
        kv4(tail(S_ROW0, DEC_BATCH, DEC_SEQ, DEC_SEQ, C_K, KV_W)),
        kv4(tail(S_ROW0, DEC_BATCH, DEC_SEQ, DEC_SEQ, C_V, KV_W)),
        tail(S_ROW0, DEC_BATCH, DEC_SEQ, POOL_STATE, C_U, GROUP_W),
        tail(S_ROW0, DEC_BATCH, DEC_SEQ, CONV_W - 1, C_XS, C_CONV_CH),
        hc_s.reshape(DEC_BATCH, C_HEADS, 64, C_STATE),
        tail(S_ROW0, DEC_BATCH, DEC_SEQ, CONV_W - 1, C_RX, GROUP_W),
        hd_s.reshape(DEC_BATCH, GROUP_W),
    )
    return h_new, p_state, s_state


def _block_diag(w):
    z = jnp.zeros((D_BLOCK_W, D_BLOCK_W), w.dtype)
    return jnp.stack([jnp.block([[w[2 * s], z], [z, w[2 * s + 1]]]) for s in range(4)])


def _pad_lanes(v, width=LANE):
    return jnp.pad(v, (0, width - v.shape[0])).reshape(1, width)


def kernel(x_prompt, x_sample, cache_attn_meta_k, cache_attn_meta_v, cache_attn_k, cache_attn_v, state_pool, state_ssm_conv, state_ssm, state_lru_conv, state_lru, meta_tokens, ln_in_g, ln_in_b, w_in, w_out, attn_sinks, pool_w, pool_scale, ssm_conv_w, ssm_conv_b, ssm_dt_bias, ssm_a_log, ssm_d, ssm_norm_g, lru_conv_w, lru_conv_b, lru_wr, lru_br, lru_wi, lru_bi, lru_lambda, ln1_g, ln1_b, ln2_g, ln2_b, router_w, router_bias, exp_w_gate, exp_w_up, exp_w_down, sh_w_gate, sh_w_up, sh_w_down):
    row = lambda v: v.reshape(1, -1).astype(F32)
    h = ln_in(x_prompt.reshape(P_ROWS, D_MODEL), x_sample.reshape(S_ROWS, D_MODEL),
              meta_tokens.astype(F32), row(ln_in_g), row(ln_in_b))
    p_states, s_states = [], []
    for i in range(DEPTH):
        wi = w_in[i]
        s0 = 0
        parts = {}
        for name, size in zip(('q', 'k', 'v', 'u', 'z', 'xbc', 'dt', 'rx', 'rg'),
                              (512, 128, 128, 512, 512, 1024, 8, 512, 512)):
            parts[name] = wi[:, s0:s0 + size]
            s0 += size
        w_in_p = jnp.concatenate(
            [parts[n] for n in ('q', 'u', 'z', 'rx', 'rg', 'xbc', 'k', 'v', 'dt')]
            + [jnp.zeros((D_MODEL, PROJ_P - C_DT - C_HEADS), F32)], axis=1).astype(BF16)
        rw = jnp.pad(router_w[i].astype(F32), ((0, 0), (0, LANE - N_EXPERTS))).astype(BF16)
        cw = ssm_conv_w[i].astype(F32)
        cb = ssm_conv_b[i].astype(F32)
        lw = dict(
            w_in=w_in_p, w_out=w_out[i].astype(BF16), sinks=attn_sinks[i].astype(F32),
            pool_w=pool_w[i].astype(BF16), pool_scale=row(pool_scale[i]),
            ssd=dict(cw_x=cw[:, :512], cw_b=cw[:, 512:768], cw_c=cw[:, 768:],
                     cb_x=row(cb[:512]), cb_b=row(cb[512:768]), cb_c=row(cb[768:]),
                     dt_bias=_pad_lanes(ssm_dt_bias[i].astype(F32)),
                     a_log=_pad_lanes(ssm_a_log[i].astype(F32)),
                     d_skip=row(jnp.repeat(ssm_d[i].astype(F32), 64)),
                     norm_g=row(ssm_norm_g[i])),
            lru=dict(cw=lru_conv_w[i].astype(F32), cb=row(lru_conv_b[i]),
                     wr=_block_diag(lru_wr[i]).astype(BF16), wi=_block_diag(lru_wi[i]).astype(BF16),
                     br=row(lru_br[i]), bi=row(lru_bi[i]), lam=row(lru_lambda[i])),
            ln1_g=row(ln1_g[i]), ln1_b=row(ln1_b[i]), ln2_g=row(ln2_g[i]), ln2_b=row(ln2_b[i]),
            rw=rw, router_bias=_pad_lanes(router_bias[i].astype(F32)),
            wg=exp_w_gate, wu=exp_w_up, wd=exp_w_down, layer=i,
            sh_wg=sh_w_gate[i].astype(BF16), sh_wu=sh_w_up[i].astype(BF16),
            sh_wd=sh_w_down[i].astype(BF16),
        )
        st = dict(
            meta_k=cache_attn_meta_k[i].reshape(DEC_BATCH * N_META, KV_W),
            meta_v=cache_attn_meta_v[i].reshape(DEC_BATCH * N_META, KV_W),
            win_k=cache_attn_k[i].reshape(DEC_BATCH * WINDOW, KV_W),
            win_v=cache_attn_v[i].reshape(DEC_BATCH * WINDOW, KV_W),
            pool=_pad_rows(state_pool[i], 16),
            ssm_conv=_pad_rows(state_ssm_conv[i], 8),
            ssm=state_ssm[i].reshape(DEC_BATCH, 512, C_STATE),
            lru_conv=_pad_rows(state_lru_conv[i], 8),
            lru=state_lru[i].reshape(DEC_BATCH, 1, GROUP_W),
        )
        h, ps, ss = _layer(h, st, lw)
        p_states.append(ps)
        s_states.append(ss)
    stk = lambda sts, j: jnp.stack([s[j] for s in sts])
    y_prompt = h[:P_ROWS].reshape(BATCH, SEQ, D_MODEL)
    y_sample = h[S_ROW0:S_ROW0 + S_ROWS].reshape(DEC_BATCH, DEC_SEQ, D_MODEL)
    return ((y_prompt, y_sample)
            + tuple(stk(p_states, j) for j in range(9))
            + tuple(stk(s_states, j) for j in range(7)))
```

```python
import functools
import math

import jax
import jax.numpy as jnp
from jax import lax
from jax.experimental import pallas as pl
from jax.experimental.pallas import tpu as pltpu

F32 = jnp.float32
BF16 = jnp.bfloat16

D_MODEL = 2048
BATCH = 4
SEQ = 4096
DEPTH = 4
DEC_BATCH = 16
DEC_SEQ = 64
N_META = 16
GROUP_W = 512
HEAD_DIM = 64
A_HEADS = 8
A_KV_HEADS = 2
A_GROUP = 4
KV_W = 128
WINDOW = 128
POOL_SIZES = (2, 4, 8, 16)
POOL_GW = 128
POOL_STATE = 15
C_HEADS = 8
C_STATE = 128
C_CONV_CH = 1024
CONV_W = 4
D_BLOCKS = 8
D_BLOCK_W = 64
LRU_C = 8.0
N_EXPERTS = 64
TOP_K = 8
N_GROUP = 8
TOPK_GROUP = 4
D_EXPERT = 512
ROUTE_SCALE = 2.5
ALPHA = (2 * DEPTH) ** 0.25
LN_EPS = 1e-5

LANE = 128
SUBLANE = 8
VMEM_LIMIT = 56 * 1024 * 1024

P_ROWS = BATCH * SEQ
S_ROWS = DEC_BATCH * DEC_SEQ
M_ROWS = BATCH * N_META
S_ROW0 = P_ROWS
M_ROW0 = P_ROWS + S_ROWS
NT = P_ROWS + S_ROWS + M_ROWS
TM = 512
N_TILES = -(-NT // TM)

C_Q, C_U, C_Z, C_RX, C_RG, C_XS, C_B, C_C, C_K, C_V, C_DT = (
    0, 512, 1024, 1536, 2048, 2560, 3072, 3328, 3584, 3712, 3840)
PROJ_P = 4096
PROJ_TN = 1024

MOE_T = 512
NK = NT * TOP_K
MOE_NB = (NK + N_EXPERTS * (MOE_T - 1) + MOE_T - 1) // MOE_T


def _cparams(sem):
    return pltpu.CompilerParams(dimension_semantics=sem, vmem_limit_bytes=VMEM_LIMIT)


def _ln(x, g, b):
    mu = jnp.mean(x, axis=-1, keepdims=True)
    xc = x - mu
    var = jnp.mean(xc * xc, axis=-1, keepdims=True)
    return xc * lax.rsqrt(var + LN_EPS) * g + b


def _dot(a, b):
    return jnp.dot(a, b, preferred_element_type=F32)


def _dot_nt(a, b, precision=None):
    return lax.dot_general(a, b, (((1,), (1,)), ((), ())), precision=precision,
                           preferred_element_type=F32)


def _full(shape):
    nd = len(shape)
    return pl.BlockSpec(shape, lambda *_: (0,) * nd)


MIX_W = 4 * GROUP_W


def _skip_ref(kern, idx, *refs):
    return kern(*refs[:idx], *refs[idx + 1:])


def _mixer_call(kern, grid, in_specs, args, y_spec, more_out_specs, more_out_shapes, scratch,
                mix, mix_rows, name):
    in_specs = list(in_specs)
    args = tuple(args)
    aliases = {}
    if mix is not None:
        n_in = len(in_specs)
        kern = functools.partial(_skip_ref, kern, n_in)
        in_specs.append(pl.BlockSpec(memory_space=pl.ANY))
        args = args + (mix,)
        aliases = {n_in: 0}
    return pl.pallas_call(
        kern,
        grid=grid,
        in_specs=in_specs,
        out_specs=[y_spec] + list(more_out_specs),
        out_shape=[jax.ShapeDtypeStruct((mix_rows, MIX_W), F32)] + list(more_out_shapes),
        scratch_shapes=scratch,
        input_output_aliases=aliases,
        compiler_params=_cparams(("arbitrary",) * len(grid)),
        name=name,
    )(*args)


def _ln_in_kernel(xp_ref, xs_ref, meta_ref, g_ref, b_ref, o_ref):
    i = pl.program_id(0)
    g = g_ref[...]
    b = b_ref[...]
    n_p = P_ROWS // TM
    n_s = S_ROWS // TM

    @pl.when(i < n_p)
    def _():
        o_ref[...] = _ln(xp_ref[...], g, b)

    @pl.when((i >= n_p) & (i < n_p + n_s))
    def _():
        o_ref[...] = _ln(xs_ref[...], g, b)

    @pl.when(i == n_p + n_s)
    def _():
        m = _ln(meta_ref[...], g, b)
        for r in range(BATCH):
            o_ref[N_META * r:N_META * (r + 1), :] = m


def ln_in(xp2, xs2, meta, g, b):
    n_p = P_ROWS // TM
    n_s = S_ROWS // TM
    return pl.pallas_call(
        _ln_in_kernel,
        grid=(N_TILES,),
        in_specs=[
            pl.BlockSpec((TM, D_MODEL), lambda i: (jnp.minimum(i, n_p - 1), 0)),
            pl.BlockSpec((TM, D_MODEL), lambda i: (jnp.clip(i - n_p, 0, n_s - 1), 0)),
            _full((N_META, D_MODEL)),
            _full((1, D_MODEL)),
            _full((1, D_MODEL)),
        ],
        out_specs=pl.BlockSpec((TM, D_MODEL), lambda i: (i, 0)),
        out_shape=jax.ShapeDtypeStruct((NT, D_MODEL), F32),
        compiler_params=_cparams(("arbitrary",)),
        name="ln_in",
    )(xp2, xs2, meta, g, b)


def _in_proj_kernel(x_ref, w_ref, o_ref):
    o_ref[...] = _dot(x_ref[...].astype(BF16), w_ref[...])


PROJ_TM = 1024


def in_proj(h, w_bf):
    return pl.pallas_call(
        _in_proj_kernel,
        grid=(-(-NT // PROJ_TM), PROJ_P // PROJ_TN),
        in_specs=[
            pl.BlockSpec((PROJ_TM, D_MODEL), lambda i, n: (i, 0)),
            pl.BlockSpec((D_MODEL, PROJ_TN), lambda i, n: (0, n)),
        ],
        out_specs=pl.BlockSpec((PROJ_TM, PROJ_TN), lambda i, n: (i, n)),
        out_shape=jax.ShapeDtypeStruct((NT, PROJ_P), F32),
        compiler_params=_cparams(("arbitrary", "arbitrary")),
        name="in_proj",
    )(h, w_bf)


ATT_PAD = 64
ATT_WIN = ATT_PAD + WINDOW + 64


def _attn_kernel(sink_ref, q_ref, kc_ref, vc_ref, kp_ref, vp_ref, km_ref, vm_ref, o_ref,
                 kbuf, vbuf, *, ch, nq, use_meta, chunk0):
    j = pl.program_id(1)
    tq = ch * nq
    zpad = jnp.zeros((ATT_PAD - N_META, KV_W), BF16)
    kbuf[0:N_META, :] = km_ref[...].astype(BF16)
    kbuf[N_META:ATT_PAD, :] = zpad
    vbuf[0:N_META, :] = vm_ref[...].astype(BF16)
    vbuf[N_META:ATT_PAD, :] = zpad
    kbuf[ATT_PAD:ATT_PAD + WINDOW, :] = kp_ref[...].astype(BF16)
    vbuf[ATT_PAD:ATT_PAD + WINDOW, :] = vp_ref[...].astype(BF16)
    kbuf[ATT_PAD + WINDOW:ATT_PAD + WINDOW + tq, :] = kc_ref[...].astype(BF16)
    vbuf[ATT_PAD + WINDOW:ATT_PAD + WINDOW + tq, :] = vc_ref[...].astype(BF16)
    if ch < 64:
        zc = jnp.zeros((64 - ch, KV_W), BF16)
        kbuf[ATT_PAD + WINDOW + tq:ATT_PAD + WINDOW + tq + 64 - ch, :] = zc
        vbuf[ATT_PAD + WINDOW + tq:ATT_PAD + WINDOW + tq + 64 - ch, :] = zc

    rows = A_GROUP * ch
    col = lax.broadcasted_iota(jnp.int32, (rows, ATT_WIN), 1)
    row = lax.broadcasted_iota(jnp.int32, (rows, 1), 0)
    lane = lax.broadcasted_iota(jnp.int32, (ch, LANE), 1)
    lo = lane < HEAD_DIM

    for i in range(nq):
        c = chunk0 + j * nq + i
        first_band = ATT_PAD + 64 * jnp.maximum(2 - c, 0)
        valid = (col >= first_band) & (col < ATT_PAD + WINDOW + ch)
        if use_meta:
            valid = valid | (col < N_META)
        kcat = jnp.concatenate(
            [kbuf[0:ATT_PAD, :], kbuf[ATT_PAD + ch * i:ATT_PAD + ch * i + WINDOW + 64, :]], axis=0)
        vcat = jnp.concatenate(
            [vbuf[0:ATT_PAD, :], vbuf[ATT_PAD + ch * i:ATT_PAD + ch * i + WINDOW + 64, :]], axis=0)
        qi = q_ref[ch * i:ch * (i + 1), :] * (HEAD_DIM ** -0.5)
        tiles = [qi[:, LANE * t:LANE * (t + 1)] for t in range(A_HEADS // 2)]
        out_tiles = [None] * (A_HEADS // 2)
        for kh in range(A_KV_HEADS):
            qs = []
            for r in range(A_GROUP):
                h = A_GROUP * kh + r
                t = tiles[h // 2]
                if h % 2 != kh:
                    t = pltpu.roll(t, HEAD_DIM, axis=1)
                keep = lo if kh == 0 else jnp.logical_not(lo)
                qs.append(jnp.where(keep, t, 0.0).astype(BF16))
            qz = jnp.concatenate(qs, axis=0)
            s = _dot_nt(qz, kcat)
            s = jnp.where(valid, s, -jnp.inf)
            sink = jnp.zeros((rows, 1), F32)
            for r in range(A_GROUP):
                sink = jnp.where((row >= r * ch) & (row < (r + 1) * ch),
                                 sink_ref[A_GROUP * kh + r], sink)
            m = jnp.maximum(jnp.max(s, axis=-1, keepdims=True), sink)
            p = jnp.exp(s - m)
            den = jnp.sum(p, axis=-1, keepdims=True) + jnp.exp(sink - m)
            probs = (p / den).astype(BF16)
            o = _dot(probs, vcat)
            for r in range(A_GROUP):
                h = A_GROUP * kh + r
                oh = o[r * ch:(r + 1) * ch, :]
                if h % 2 != kh:
                    oh = pltpu.roll(oh, HEAD_DIM, axis=1)
                keep = lo if h % 2 == 0 else jnp.logical_not(lo)
                prev = out_tiles[h // 2]
                out_tiles[h // 2] = jnp.where(keep, oh, 0.0 if prev is None else prev)
        o_ref[ch * i:ch * (i + 1), :] = jnp.concatenate(out_tiles, axis=1)


def attn_call(sinks, q_src, kc_src, kp_src, km_src, *, nb, length, ch, nq, use_meta, chunk0,
              q_row0, kp_map, km_map, kp_cols, km_cols, mix, mix_rows):
    tq = ch * nq
    nj = length // tq
    qb0 = q_row0 // tq
    kern = functools.partial(_attn_kernel, ch=ch, nq=nq, use_meta=use_meta, chunk0=chunk0)
    in_specs = [
        pl.BlockSpec(memory_space=pltpu.SMEM),
        pl.BlockSpec((tq, GROUP_W), lambda b, j: (qb0 + b * nj + j, C_Q // GROUP_W)),
        pl.BlockSpec((tq, KV_W), lambda b, j: (qb0 + b * nj + j, C_K // KV_W)),
        pl.BlockSpec((tq, KV_W), lambda b, j: (qb0 + b * nj + j, C_V // KV_W)),
        pl.BlockSpec((WINDOW, KV_W), lambda b, j: (kp_map(b, j), kp_cols[0])),
        pl.BlockSpec((WINDOW, KV_W), lambda b, j: (kp_map(b, j), kp_cols[1])),
        pl.BlockSpec((N_META, KV_W), lambda b, j: (km_map(b, j), km_cols[0])),
        pl.BlockSpec((N_META, KV_W), lambda b, j: (km_map(b, j), km_cols[1])),
    ]
    return _mixer_call(
        kern, (nb, nj), in_specs,
        (sinks, q_src, kc_src, kc_src, kp_src[0], kp_src[1], km_src[0], km_src[1]),
        pl.BlockSpec((tq, GROUP_W), lambda b, j: (qb0 + b * nj + j, 0)), [], [],
        [pltpu.VMEM((ATT_PAD + WINDOW + tq + 64, KV_W), BF16),
         pltpu.VMEM((ATT_PAD + WINDOW + tq + 64, KV_W), BF16)],
        mix, mix_rows, "attn")[0]


def _pool_kernel(u_ref, prev_ref, w_ref, scale_ref, o_ref, buf, *, tb, ramp):
    j = pl.program_id(1)

    @pl.when(j == 0)
    def _():
        buf[0:16, :] = prev_ref[...]

    buf[16:16 + tb, :] = u_ref[...]
    pos = j * tb + lax.broadcasted_iota(jnp.int32, (tb, 1), 0)
    outs = []
    for g, win in enumerate(POOL_SIZES):
        sl = slice(g * POOL_GW, (g + 1) * POOL_GW)
        tot = buf[16:16 + tb, sl]
        for k in range(1, win):
            tot = tot + buf[16 - k:16 - k + tb, sl]
        if ramp:
            cnt = jnp.minimum(win, pos + 1).astype(F32)
            mean = tot / cnt
        else:
            mean = tot * (1.0 / win)
        d = mean - buf[16:16 + tb, sl]
        outs.append(_dot(d.astype(BF16), w_ref[g]))
    o_ref[...] = jnp.concatenate(outs, axis=1) * scale_ref[...]
    buf[0:16, :] = buf[tb:tb + 16, :]


def pool_call(proj, prev_src, prev_map, prev_col, w_bf, scale, *, nb, length, tb, row0, ramp,
              mix, mix_rows):
    nj = length // tb
    rb0 = row0 // tb
    kern = functools.partial(_pool_kernel, tb=tb, ramp=ramp)
    in_specs = [
        pl.BlockSpec((tb, GROUP_W), lambda b, j: (rb0 + b * nj + j, C_U // GROUP_W)),
        pl.BlockSpec((16, GROUP_W), lambda b, j: (prev_map(b), prev_col)),
        _full((4, POOL_GW, POOL_GW)),
        _full((1, GROUP_W)),
    ]
    return _mixer_call(
        kern, (nb, nj), in_specs, (proj, prev_src, w_bf, scale),
        pl.BlockSpec((tb, GROUP_W), lambda b, j: (rb0 + b * nj + j, 1)), [], [],
        [pltpu.VMEM((tb + 16, GROUP_W), F32)], mix, mix_rows, "pool")[0]


def _conv_block(buf, x_ref, w_ref, b_ref, tb):
    buf[8:8 + tb, :] = x_ref[...]
    acc = b_ref[...] + buf[5:5 + tb, :] * w_ref[0:1, :]
    for k in range(1, CONV_W):
        acc = acc + buf[5 + k:5 + k + tb, :] * w_ref[k:k + 1, :]
    return acc


def _conv_carry(buf, tb):
    buf[0:8, :] = buf[tb:tb + 8, :]


def _col(x, h):
    lane = lax.broadcasted_iota(jnp.int32, x.shape, 1)
    return jnp.sum(jnp.where(lane == h, x, 0.0), axis=1, keepdims=True)


def _ssd_kernel(xs_ref, bm_ref, cm_ref, dt_ref, z_ref, px_ref, pb_ref, pc_ref, h0_ref,
                wx_ref, wb_ref, wc_ref, bx_ref, bb_ref, bc_ref,
                dtb_ref, alog_ref, dskip_ref, ng_ref,
                y_ref, hout_ref, bufx, bufb, bufc, hst, *, q):
    j = pl.program_id(1)
    nj = pl.num_programs(1)
    hi = lax.Precision.HIGHEST

    @pl.when(j == 0)
    def _():
        bufx[0:8, :] = px_ref[...]
        bufb[0:8, :] = pb_ref[...]
        bufc[0:8, :] = pc_ref[...]
        hst[...] = h0_ref[...]

    xs = jax.nn.silu(_conv_block(bufx, xs_ref, wx_ref, bx_ref, q))
    bm = jax.nn.silu(_conv_block(bufb, bm_ref, wb_ref, bb_ref, q))
    cm = jax.nn.silu(_conv_block(bufc, cm_ref, wc_ref, bc_ref, q))
    _conv_carry(bufx, q)
    _conv_carry(bufb, q)
    _conv_carry(bufc, q)

    lane1 = lax.broadcasted_iota(jnp.int32, (1, LANE), 1)
    hmask = lane1 < C_HEADS
    dt = jnp.where(hmask, jax.nn.softplus(dt_ref[...] + dtb_ref[...]), 0.0)
    a = jnp.where(hmask, -jnp.exp(alog_ref[...]), 0.0)
    dta = dt * a
    ri = lax.broadcasted_iota(jnp.int32, (q, q), 0)
    ci = lax.broadcasted_iota(jnp.int32, (q, q), 1)
    tri = ri >= ci
    cum = jnp.dot(tri.astype(F32), dta, precision=hi, preferred_element_type=F32)
    eye = (lax.broadcasted_iota(jnp.int32, (LANE, LANE), 0)
           == lax.broadcasted_iota(jnp.int32, (LANE, LANE), 1)).astype(F32)
    cum_t = _dot_nt(eye, cum, precision=hi)
    dt_t = _dot_nt(eye, dt, precision=hi)
    ecum = jnp.exp(cum)
    cum_last = cum[q - 1:q, :]
    te = jnp.exp(cum_last - cum) * dt

    lane = lax.broadcasted_iota(jnp.int32, (q, LANE), 1)
    lo = lane < 64
    bm_bf = bm.astype(BF16)
    cm_bf = cm.astype(BF16)
    cb = [_dot_nt(cm_bf[:, LANE * g:LANE * (g + 1)], bm_bf[:, LANE * g:LANE * (g + 1)])
          for g in range(2)]
    yoff = [_dot_nt(cm_bf[:, LANE * g:LANE * (g + 1)], hst[256 * g:256 * (g + 1), :].astype(BF16))
            for g in range(2)]

    y_tiles = []
    xw_tiles = []
    for k in range(C_HEADS // 2):
        g = k // 2
        x_pair = xs[:, LANE * k:LANE * (k + 1)]
        ydiag = None
        for par in range(2):
            h = 2 * k + par
            seg = _col(cum, h) - cum_t[h:h + 1, :]
            lm = jnp.exp(jnp.where(tri, seg, -jnp.inf))
            mm = (cb[g] * lm * dt_t[h:h + 1, :]).astype(BF16)
            xm = jnp.where(lo if par == 0 else jnp.logical_not(lo), x_pair, 0.0).astype(BF16)
            part = _dot(mm, xm)
            ydiag = part if ydiag is None else ydiag + part
        e_pair = jnp.where(lo, _col(ecum, 2 * k), _col(ecum, 2 * k + 1))
        te_pair = jnp.where(lo, _col(te, 2 * k), _col(te, 2 * k + 1))
        kk = k % 2
        y_tiles.append(ydiag + yoff[g][:, LANE * kk:LANE * (kk + 1)] * e_pair
                       + dskip_ref[:, LANE * k:LANE * (k + 1)] * x_pair)
        xw_tiles.append((x_pair * te_pair).astype(BF16))

    eye2 = (lax.broadcasted_iota(jnp.int32, (256, 256), 0)
            == lax.broadcasted_iota(jnp.int32, (256, 256), 1)).astype(BF16)
    for g in range(2):
        xw = jnp.concatenate(xw_tiles[2 * g:2 * g + 2], axis=1)
        xw_t = _dot_nt(eye2, xw).astype(BF16)
        s_new = _dot(xw_t, bm_bf[:, LANE * g:LANE * (g + 1)])
        dec = jnp.concatenate(
            [jnp.broadcast_to(jnp.exp(cum_t[4 * g + r:4 * g + r + 1, q - 1:q]), (64, LANE))
             for r in range(4)], axis=0)
        hst[256 * g:256 * (g + 1), :] = dec * hst[256 * g:256 * (g + 1), :] + s_new

    y = jnp.concatenate(y_tiles, axis=1) * jax.nn.silu(z_ref[...])
    y = y * lax.rsqrt(jnp.mean(y * y, axis=-1, keepdims=True) + 1e-6) * ng_ref[...]
    y_ref[...] = y

    @pl.when(j == nj - 1)
    def _():
        hout_ref[...] = hst[...]


def ssd_call(proj, prev_src, prev_map, prev_cols, h0, lw, *, nb, length, q, row0, mix, mix_rows):
    nj = length // q
    rb0 = row0 // q
    kern = functools.partial(_ssd_kernel, q=q)
    blk = lambda width, col: pl.BlockSpec((q, width), lambda b, j: (rb0 + b * nj + j, col // width))
    pblk = lambda width, col: pl.BlockSpec((8, width), lambda b, j: (prev_map(b), col))
    in_specs = [
        blk(512, C_XS), blk(256, C_B), blk(256, C_C), blk(LANE, C_DT), blk(512, C_Z),
        pblk(512, prev_cols[0]), pblk(256, prev_cols[1]), pblk(256, prev_cols[2]),
        pl.BlockSpec((None, 512, C_STATE), lambda b, j: (b, 0, 0)),
        _full((CONV_W, 512)), _full((CONV_W, 256)), _full((CONV_W, 256)),
        _full((1, 512)), _full((1, 256)), _full((1, 256)),
        _full((1, LANE)), _full((1, LANE)), _full((1, 512)), _full((1, 512)),
    ]
    args = (proj, proj, proj, proj, proj, prev_src, prev_src, prev_src, h0,
            lw['cw_x'], lw['cw_b'], lw['cw_c'], lw['cb_x'], lw['cb_b'], lw['cb_c'],
            lw['dt_bias'], lw['a_log'], lw['d_skip'], lw['norm_g'])
    return _mixer_call(
        kern, (nb, nj), in_specs, args,
        pl.BlockSpec((q, GROUP_W), lambda b, j: (rb0 + b * nj + j, 2)),
        [pl.BlockSpec((None, 512, C_STATE), lambda b, j: (b, 0, 0))],
        [jax.ShapeDtypeStruct((nb, 512, C_STATE), F32)],
        [pltpu.VMEM((q + 8, 512), F32), pltpu.VMEM((q + 8, 256), F32),
         pltpu.VMEM((q + 8, 256), F32), pltpu.VMEM((512, C_STATE), F32)],
        mix, mix_rows, "ssd")


def _lru_kernel(rx_ref, rg_ref, prev_ref, h0_ref, cw_ref, cb_ref, wr_ref, wi_ref,
                br_ref, bi_ref, lam_ref, y_ref, hout_ref, buf, hc, *, tb):
    j = pl.program_id(1)
    nj = pl.num_programs(1)

    @pl.when(j == 0)
    def _():
        buf[0:8, :] = prev_ref[...]
        hc[...] = jnp.broadcast_to(h0_ref[...], hc.shape)

    xc = _conv_block(buf, rx_ref, cw_ref, cb_ref, tb)
    _conv_carry(buf, tb)
    rs, gs = [], []
    for s in range(GROUP_W // LANE):
        xb = xc[:, LANE * s:LANE * (s + 1)].astype(BF16)
        rs.append(_dot(xb, wr_ref[s]))
        gs.append(_dot(xb, wi_ref[s]))
    r = jax.nn.sigmoid(jnp.concatenate(rs, axis=1) + br_ref[...])
    gi = jax.nn.sigmoid(jnp.concatenate(gs, axis=1) + bi_ref[...])
    log_a = -LRU_C * r * jax.nn.softplus(-lam_ref[...])
    a = jnp.exp(log_a)
    u = jnp.sqrt(jnp.maximum(1.0 - jnp.exp(2.0 * log_a), 0.0)) * (gi * xc)
    t = lax.broadcasted_iota(jnp.int32, (tb, 1), 0)
    d = 1
    while d < tb:
        a_sh = jnp.where(t >= d, pltpu.roll(a, d, axis=0), 1.0)
        u_sh = jnp.where(t >= d, pltpu.roll(u, d, axis=0), 0.0)
        u = a * u_sh + u
        a = a * a_sh
        d *= 2
    h = u + a * hc[0:1, :]
    y_ref[...] = h * jax.nn.gelu(rg_ref[...])
    hc[...] = jnp.broadcast_to(h[tb - 1:tb, :], hc.shape)

    @pl.when(j == nj - 1)
    def _():
        hout_ref[...] = h[tb - 1:tb, :]


def lru_call(proj, prev_src, prev_map, prev_col, h0, lw, *, nb, length, tb, row0, mix, mix_rows):
    nj = length // tb
    rb0 = row0 // tb
    kern = functools.partial(_lru_kernel, tb=tb)
    in_specs = [
        pl.BlockSpec((tb, GROUP_W), lambda b, j: (rb0 + b * nj + j, C_RX // GROUP_W)),
        pl.BlockSpec((tb, GROUP_W), lambda b, j: (rb0 + b * nj + j, C_RG // GROUP_W)),
        pl.BlockSpec((8, GROUP_W), lambda b, j: (prev_map(b), prev_col)),
        pl.BlockSpec((None, 1, GROUP_W), lambda b, j: (b, 0, 0)),
        _full((CONV_W, GROUP_W)), _full((1, GROUP_W)),
        _full((4, LANE, LANE)), _full((4, LANE, LANE)),
        _full((1, GROUP_W)), _full((1, GROUP_W)), _full((1, GROUP_W)),
    ]
    args = (proj, proj, prev_src, h0, lw['cw'], lw['cb'], lw['wr'], lw['wi'],
            lw['br'], lw['bi'], lw['lam'])
    return _mixer_call(
        kern, (nb, nj), in_specs, args,
        pl.BlockSpec((tb, GROUP_W), lambda b, j: (rb0 + b * nj + j, 3)),
        [pl.BlockSpec((None, 1, GROUP_W), lambda b, j: (b, 0, 0))],
        [jax.ShapeDtypeStruct((nb, 1, GROUP_W), F32)],
        [pltpu.VMEM((tb + 8, GROUP_W), F32), pltpu.VMEM((8, GROUP_W), F32)],
        mix, mix_rows, "lru")


HALF = D_MODEL // 2


def _pack_bf16_pairs(x):
    bits = lax.bitcast_convert_type(x.astype(BF16).astype(F32), jnp.uint32)
    return (bits[:, :HALF] >> 16) | (bits[:, HALF:] & jnp.uint32(0xFFFF0000))


def _unpack_pairs_f32(w):
    lo = lax.bitcast_convert_type(w << 16, F32)
    hi = lax.bitcast_convert_type(w & jnp.uint32(0xFFFF0000), F32)
    return lo, hi


def _unpack_bf16_pairs(w):
    lo, hi = _unpack_pairs_f32(w)
    return lo.astype(BF16), hi.astype(BF16)


DMA_UNROLL = 8

ROW_TILE = HALF // LANE


def _store_tile_rows(ref, row0, mat):
    n = mat.shape[0]
    for j in range(ROW_TILE):
        ref[pl.ds(row0 * ROW_TILE + j, n, stride=ROW_TILE), :] = mat[:, LANE * j:LANE * (j + 1)]


def _load_tile_cols(ref, n, j):
    return ref[pl.ds(j, n, stride=ROW_TILE), :]


OUT_SUB = 256


def _out_proj_kernel(mix_ref, h_ref, w_ref, g_ref, b_ref, rw_ref, x1_ref, xp_ref, sc_ref):
    for r in range(TM // OUT_SUB):
        rows = slice(r * OUT_SUB, (r + 1) * OUT_SUB)
        y = _dot(mix_ref[rows, :].astype(BF16), w_ref[...])
        x1 = _ln(ALPHA * h_ref[rows, :] + y, g_ref[...], b_ref[...])
        x1_ref[rows, :] = x1
        _store_tile_rows(xp_ref, r * OUT_SUB, _pack_bf16_pairs(x1))
        logits = lax.dot_general(x1, rw_ref[...], (((1,), (0,)), ((), ())),
                                 preferred_element_type=F32)
        sc_ref[rows, :] = jax.nn.sigmoid(logits)


def out_proj(mix, h, w_bf, g, b, rw):
    return pl.pallas_call(
        _out_proj_kernel,
        grid=(N_TILES,),
        in_specs=[
            pl.BlockSpec((TM, D_MODEL), lambda i: (i, 0)),
            pl.BlockSpec((TM, D_MODEL), lambda i: (i, 0)),
            _full((D_MODEL, D_MODEL)),
            _full((1, D_MODEL)), _full((1, D_MODEL)),
            _full((D_MODEL, LANE)),
        ],
        out_specs=[pl.BlockSpec((TM, D_MODEL), lambda i: (i, 0)),
                   pl.BlockSpec((TM * ROW_TILE, LANE), lambda i: (i, 0)),
                   pl.BlockSpec((TM, LANE), lambda i: (i, 0))],
        out_shape=[jax.ShapeDtypeStruct((NT, D_MODEL), F32),
                   jax.ShapeDtypeStruct((NT * ROW_TILE, LANE), jnp.uint32),
                   jax.ShapeDtypeStruct((NT, LANE), F32)],
        compiler_params=_cparams(("arbitrary",)),
        name="out_proj",
    )(mix, h, w_bf, g, b, rw)


_BIG = 4096


def _group_allreduce(x, lane, op):
    for sh in (1, 2, 4):
        up = pltpu.roll(x, sh, axis=1)
        dn = pltpu.roll(x, LANE - sh, axis=1)
        x = op(x, jnp.where((lane & sh) != 0, up, dn))
    return x


def _router_kernel(sc_ref, bias_ref, e_ref, r_ref, w_ref, cnt_ref, run):
    i = pl.program_id(0)

    @pl.when(i == 0)
    def _():
        run[...] = jnp.zeros_like(run)

    sc = sc_ref[...]
    lane = lax.broadcasted_iota(jnp.int32, (TM, LANE), 1)
    valid = lane < N_EXPERTS
    neg = -jnp.inf
    biased = jnp.where(valid, sc + bias_ref[...], neg)
    gmax = _group_allreduce(biased, lane, jnp.maximum)
    first = _group_allreduce(jnp.where(biased == gmax, lane, _BIG), lane, jnp.minimum)
    second = _group_allreduce(jnp.where(lane == first, neg, biased), lane, jnp.maximum)
    gs = jnp.where(valid, gmax + second, neg)
    grp = lane >> 3
    cand = jnp.full((TM, LANE), neg, F32)
    for _ in range(TOPK_GROUP):
        m = jnp.max(gs, axis=1, keepdims=True)
        g1 = jnp.min(jnp.where(gs == m, grp, _BIG), axis=1, keepdims=True)
        hit = grp == g1
        cand = jnp.where(hit, biased, cand)
        gs = jnp.where(hit, neg, gs)
    sel = jnp.zeros((TM, LANE), F32)
    e_out = jnp.zeros((TM, LANE), jnp.int32)
    w_out = jnp.zeros((TM, LANE), F32)
    idxs = []
    for k in range(TOP_K):
        m = jnp.max(cand, axis=1, keepdims=True)
        ik = jnp.min(jnp.where(cand == m, lane, _BIG), axis=1, keepdims=True)
        hit = lane == ik
        vk = jnp.sum(jnp.where(hit, sc, 0.0), axis=1, keepdims=True)
        sel = jnp.where(hit, 1.0, sel)
        cand = jnp.where(hit, neg, cand)
        e_out = jnp.where(lane == k, ik, e_out)
        w_out = jnp.where(lane == k, vk, w_out)
        idxs.append(ik)
    wsum = jnp.sum(w_out, axis=1, keepdims=True)
    w_ref[...] = w_out / wsum * ROUTE_SCALE
    e_ref[...] = e_out
    rowi = lax.broadcasted_iota(jnp.int32, (TM, 1), 0)
    sel = jnp.where(rowi < NT - i * TM, sel, 0.0)
    ri = lax.broadcasted_iota(jnp.int32, (TM, TM), 0)
    ci = lax.broadcasted_iota(jnp.int32, (TM, TM), 1)
    before = _dot((ri > ci).astype(BF16), sel.astype(BF16))
    rank = run[0:1, :] + before
    r_out = jnp.zeros((TM, LANE), F32)
    for k in range(TOP_K):
        rk = jnp.sum(jnp.where(lane == idxs[k], rank, 0.0), axis=1, keepdims=True)
        r_out = jnp.where(lane == k, rk, r_out)
    r_ref[...] = r_out.astype(jnp.int32)
    run[...] = jnp.broadcast_to(rank[TM - 1:TM, :] + sel[TM - 1:TM, :], run.shape)

    @pl.when(i == pl.num_programs(0) - 1)
    def _():
        cnt_ref[...] = run[...]


def router(scores, bias_row):
    return pl.pallas_call(
        _router_kernel,
        grid=(N_TILES,),
        in_specs=[pl.BlockSpec((TM, LANE), lambda i: (i, 0)), _full((1, LANE))],
        out_specs=[pl.BlockSpec((TM, LANE), lambda i: (i, 0)),
                   pl.BlockSpec((TM, LANE), lambda i: (i, 0)),
                   pl.BlockSpec((TM, LANE), lambda i: (i, 0)),
                   _full((8, LANE))],
        out_shape=[jax.ShapeDtypeStruct((NT, LANE), jnp.int32),
                   jax.ShapeDtypeStruct((NT, LANE), jnp.int32),
                   jax.ShapeDtypeStruct((NT, LANE), F32),
                   jax.ShapeDtypeStruct((8, LANE), F32)],
        scratch_shapes=[pltpu.VMEM((8, LANE), F32)],
        compiler_params=_cparams(("arbitrary",)),
        name="router",
    )(scores, bias_row)


def _dispatch_kernel(cnt_ref, pst_ref, dest_ref, x_ref, xg_hbm, zbuf, sem, zsem):
    i = pl.program_id(0)

    def tile_rows(row):
        return pl.ds(pl.multiple_of(row * ROW_TILE, ROW_TILE), ROW_TILE)

    def row_copy(src_row, dst_tile_row):
        dst = pl.ds(pl.multiple_of(dst_tile_row, ROW_TILE), ROW_TILE)
        return pltpu.make_async_copy(x_ref.at[tile_rows(src_row)], xg_hbm.at[dst], sem)

    def zero_copy(dst_row):
        return pltpu.make_async_copy(zbuf, xg_hbm.at[tile_rows(dst_row)], zsem)

    @pl.when(i == 0)
    def _():
        zbuf[...] = jnp.zeros_like(zbuf)

        def per_expert(e, carry):
            c = cnt_ref[e]
            npad = (MOE_T - c % MOE_T) % MOE_T
            base = pst_ref[e] + c

            def start(r, cc):
                zero_copy(base + r).start()
                return cc

            lax.fori_loop(0, npad, start, 0)

            def wait(r, cc):
                zero_copy(base + r).wait()
                return cc

            lax.fori_loop(0, npad, wait, 0)
            return carry

        lax.fori_loop(0, N_EXPERTS, per_expert, 0)

    def start(t, carry):
        for s in range(TOP_K):
            row_copy(t, dest_ref[t * TOP_K + s]).start(priority=s % 2)
        return carry

    lax.fori_loop(0, TM, start, 0, unroll=DMA_UNROLL)

    for s in range(TOP_K):
        pltpu.make_async_copy(x_ref, x_ref, sem).wait()


def dispatch(counts, pstarts, dest_flat, x1p):
    grid_spec = pltpu.PrefetchScalarGridSpec(
        num_scalar_prefetch=2,
        grid=(N_TILES,),
        in_specs=[
            pl.BlockSpec((TM * TOP_K,), lambda i, c, p: (i,), memory_space=pltpu.SMEM),
            pl.BlockSpec((TM * ROW_TILE, LANE), lambda i, c, p: (i, 0)),
        ],
        out_specs=pl.BlockSpec(memory_space=pl.ANY),
        scratch_shapes=[pltpu.VMEM((ROW_TILE, LANE), jnp.uint32),
                        pltpu.SemaphoreType.DMA(()), pltpu.SemaphoreType.DMA(())],
    )
    return pl.pallas_call(
        _dispatch_kernel,
        grid_spec=grid_spec,
        out_shape=jax.ShapeDtypeStruct(((MOE_NB * MOE_T + N_TRASH) * ROW_TILE, LANE), jnp.uint32),
        compiler_params=_cparams(("arbitrary",)),
        name="dispatch",
    )(counts, pstarts, dest_flat, x1p)


def _moe_kernel(blk_e_ref, nused_ref, next_e_ref, x_ref, wg_hbm, wu_hbm, wd_hbm, o_ref,
                wg_st, wu_st, wd_st, wg_bf, wu_bf, wd_bf, sems, *, layer):
    i = pl.program_id(0)
    changed = jnp.logical_or(i == 0, blk_e_ref[i] != blk_e_ref[jnp.maximum(i - 1, 0)])
    active = i < nused_ref[0]

    def fetch(e):
        return (pltpu.make_async_copy(wg_hbm.at[layer, e], wg_st, sems.at[0]),
                pltpu.make_async_copy(wu_hbm.at[layer, e], wu_st, sems.at[1]),
                pltpu.make_async_copy(wd_hbm.at[layer, e], wd_st, sems.at[2]))

    @pl.when(i == 0)
    def _():
        for c in fetch(blk_e_ref[0]):
            c.start()

    @pl.when(jnp.logical_and(changed, active))
    def _():
        for c in fetch(blk_e_ref[i]):
            c.wait()
        wg_bf[...] = wg_st[...].astype(BF16)
        wu_bf[...] = wu_st[...].astype(BF16)
        wd_bf[...] = wd_st[...].astype(BF16)

        @pl.when(next_e_ref[i] >= 0)
        def _():
            for c in fetch(next_e_ref[i]):
                c.start()

    @pl.when(active)
    def _():
        halves = [_unpack_bf16_pairs(_load_tile_cols(x_ref, MOE_T, j)) for j in range(ROW_TILE)]
        lo = jnp.concatenate([p[0] for p in halves], axis=1)
        hi = jnp.concatenate([p[1] for p in halves], axis=1)
        gate = _dot(lo, wg_bf[0:HALF, :]) + _dot(hi, wg_bf[HALF:D_MODEL, :])
        up = _dot(lo, wu_bf[0:HALF, :]) + _dot(hi, wu_bf[HALF:D_MODEL, :])
        hb = jax.nn.silu(gate) * up
        _store_tile_rows(o_ref, 0, _pack_bf16_pairs(_dot(hb.astype(BF16), wd_bf[...])))

    @pl.when(i >= nused_ref[0])
    def _():
        o_ref[...] = jnp.zeros_like(o_ref)


def moe_experts(blk_e, nused, next_e, xg, wg, wu, wd, layer):
    grid_spec = pltpu.PrefetchScalarGridSpec(
        num_scalar_prefetch=3,
        grid=(MOE_NB,),
        in_specs=[
            pl.BlockSpec((MOE_T * ROW_TILE, LANE),
                         lambda i, be, nu, ne: (jnp.minimum(i, nu[0] - 1), 0)),
            pl.BlockSpec(memory_space=pl.ANY),
            pl.BlockSpec(memory_space=pl.ANY),
            pl.BlockSpec(memory_space=pl.ANY),
        ],
        out_specs=pl.BlockSpec((MOE_T * ROW_TILE, LANE), lambda i, be, nu, ne: (i, 0)),
        scratch_shapes=[pltpu.VMEM((D_MODEL, D_EXPERT), F32),
                        pltpu.VMEM((D_MODEL, D_EXPERT), F32),
                        pltpu.VMEM((D_EXPERT, D_MODEL), F32),
                        pltpu.VMEM((D_MODEL, D_EXPERT), BF16),
                        pltpu.VMEM((D_MODEL, D_EXPERT), BF16),
                        pltpu.VMEM((D_EXPERT, D_MODEL), BF16),
                        pltpu.SemaphoreType.DMA((3,))],
    )
    return pl.pallas_call(
        functools.partial(_moe_kernel, layer=layer),
        grid_spec=grid_spec,
        out_shape=jax.ShapeDtypeStruct((MOE_NB * MOE_T * ROW_TILE, LANE), jnp.uint32),
        compiler_params=_cparams(("arbitrary",)),
        name="moe_experts",
    )(blk_e, nused, next_e, xg, wg, wu, wd)


TMC = 256
NC_TILES = -(-NT // TMC)
DEST_LEN = max(N_TILES * TM, NC_TILES * TMC) * TOP_K
N_TRASH = DEST_LEN - NK


def _combine_kernel(dest_ref, x1_ref, w_ref, yb_hbm, wg_ref, wu_ref, wd_ref, g_ref, b_ref, o_ref,
                    ybuf, sem):
    def row_copy(t, s):
        src = pl.ds(pl.multiple_of(dest_ref[t * TOP_K + s], ROW_TILE), ROW_TILE)
        dst = pl.ds(pl.multiple_of(t * ROW_TILE, ROW_TILE), ROW_TILE)
        return pltpu.make_async_copy(yb_hbm.at[src], ybuf.at[s, dst], sem)

    def start(t, carry):
        for s in range(TOP_K):
            row_copy(t, s).start(priority=s % 2)
        return carry

    lax.fori_loop(0, TMC, start, 0, unroll=DMA_UNROLL)

    x1 = x1_ref[...]
    xb = x1.astype(BF16)
    hb = jax.nn.silu(_dot(xb, wg_ref[...])) * _dot(xb, wu_ref[...])
    shared = _dot(hb.astype(BF16), wd_ref[...])

    pltpu.make_async_copy(ybuf, ybuf, sem).wait()

    w = w_ref[...]
    acc_lo = [shared[:, LANE * j:LANE * (j + 1)] for j in range(ROW_TILE)]
    acc_hi = [shared[:, HALF + LANE * j:HALF + LANE * (j + 1)] for j in range(ROW_TILE)]
    for s in range(TOP_K):
        ws = _col(w, s)
        for j in range(ROW_TILE):
            lo, hi = _unpack_pairs_f32(_load_tile_cols(ybuf.at[s], TMC, j))
            acc_lo[j] = acc_lo[j] + ws * lo
            acc_hi[j] = acc_hi[j] + ws * hi
    acc = jnp.concatenate(acc_lo + acc_hi, axis=1)
    o_ref[...] = _ln(ALPHA * x1 + acc, g_ref[...], b_ref[...])


def combine_shared_ln2(dest_flat, x1, w, yb, wg, wu, wd, g, b):
    grid_spec = pl.GridSpec(
        grid=(NC_TILES,),
        in_specs=[
            pl.BlockSpec((TMC * TOP_K,), lambda i: (i,), memory_space=pltpu.SMEM),
            pl.BlockSpec((TMC, D_MODEL), lambda i: (i, 0)),
            pl.BlockSpec((TMC, LANE), lambda i: (i, 0)),
            pl.BlockSpec(memory_space=pl.ANY),
            _full((D_MODEL, D_EXPERT)), _full((D_MODEL, D_EXPERT)), _full((D_EXPERT, D_MODEL)),
            _full((1, D_MODEL)), _full((1, D_MODEL)),
        ],
        out_specs=pl.BlockSpec((TMC, D_MODEL), lambda i: (i, 0)),
        scratch_shapes=[pltpu.VMEM((TOP_K, TMC * ROW_TILE, LANE), jnp.uint32),
                        pltpu.SemaphoreType.DMA(())],
    )
    return pl.pallas_call(
        _combine_kernel,
        grid_spec=grid_spec,
        out_shape=jax.ShapeDtypeStruct((NT, D_MODEL), F32),
        compiler_params=_cparams(("arbitrary",)),
        name="combine_shared_ln2",
    )(dest_flat, x1, w, yb, wg, wu, wd, g, b)


def _schedule(counts, e_sel, rank_sel):
    padded = (counts + MOE_T - 1) // MOE_T * MOE_T
    pends = jnp.cumsum(padded)
    pstarts = pends - padded
    onehot = e_sel[:, :, None] == jnp.arange(N_EXPERTS, dtype=jnp.int32)
    dest = rank_sel + jnp.sum(jnp.where(onehot, pstarts, 0), axis=-1)
    dest_flat = dest.reshape(NK).astype(jnp.int32) * ROW_TILE
    trash = (MOE_NB * MOE_T + jnp.arange(N_TRASH, dtype=jnp.int32)) * ROW_TILE
    dest_disp = jnp.concatenate([dest_flat, trash])
    dest_comb = jnp.pad(dest_flat, (0, N_TRASH))
    blk_row0 = jnp.arange(MOE_NB, dtype=jnp.int32) * MOE_T
    experts_ending_by = lambda row: jnp.sum((pends[None, :] <= row[:, None]).astype(jnp.int32), axis=1)
    blk_e = jnp.minimum(experts_ending_by(blk_row0), N_EXPERTS - 1).astype(jnp.int32)
    nused = (pends[-1] // MOE_T).astype(jnp.int32).reshape(1)
    group_end = jnp.sum(jnp.where(blk_e[:, None] == jnp.arange(N_EXPERTS, dtype=jnp.int32), pends, 0), axis=1)
    next_e = jnp.where(group_end < pends[-1],
                       jnp.minimum(experts_ending_by(group_end), N_EXPERTS - 1), -1).astype(jnp.int32)
    return pstarts.astype(jnp.int32), dest_disp, dest_comb, blk_e, nused, next_e


def _pad_rows(x, rows):
    b, r, c = x.shape
    return jnp.pad(x, ((0, 0), (rows - r, 0), (0, 0))).reshape(b * rows, c)


def _layer(h, st, lw):
    proj = in_proj(h, lw['w_in'])
    sinks = lw['sinks']
    zeros16 = jnp.zeros((BATCH * 16, GROUP_W), F32)
    zeros8 = jnp.zeros((BATCH * 8, C_CONV_CH), F32)

    meta_blk = lambda b, j=None: M_ROW0 // N_META + b
    mix = attn_call(sinks, proj, proj, (proj, proj), (proj, proj), nb=BATCH, length=N_META,
                    ch=N_META, nq=1, use_meta=False, chunk0=0, q_row0=M_ROW0,
                    kp_map=lambda b, j: 0, km_map=meta_blk,
                    kp_cols=(C_K // KV_W, C_V // KV_W), km_cols=(C_K // KV_W, C_V // KV_W),
                    mix=None, mix_rows=NT)
    nq_p = 4
    mix = attn_call(sinks, proj, proj, (proj, proj), (proj, proj), nb=BATCH, length=SEQ,
                    ch=64, nq=nq_p, use_meta=True, chunk0=0, q_row0=0,
                    kp_map=lambda b, j: jnp.maximum(b * (SEQ // WINDOW) + j * (64 * nq_p // WINDOW) - 1, 0),
                    km_map=meta_blk,
                    kp_cols=(C_K // KV_W, C_V // KV_W), km_cols=(C_K // KV_W, C_V // KV_W),
                    mix=mix, mix_rows=NT)
    mix = attn_call(sinks, proj, proj, (st['win_k'], st['win_v']), (st['meta_k'], st['meta_v']),
                    nb=DEC_BATCH, length=DEC_SEQ, ch=64, nq=1, use_meta=True, chunk0=2,
                    q_row0=S_ROW0, kp_map=lambda b, j: b, km_map=lambda b, j: b,
                    kp_cols=(0, 0), km_cols=(0, 0), mix=mix, mix_rows=NT)

    mix = pool_call(proj, zeros16, lambda b: b, 0, lw['pool_w'], lw['pool_scale'],
                    nb=BATCH, length=N_META, tb=N_META, row0=M_ROW0, ramp=True, mix=mix, mix_rows=NT)
    mix = pool_call(proj, proj, lambda b: M_ROW0 // 16 + b, C_U // GROUP_W, lw['pool_w'],
                    lw['pool_scale'], nb=BATCH, length=SEQ, tb=512, row0=0, ramp=False,
                    mix=mix, mix_rows=NT)
    mix = pool_call(proj, st['pool'], lambda b: b, 0, lw['pool_w'], lw['pool_scale'],
                    nb=DEC_BATCH, length=DEC_SEQ, tb=DEC_SEQ, row0=S_ROW0, ramp=False,
                    mix=mix, mix_rows=NT)

    ssd_w = lw['ssd']
    h0z = jnp.zeros((BATCH, 512, C_STATE), F32)
    mix, hc_m = ssd_call(proj, zeros8, lambda b: b, (0, 2, 3), h0z, ssd_w,
                         nb=BATCH, length=N_META, q=N_META, row0=M_ROW0, mix=mix, mix_rows=NT)
    meta_tail = lambda b: (M_ROW0 + 8) // 8 + 2 * b
    mix, hc_p = ssd_call(proj, proj, meta_tail, (C_XS // 512, C_B // 256, C_C // 256), hc_m, ssd_w,
                         nb=BATCH, length=SEQ, q=256, row0=0, mix=mix, mix_rows=NT)
    mix, hc_s = ssd_call(proj, st['ssm_conv'], lambda b: b, (0, 2, 3), st['ssm'], ssd_w,
                         nb=DEC_BATCH, length=DEC_SEQ, q=DEC_SEQ, row0=S_ROW0, mix=mix, mix_rows=NT)

    lru_w = lw['lru']
    l0z = jnp.zeros((BATCH, 1, GROUP_W), F32)
    mix, hd_m = lru_call(proj, zeros8, lambda b: b, 0, l0z, lru_w,
                         nb=BATCH, length=N_META, tb=N_META, row0=M_ROW0, mix=mix, mix_rows=NT)
    mix, hd_p = lru_call(proj, proj, meta_tail, C_RX // GROUP_W, hd_m, lru_w,
                         nb=BATCH, length=SEQ, tb=256, row0=0, mix=mix, mix_rows=NT)
    mix, hd_s = lru_call(proj, st['lru_conv'], lambda b: b, 0, st['lru'], lru_w,
                         nb=DEC_BATCH, length=DEC_SEQ, tb=DEC_SEQ, row0=S_ROW0, mix=mix, mix_rows=NT)

    x1, x1p, sc = out_proj(mix, h, lw['w_out'], lw['ln1_g'], lw['ln1_b'], lw['rw'])

    e_sel, rank_sel, w_sel, cnt = router(sc, lw['router_bias'])
    counts = cnt[0, :N_EXPERTS].astype(jnp.int32)
    pstarts, dest_disp, dest_comb, blk_e, nused, next_e = _schedule(
        counts, e_sel[:, :TOP_K], rank_sel[:, :TOP_K])
    xg = dispatch(counts, pstarts, dest_disp, x1p)
    yb = moe_experts(blk_e, nused, next_e, xg, lw['wg'], lw['wu'], lw['wd'], lw['layer'])
    h_new = combine_shared_ln2(dest_comb, x1, w_sel, yb, lw['sh_wg'], lw['sh_wu'], lw['sh_wd'],
                               lw['ln2_g'], lw['ln2_b'])

    def tail(row0, nb, length, nrows, c0, width):
        return jnp.stack([proj[row0 + (b + 1) * length - nrows:row0 + (b + 1) * length, c0:c0 + width]
                          for b in range(nb)])

    kv4 = lambda x: x.reshape(x.shape[0], x.shape[1], A_KV_HEADS, HEAD_DIM)
    p_state = (
        kv4(tail(M_ROW0, BATCH, N_META, N_META, C_K, KV_W)),
        kv4(tail(M_ROW0, BATCH, N_META, N_META, C_V, KV_W)),
        kv4(tail(0, BATCH, SEQ, WINDOW, C_K, KV_W)), kv4(tail(0, BATCH, SEQ, WINDOW, C_V, KV_W)),
        tail(0, BATCH, SEQ, POOL_STATE, C_U, GROUP_W),
        tail(0, BATCH, SEQ, CONV_W - 1, C_XS, C_CONV_CH),
        hc_p.reshape(BATCH, C_HEADS, 64, C_STATE),
        tail(0, BATCH, SEQ, CONV_W - 1, C_RX, GROUP_W),
        hd_p.reshape(BATCH, GROUP_W),
    )
    s_state = (
        kv4(tail(S_ROW0, DEC_BATCH, DEC_SEQ, DEC_SEQ, C_K, KV_W)),
        kv4(tail(S_ROW0, DEC_BATCH, DEC_SEQ, DEC_SEQ, C_V, KV_W)),
        tail(S_ROW0, DEC_BATCH, DEC_SEQ, POOL_STATE, C_U, GROUP_W),
        tail(S_ROW0, DEC_BATCH, DEC_SEQ, CONV_W - 1, C_XS, C_CONV_CH),
        hc_s.reshape(DEC_BATCH, C_HEADS, 64, C_STATE),
        tail(S_ROW0, DEC_BATCH, DEC_SEQ, CONV_W - 1, C_RX, GROUP_W),
        hd_s.reshape(DEC_BATCH, GROUP_W),
    )
    return h_new, p_state, s_state


def _block_diag(w):
    z = jnp.zeros((D_BLOCK_W, D_BLOCK_W), w.dtype)
    return jnp.stack([jnp.block([[w[2 * s], z], [z, w[2 * s + 1]]]) for s in range(4)])


def _pad_lanes(v, width=LANE):
    return jnp.pad(v, (0, width - v.shape[0])).reshape(1, width)


def kernel(x_prompt, x_sample, cache_attn_meta_k, cache_attn_meta_v, cache_attn_k, cache_attn_v, state_pool, state_ssm_conv, state_ssm, state_lru_conv, state_lru, meta_tokens, ln_in_g, ln_in_b, w_in, w_out, attn_sinks, pool_w, pool_scale, ssm_conv_w, ssm_conv_b, ssm_dt_bias, ssm_a_log, ssm_d, ssm_norm_g, lru_conv_w, lru_conv_b, lru_wr, lru_br, lru_wi, lru_bi, lru_lambda, ln1_g, ln1_b, ln2_g, ln2_b, router_w, router_bias, exp_w_gate, exp_w_up, exp_w_down, sh_w_gate, sh_w_up, sh_w_down):
    row = lambda v: v.reshape(1, -1).astype(F32)
    h = ln_in(x_prompt.reshape(P_ROWS, D_MODEL), x_sample.reshape(S_ROWS, D_MODEL),
              meta_tokens.astype(F32), row(ln_in_g), row(ln_in_b))
    p_states, s_states = [], []
    for i in range(DEPTH):
        wi = w_in[i]
        s0 = 0
        parts = {}
        for name, size in zip(('q', 'k', 'v', 'u', 'z', 'xbc', 'dt', 'rx', 'rg'),
                              (512, 128, 128, 512, 512, 1024, 8, 512, 512)):
            parts[name] = wi[:, s0:s0 + size]
            s0 += size
        w_in_p = jnp.concatenate(
            [parts[n] for n in ('q', 'u', 'z', 'rx', 'rg', 'xbc', 'k', 'v', 'dt')]
            + [jnp.zeros((D_MODEL, PROJ_P - C_DT - C_HEADS), F32)], axis=1).astype(BF16)
        rw = jnp.pad(router_w[i].astype(F32), ((0, 0), (0, LANE - N_EXPERTS))).astype(BF16)
        cw = ssm_conv_w[i].astype(F32)
        cb = ssm_conv_b[i].astype(F32)
        lw = dict(
            w_in=w_in_p, w_out=w_out[i].astype(BF16), sinks=attn_sinks[i].astype(F32),
            pool_w=pool_w[i].astype(BF16), pool_scale=row(pool_scale[i]),
            ssd=dict(cw_x=cw[:, :512], cw_b=cw[:, 512:768], cw_c=cw[:, 768:],
                     cb_x=row(cb[:512]), cb_b=row(cb[512:768]), cb_c=row(cb[768:]),
                     dt_bias=_pad_lanes(ssm_dt_bias[i].astype(F32)),
                     a_log=_pad_lanes(ssm_a_log[i].astype(F32)),
                     d_skip=row(jnp.repeat(ssm_d[i].astype(F32), 64)),
                     norm_g=row(ssm_norm_g[i])),
            lru=dict(cw=lru_conv_w[i].astype(F32), cb=row(lru_conv_b[i]),
                     wr=_block_diag(lru_wr[i]).astype(BF16), wi=_block_diag(lru_wi[i]).astype(BF16),
                     br=row(lru_br[i]), bi=row(lru_bi[i]), lam=row(lru_lambda[i])),
            ln1_g=row(ln1_g[i]), ln1_b=row(ln1_b[i]), ln2_g=row(ln2_g[i]), ln2_b=row(ln2_b[i]),
            rw=rw, router_bias=_pad_lanes(router_bias[i].astype(F32)),
            wg=exp_w_gate, wu=exp_w_up, wd=exp_w_down, layer=i,
            sh_wg=sh_w_gate[i].astype(BF16), sh_wu=sh_w_up[i].astype(BF16),
            sh_wd=sh_w_down[i].astype(BF16),
        )
        st = dict(
            meta_k=cache_attn_meta_k[i].reshape(DEC_BATCH * N_META, KV_W),
            meta_v=cache_attn_meta_v[i].reshape(DEC_BATCH * N_META, KV_W),
            win_k=cache_attn_k[i].reshape(DEC_BATCH * WINDOW, KV_W),
            win_v=cache_attn_v[i].reshape(DEC_BATCH * WINDOW, KV_W),
            pool=_pad_rows(state_pool[i], 16),
            ssm_conv=_pad_rows(state_ssm_conv[i], 8),
            ssm=state_ssm[i].reshape(DEC_BATCH, 512, C_STATE),
            lru_conv=_pad_rows(state_lru_conv[i], 8),
            lru=state_lru[i].reshape(DEC_BATCH, 1, GROUP_W),
        )
        h, ps, ss = _layer(h, st, lw)
        p_states.append(ps)
        s_states.append(ss)
    stk = lambda sts, j: jnp.stack([s[j] for s in sts])
    y_prompt = h[:P_ROWS].reshape(BATCH, SEQ, D_MODEL)
    y_sample = h[S_ROW0:S_ROW0 + S_ROWS].reshape(DEC_BATCH, DEC_SEQ, D_MODEL)
    return ((y_prompt, y_sample)
            + tuple(stk(p_states, j) for j in range(9))
            + tuple(stk(s_states, j) for j in range(7)))
```

```python
import functools
import math

import jax
import jax.numpy as jnp
from jax import lax
from jax.experimental import pallas as pl
from jax.experimental.pallas import tpu as pltpu

F32 = jnp.float32
BF16 = jnp.bfloat16

D_MODEL = 2048
BATCH = 4
SEQ = 4096
DEPTH = 4
DEC_BATCH = 16
DEC_SEQ = 64
N_META = 16
GROUP_W = 512
HEAD_DIM = 64
A_HEADS = 8
A_KV_HEADS = 2
A_GROUP = 4
KV_W = 128
WINDOW = 128
POOL_SIZES = (2, 4, 8, 16)
POOL_GW = 128
POOL_STATE = 15
C_HEADS = 8
C_STATE = 128
C_CONV_CH = 1024
CONV_W = 4
D_BLOCKS = 8
D_BLOCK_W = 64
LRU_C = 8.0
N_EXPERTS = 64
TOP_K = 8
N_GROUP = 8
TOPK_GROUP = 4
D_EXPERT = 512
ROUTE_SCALE = 2.5
ALPHA = (2 * DEPTH) ** 0.25
LN_EPS = 1e-5

LANE = 128
SUBLANE = 8
VMEM_LIMIT = 56 * 1024 * 1024

P_ROWS = BATCH * SEQ
S_ROWS = DEC_BATCH * DEC_SEQ
M_ROWS = BATCH * N_META
S_ROW0 = P_ROWS
M_ROW0 = P_ROWS + S_ROWS
NT = P_ROWS + S_ROWS + M_ROWS
TM = 512
N_TILES = -(-NT // TM)

C_Q, C_U, C_Z, C_RX, C_RG, C_XS, C_B, C_C, C_K, C_V, C_DT = (
    0, 512, 1024, 1536, 2048, 2560, 3072, 3328, 3584, 3712, 3840)
PROJ_P = 4096
PROJ_TN = 1024

MOE_T = 512
NK = NT * TOP_K
MOE_NB = (NK + N_EXPERTS * (MOE_T - 1) + MOE_T - 1) // MOE_T


def _cparams(sem):
    return pltpu.CompilerParams(dimension_semantics=sem, vmem_limit_bytes=VMEM_LIMIT)


def _ln(x, g, b):
    mu = jnp.mean(x, axis=-1, keepdims=True)
    xc = x - mu
    var = jnp.mean(xc * xc, axis=-1, keepdims=True)
    return xc * lax.rsqrt(var + LN_EPS) * g + b


def _dot(a, b):
    return jnp.dot(a, b, preferred_element_type=F32)


def _dot_nt(a, b, precision=None):
    return lax.dot_general(a, b, (((1,), (1,)), ((), ())), precision=precision,
                           preferred_element_type=F32)


def _full(shape):
    nd = len(shape)
    return pl.BlockSpec(shape, lambda *_: (0,) * nd)


MIX_W = 4 * GROUP_W


def _skip_ref(kern, idx, *refs):
    return kern(*refs[:idx], *refs[idx + 1:])


def _mixer_call(kern, grid, in_specs, args, y_spec, more_out_specs, more_out_shapes, scratch,
                mix, mix_rows, name):
    in_specs = list(in_specs)
    args = tuple(args)
    aliases = {}
    if mix is not None:
        n_in = len(in_specs)
        kern = functools.partial(_skip_ref, kern, n_in)
        in_specs.append(pl.BlockSpec(memory_space=pl.ANY))
        args = args + (mix,)
        aliases = {n_in: 0}
    return pl.pallas_call(
        kern,
        grid=grid,
        in_specs=in_specs,
        out_specs=[y_spec] + list(more_out_specs),
        out_shape=[jax.ShapeDtypeStruct((mix_rows, MIX_W), F32)] + list(more_out_shapes),
        scratch_shapes=scratch,
        input_output_aliases=aliases,
        compiler_params=_cparams(("arbitrary",) * len(grid)),
        name=name,
    )(*args)


def _ln_in_kernel(xp_ref, xs_ref, meta_ref, g_ref, b_ref, o_ref):
    i = pl.program_id(0)
    g = g_ref[...]
    b = b_ref[...]
    n_p = P_ROWS // TM
    n_s = S_ROWS // TM

    @pl.when(i < n_p)
    def _():
        o_ref[...] = _ln(xp_ref[...], g, b)

    @pl.when((i >= n_p) & (i < n_p + n_s))
    def _():
        o_ref[...] = _ln(xs_ref[...], g, b)

    @pl.when(i == n_p + n_s)
    def _():
        m = _ln(meta_ref[...], g, b)
        for r in range(BATCH):
            o_ref[N_META * r:N_META * (r + 1), :] = m


def ln_in(xp2, xs2, meta, g, b):
    n_p = P_ROWS // TM
    n_s = S_ROWS // TM
    return pl.pallas_call(
        _ln_in_kernel,
        grid=(N_TILES,),
        in_specs=[
            pl.BlockSpec((TM, D_MODEL), lambda i: (jnp.minimum(i, n_p - 1), 0)),
            pl.BlockSpec((TM, D_MODEL), lambda i: (jnp.clip(i - n_p, 0, n_s - 1), 0)),
            _full((N_META, D_MODEL)),
            _full((1, D_MODEL)),
            _full((1, D_MODEL)),
        ],
        out_specs=pl.BlockSpec((TM, D_MODEL), lambda i: (i, 0)),
        out_shape=jax.ShapeDtypeStruct((NT, D_MODEL), F32),
        compiler_params=_cparams(("arbitrary",)),
        name="ln_in",
    )(xp2, xs2, meta, g, b)


def _in_proj_kernel(x_ref, w_ref, o_ref):
    o_ref[...] = _dot(x_ref[...].astype(BF16), w_ref[...])


PROJ_TM = 1024


def in_proj(h, w_bf):
    return pl.pallas_call(
        _in_proj_kernel,
        grid=(-(-NT // PROJ_TM), PROJ_P // PROJ_TN),
        in_specs=[
            pl.BlockSpec((PROJ_TM, D_MODEL), lambda i, n: (i, 0)),
            pl.BlockSpec((D_MODEL, PROJ_TN), lambda i, n: (0, n)),
        ],
        out_specs=pl.BlockSpec((PROJ_TM, PROJ_TN), lambda i, n: (i, n)),
        out_shape=jax.ShapeDtypeStruct((NT, PROJ_P), F32),
        compiler_params=_cparams(("arbitrary", "arbitrary")),
        name="in_proj",
    )(h, w_bf)


ATT_PAD = 64
ATT_WIN = ATT_PAD + WINDOW + 64


def _attn_kernel(sink_ref, q_ref, kc_ref, vc_ref, kp_ref, vp_ref, km_ref, vm_ref, o_ref,
                 kbuf, vbuf, *, ch, nq, use_meta, chunk0):
    j = pl.program_id(1)
    tq = ch * nq
    zpad = jnp.zeros((ATT_PAD - N_META, KV_W), BF16)
    kbuf[0:N_META, :] = km_ref[...].astype(BF16)
    kbuf[N_META:ATT_PAD, :] = zpad
    vbuf[0:N_META, :] = vm_ref[...].astype(BF16)
    vbuf[N_META:ATT_PAD, :] = zpad
    kbuf[ATT_PAD:ATT_PAD + WINDOW, :] = kp_ref[...].astype(BF16)
    vbuf[ATT_PAD:ATT_PAD + WINDOW, :] = vp_ref[...].astype(BF16)
    kbuf[ATT_PAD + WINDOW:ATT_PAD + WINDOW + tq, :] = kc_ref[...].astype(BF16)
    vbuf[ATT_PAD + WINDOW:ATT_PAD + WINDOW + tq, :] = vc_ref[...].astype(BF16)
    if ch < 64:
        zc = jnp.zeros((64 - ch, KV_W), BF16)
        kbuf[ATT_PAD + WINDOW + tq:ATT_PAD + WINDOW + tq + 64 - ch, :] = zc
        vbuf[ATT_PAD + WINDOW + tq:ATT_PAD + WINDOW + tq + 64 - ch, :] = zc

    rows = A_GROUP * ch
    col = lax.broadcasted_iota(jnp.int32, (rows, ATT_WIN), 1)
    row = lax.broadcasted_iota(jnp.int32, (rows, 1), 0)
    lane = lax.broadcasted_iota(jnp.int32, (ch, LANE), 1)
    lo = lane < HEAD_DIM

    for i in range(nq):
        c = chunk0 + j * nq + i
        first_band = ATT_PAD + 64 * jnp.maximum(2 - c, 0)
        valid = (col >= first_band) & (col < ATT_PAD + WINDOW + ch)
        if use_meta:
            valid = valid | (col < N_META)
        kcat = jnp.concatenate(
            [kbuf[0:ATT_PAD, :], kbuf[ATT_PAD + ch * i:ATT_PAD + ch * i + WINDOW + 64, :]], axis=0)
        vcat = jnp.concatenate(
            [vbuf[0:ATT_PAD, :], vbuf[ATT_PAD + ch * i:ATT_PAD + ch * i + WINDOW + 64, :]], axis=0)
        qi = q_ref[ch * i:ch * (i + 1), :] * (HEAD_DIM ** -0.5)
        tiles = [qi[:, LANE * t:LANE * (t + 1)] for t in range(A_HEADS // 2)]
        out_tiles = [None] * (A_HEADS // 2)
        for kh in range(A_KV_HEADS):
            qs = []
            for r in range(A_GROUP):
                h = A_GROUP * kh + r
                t = tiles[h // 2]
                if h % 2 != kh:
                    t = pltpu.roll(t, HEAD_DIM, axis=1)
                keep = lo if kh == 0 else jnp.logical_not(lo)
                qs.append(jnp.where(keep, t, 0.0).astype(BF16))
            qz = jnp.concatenate(qs, axis=0)
            s = _dot_nt(qz, kcat)
            s = jnp.where(valid, s, -jnp.inf)
            sink = jnp.zeros((rows, 1), F32)
            for r in range(A_GROUP):
                sink = jnp.where((row >= r * ch) & (row < (r + 1) * ch),
                                 sink_ref[A_GROUP * kh + r], sink)
            m = jnp.maximum(jnp.max(s, axis=-1, keepdims=True), sink)
            p = jnp.exp(s - m)
            den = jnp.sum(p, axis=-1, keepdims=True) + jnp.exp(sink - m)
            probs = (p / den).astype(BF16)
            o = _dot(probs, vcat)
            for r in range(A_GROUP):
                h = A_GROUP * kh + r
                oh = o[r * ch:(r + 1) * ch, :]
                if h % 2 != kh:
                    oh = pltpu.roll(oh, HEAD_DIM, axis=1)
                keep = lo if h % 2 == 0 else jnp.logical_not(lo)
                prev = out_tiles[h // 2]
                out_tiles[h // 2] = jnp.where(keep, oh, 0.0 if prev is None else prev)
        o_ref[ch * i:ch * (i + 1), :] = jnp.concatenate(out_tiles, axis=1)


def attn_call(sinks, q_src, kc_src, kp_src, km_src, *, nb, length, ch, nq, use_meta, chunk0,
              q_row0, kp_map, km_map, kp_cols, km_cols, mix, mix_rows):
    tq = ch * nq
    nj = length // tq
    qb0 = q_row0 // tq
    kern = functools.partial(_attn_kernel, ch=ch, nq=nq, use_meta=use_meta, chunk0=chunk0)
    in_specs = [
        pl.BlockSpec(memory_space=pltpu.SMEM),
        pl.BlockSpec((tq, GROUP_W), lambda b, j: (qb0 + b * nj + j, C_Q // GROUP_W)),
        pl.BlockSpec((tq, KV_W), lambda b, j: (qb0 + b * nj + j, C_K // KV_W)),
        pl.BlockSpec((tq, KV_W), lambda b, j: (qb0 + b * nj + j, C_V // KV_W)),
        pl.BlockSpec((WINDOW, KV_W), lambda b, j: (kp_map(b, j), kp_cols[0])),
        pl.BlockSpec((WINDOW, KV_W), lambda b, j: (kp_map(b, j), kp_cols[1])),
        pl.BlockSpec((N_META, KV_W), lambda b, j: (km_map(b, j), km_cols[0])),
        pl.BlockSpec((N_META, KV_W), lambda b, j: (km_map(b, j), km_cols[1])),
    ]
    return _mixer_call(
        kern, (nb, nj), in_specs,
        (sinks, q_src, kc_src, kc_src, kp_src[0], kp_src[1], km_src[0], km_src[1]),
        pl.BlockSpec((tq, GROUP_W), lambda b, j: (qb0 + b * nj + j, 0)), [], [],
        [pltpu.VMEM((ATT_PAD + WINDOW + tq + 64, KV_W), BF16),
         pltpu.VMEM((ATT_PAD + WINDOW + tq + 64, KV_W), BF16)],
        mix, mix_rows, "attn")[0]


def _pool_kernel(u_ref, prev_ref, w_ref, scale_ref, o_ref, buf, *, tb, ramp):
    j = pl.program_id(1)

    @pl.when(j == 0)
    def _():
        buf[0:16, :] = prev_ref[...]

    buf[16:16 + tb, :] = u_ref[...]
    pos = j * tb + lax.broadcasted_iota(jnp.int32, (tb, 1), 0)
    outs = []
    for g, win in enumerate(POOL_SIZES):
        sl = slice(g * POOL_GW, (g + 1) * POOL_GW)
        tot = buf[16:16 + tb, sl]
        for k in range(1, win):
            tot = tot + buf[16 - k:16 - k + tb, sl]
        if ramp:
            cnt = jnp.minimum(win, pos + 1).astype(F32)
            mean = tot / cnt
        else:
            mean = tot * (1.0 / win)
        d = mean - buf[16:16 + tb, sl]
        outs.append(_dot(d.astype(BF16), w_ref[g]))
    o_ref[...] = jnp.concatenate(outs, axis=1) * scale_ref[...]
    buf[0:16, :] = buf[tb:tb + 16, :]


def pool_call(proj, prev_src, prev_map, prev_col, w_bf, scale, *, nb, length, tb, row0, ramp,
              mix, mix_rows):
    nj = length // tb
    rb0 = row0 // tb
    kern = functools.partial(_pool_kernel, tb=tb, ramp=ramp)
    in_specs = [
        pl.BlockSpec((tb, GROUP_W), lambda b, j: (rb0 + b * nj + j, C_U // GROUP_W)),
        pl.BlockSpec((16, GROUP_W), lambda b, j: (prev_map(b), prev_col)),
        _full((4, POOL_GW, POOL_GW)),
        _full((1, GROUP_W)),
    ]
    return _mixer_call(
        kern, (nb, nj), in_specs, (proj, prev_src, w_bf, scale),
        pl.BlockSpec((tb, GROUP_W), lambda b, j: (rb0 + b * nj + j, 1)), [], [],
        [pltpu.VMEM((tb + 16, GROUP_W), F32)], mix, mix_rows, "pool")[0]


def _conv_block(buf, x_ref, w_ref, b_ref, tb):
    buf[8:8 + tb, :] = x_ref[...]
    acc = b_ref[...] + buf[5:5 + tb, :] * w_ref[0:1, :]
    for k in range(1, CONV_W):
        acc = acc + buf[5 + k:5 + k + tb, :] * w_ref[k:k + 1, :]
    return acc


def _conv_carry(buf, tb):
    buf[0:8, :] = buf[tb:tb + 8, :]


def _col(x, h):
    lane = lax.broadcasted_iota(jnp.int32, x.shape, 1)
    return jnp.sum(jnp.where(lane == h, x, 0.0), axis=1, keepdims=True)


def _ssd_kernel(xs_ref, bm_ref, cm_ref, dt_ref, z_ref, px_ref, pb_ref, pc_ref, h0_ref,
                wx_ref, wb_ref, wc_ref, bx_ref, bb_ref, bc_ref,
                dtb_ref, alog_ref, dskip_ref, ng_ref,
                y_ref, hout_ref, bufx, bufb, bufc, hst, *, q):
    j = pl.program_id(1)
    nj = pl.num_programs(1)
    hi = lax.Precision.HIGHEST

    @pl.when(j == 0)
    def _():
        bufx[0:8, :] = px_ref[...]
        bufb[0:8, :] = pb_ref[...]
        bufc[0:8, :] = pc_ref[...]
        hst[...] = h0_ref[...]

    xs = jax.nn.silu(_conv_block(bufx, xs_ref, wx_ref, bx_ref, q))
    bm = jax.nn.silu(_conv_block(bufb, bm_ref, wb_ref, bb_ref, q))
    cm = jax.nn.silu(_conv_block(bufc, cm_ref, wc_ref, bc_ref, q))
    _conv_carry(bufx, q)
    _conv_carry(bufb, q)
    _conv_carry(bufc, q)

    lane1 = lax.broadcasted_iota(jnp.int32, (1, LANE), 1)
    hmask = lane1 < C_HEADS
    dt = jnp.where(hmask, jax.nn.softplus(dt_ref[...] + dtb_ref[...]), 0.0)
    a = jnp.where(hmask, -jnp.exp(alog_ref[...]), 0.0)
    dta = dt * a
    ri = lax.broadcasted_iota(jnp.int32, (q, q), 0)
    ci = lax.broadcasted_iota(jnp.int32, (q, q), 1)
    tri = ri >= ci
    cum = jnp.dot(tri.astype(F32), dta, precision=hi, preferred_element_type=F32)
    eye = (lax.broadcasted_iota(jnp.int32, (LANE, LANE), 0)
           == lax.broadcasted_iota(jnp.int32, (LANE, LANE), 1)).astype(F32)
    cum_t = _dot_nt(eye, cum, precision=hi)
    dt_t = _dot_nt(eye, dt, precision=hi)
    ecum = jnp.exp(cum)
    cum_last = cum[q - 1:q, :]
    te = jnp.exp(cum_last - cum) * dt

    lane = lax.broadcasted_iota(jnp.int32, (q, LANE), 1)
    lo = lane < 64
    bm_bf = bm.astype(BF16)
    cm_bf = cm.astype(BF16)
    cb = [_dot_nt(cm_bf[:, LANE * g:LANE * (g + 1)], bm_bf[:, LANE * g:LANE * (g + 1)])
          for g in range(2)]
    yoff = [_dot_nt(cm_bf[:, LANE * g:LANE * (g + 1)], hst[256 * g:256 * (g + 1), :].astype(BF16))
            for g in range(2)]

    y_tiles = []
    xw_tiles = []
    for k in range(C_HEADS // 2):
        g = k // 2
        x_pair = xs[:, LANE * k:LANE * (k + 1)]
        ydiag = None
        for par in range(2):
            h = 2 * k + par
            seg = _col(cum, h) - cum_t[h:h + 1, :]
            lm = jnp.exp(jnp.where(tri, seg, -jnp.inf))
            mm = (cb[g] * lm * dt_t[h:h + 1, :]).astype(BF16)
            xm = jnp.where(lo if par == 0 else jnp.logical_not(lo), x_pair, 0.0).astype(BF16)
            part = _dot(mm, xm)
            ydiag = part if ydiag is None else ydiag + part
        e_pair = jnp.where(lo, _col(ecum, 2 * k), _col(ecum, 2 * k + 1))
        te_pair = jnp.where(lo, _col(te, 2 * k), _col(te, 2 * k + 1))
        kk = k % 2
        y_tiles.append(ydiag + yoff[g][:, LANE * kk:LANE * (kk + 1)] * e_pair
                       + dskip_ref[:, LANE * k:LANE * (k + 1)] * x_pair)
        xw_tiles.append((x_pair * te_pair).astype(BF16))

    eye2 = (lax.broadcasted_iota(jnp.int32, (256, 256), 0)
            == lax.broadcasted_iota(jnp.int32, (256, 256), 1)).astype(BF16)
    for g in range(2):
        xw = jnp.concatenate(xw_tiles[2 * g:2 * g + 2], axis=1)
        xw_t = _dot_nt(eye2, xw).astype(BF16)
        s_new = _dot(xw_t, bm_bf[:, LANE * g:LANE * (g + 1)])
        dec = jnp.concatenate(
            [jnp.broadcast_to(jnp.exp(cum_t[4 * g + r:4 * g + r + 1, q - 1:q]), (64, LANE))
             for r in range(4)], axis=0)
        hst[256 * g:256 * (g + 1), :] = dec * hst[256 * g:256 * (g + 1), :] + s_new

    y = jnp.concatenate(y_tiles, axis=1) * jax.nn.silu(z_ref[...])
    y = y * lax.rsqrt(jnp.mean(y * y, axis=-1, keepdims=True) + 1e-6) * ng_ref[...]
    y_ref[...] = y

    @pl.when(j == nj - 1)
    def _():
        hout_ref[...] = hst[...]


def ssd_call(proj, prev_src, prev_map, prev_cols, h0, lw, *, nb, length, q, row0, mix, mix_rows):
    nj = length // q
    rb0 = row0 // q
    kern = functools.partial(_ssd_kernel, q=q)
    blk = lambda width, col: pl.BlockSpec((q, width), lambda b, j: (rb0 + b * nj + j, col // width))
    pblk = lambda width, col: pl.BlockSpec((8, width), lambda b, j: (prev_map(b), col))
    in_specs = [
        blk(512, C_XS), blk(256, C_B), blk(256, C_C), blk(LANE, C_DT), blk(512, C_Z),
        pblk(512, prev_cols[0]), pblk(256, prev_cols[1]), pblk(256, prev_cols[2]),
        pl.BlockSpec((None, 512, C_STATE), lambda b, j: (b, 0, 0)),
        _full((CONV_W, 512)), _full((CONV_W, 256)), _full((CONV_W, 256)),
        _full((1, 512)), _full((1, 256)), _full((1, 256)),
        _full((1, LANE)), _full((1, LANE)), _full((1, 512)), _full((1, 512)),
    ]
    args = (proj, proj, proj, proj, proj, prev_src, prev_src, prev_src, h0,
            lw['cw_x'], lw['cw_b'], lw['cw_c'], lw['cb_x'], lw['cb_b'], lw['cb_c'],
            lw['dt_bias'], lw['a_log'], lw['d_skip'], lw['norm_g'])
    return _mixer_call(
        kern, (nb, nj), in_specs, args,
        pl.BlockSpec((q, GROUP_W), lambda b, j: (rb0 + b * nj + j, 2)),
        [pl.BlockSpec((None, 512, C_STATE), lambda b, j: (b, 0, 0))],
        [jax.ShapeDtypeStruct((nb, 512, C_STATE), F32)],
        [pltpu.VMEM((q + 8, 512), F32), pltpu.VMEM((q + 8, 256), F32),
         pltpu.VMEM((q + 8, 256), F32), pltpu.VMEM((512, C_STATE), F32)],
        mix, mix_rows, "ssd")


def _lru_kernel(rx_ref, rg_ref, prev_ref, h0_ref, cw_ref, cb_ref, wr_ref, wi_ref,
                br_ref, bi_ref, lam_ref, y_ref, hout_ref, buf, hc, *, tb):
    j = pl.program_id(1)
    nj = pl.num_programs(1)

    @pl.when(j == 0)
    def _():
        buf[0:8, :] = prev_ref[...]
        hc[...] = jnp.broadcast_to(h0_ref[...], hc.shape)

    xc = _conv_block(buf, rx_ref, cw_ref, cb_ref, tb)
    _conv_carry(buf, tb)
    rs, gs = [], []
    for s in range(GROUP_W // LANE):
        xb = xc[:, LANE * s:LANE * (s + 1)].astype(BF16)
        rs.append(_dot(xb, wr_ref[s]))
        gs.append(_dot(xb, wi_ref[s]))
    r = jax.nn.sigmoid(jnp.concatenate(rs, axis=1) + br_ref[...])
    gi = jax.nn.sigmoid(jnp.concatenate(gs, axis=1) + bi_ref[...])
    log_a = -LRU_C * r * jax.nn.softplus(-lam_ref[...])
    a = jnp.exp(log_a)
    u = jnp.sqrt(jnp.maximum(1.0 - jnp.exp(2.0 * log_a), 0.0)) * (gi * xc)
    t = lax.broadcasted_iota(jnp.int32, (tb, 1), 0)
    d = 1
    while d < tb:
        a_sh = jnp.where(t >= d, pltpu.roll(a, d, axis=0), 1.0)
        u_sh = jnp.where(t >= d, pltpu.roll(u, d, axis=0), 0.0)
        u = a * u_sh + u
        a = a * a_sh
        d *= 2
    h = u + a * hc[0:1, :]
    y_ref[...] = h * jax.nn.gelu(rg_ref[...])
    hc[...] = jnp.broadcast_to(h[tb - 1:tb, :], hc.shape)

    @pl.when(j == nj - 1)
    def _():
        hout_ref[...] = h[tb - 1:tb, :]


def lru_call(proj, prev_src, prev_map, prev_col, h0, lw, *, nb, length, tb, row0, mix, mix_rows):
    nj = length // tb
    rb0 = row0 // tb
    kern = functools.partial(_lru_kernel, tb=tb)
    in_specs = [
        pl.BlockSpec((tb, GROUP_W), lambda b, j: (rb0 + b * nj + j, C_RX // GROUP_W)),
        pl.BlockSpec((tb, GROUP_W), lambda b, j: (rb0 + b * nj + j, C_RG // GROUP_W)),
        pl.BlockSpec((8, GROUP_W), lambda b, j: (prev_map(b), prev_col)),
        pl.BlockSpec((None, 1, GROUP_W), lambda b, j: (b, 0, 0)),
        _full((CONV_W, GROUP_W)), _full((1, GROUP_W)),
        _full((4, LANE, LANE)), _full((4, LANE, LANE)),
        _full((1, GROUP_W)), _full((1, GROUP_W)), _full((1, GROUP_W)),
    ]
    args = (proj, proj, prev_src, h0, lw['cw'], lw['cb'], lw['wr'], lw['wi'],
            lw['br'], lw['bi'], lw['lam'])
    return _mixer_call(
        kern, (nb, nj), in_specs, args,
        pl.BlockSpec((tb, GROUP_W), lambda b, j: (rb0 + b * nj + j, 3)),
        [pl.BlockSpec((None, 1, GROUP_W), lambda b, j: (b, 0, 0))],
        [jax.ShapeDtypeStruct((nb, 1, GROUP_W), F32)],
        [pltpu.VMEM((tb + 8, GROUP_W), F32), pltpu.VMEM((8, GROUP_W), F32)],
        mix, mix_rows, "lru")


HALF = D_MODEL // 2


def _pack_bf16_pairs(x):
    bits = lax.bitcast_convert_type(x.astype(BF16).astype(F32), jnp.uint32)
    return (bits[:, :HALF] >> 16) | (bits[:, HALF:] & jnp.uint32(0xFFFF0000))


def _unpack_pairs_f32(w):
    lo = lax.bitcast_convert_type(w << 16, F32)
    hi = lax.bitcast_convert_type(w & jnp.uint32(0xFFFF0000), F32)
    return lo, hi


def _unpack_bf16_pairs(w):
    lo, hi = _unpack_pairs_f32(w)
    return lo.astype(BF16), hi.astype(BF16)


DMA_UNROLL = 8

ROW_TILE = HALF // LANE


def _store_tile_rows(ref, row0, mat):
    n = mat.shape[0]
    for j in range(ROW_TILE):
        ref[pl.ds(row0 * ROW_TILE + j, n, stride=ROW_TILE), :] = mat[:, LANE * j:LANE * (j + 1)]


def _load_tile_cols(ref, n, j):
    return ref[pl.ds(j, n, stride=ROW_TILE), :]


OUT_SUB = 256


def _out_proj_kernel(mix_ref, h_ref, w_ref, g_ref, b_ref, rw_ref, x1_ref, xp_ref, sc_ref):
    for r in range(TM // OUT_SUB):
        rows = slice(r * OUT_SUB, (r + 1) * OUT_SUB)
        y = _dot(mix_ref[rows, :].astype(BF16), w_ref[...])
        x1 = _ln(ALPHA * h_ref[rows, :] + y, g_ref[...], b_ref[...])
        x1_ref[rows, :] = x1
        _store_tile_rows(xp_ref, r * OUT_SUB, _pack_bf16_pairs(x1))
        logits = lax.dot_general(x1, rw_ref[...], (((1,), (0,)), ((), ())),
                                 preferred_element_type=F32)
        sc_ref[rows, :] = jax.nn.sigmoid(logits)


def out_proj(mix, h, w_bf, g, b, rw):
    return pl.pallas_call(
        _out_proj_kernel,
        grid=(N_TILES,),
        in_specs=[
            pl.BlockSpec((TM, D_MODEL), lambda i: (i, 0)),
            pl.BlockSpec((TM, D_MODEL), lambda i: (i, 0)),
            _full((D_MODEL, D_MODEL)),
            _full((1, D_MODEL)), _full((1, D_MODEL)),
            _full((D_MODEL, LANE)),
        ],
        out_specs=[pl.BlockSpec((TM, D_MODEL), lambda i: (i, 0)),
                   pl.BlockSpec((TM * ROW_TILE, LANE), lambda i: (i, 0)),
                   pl.BlockSpec((TM, LANE), lambda i: (i, 0))],
        out_shape=[jax.ShapeDtypeStruct((NT, D_MODEL), F32),
                   jax.ShapeDtypeStruct((NT * ROW_TILE, LANE), jnp.uint32),
                   jax.ShapeDtypeStruct((NT, LANE), F32)],
        compiler_params=_cparams(("arbitrary",)),
        name="out_proj",
    )(mix, h, w_bf, g, b, rw)


_BIG = 4096


def _group_allreduce(x, lane, op):
    for sh in (1, 2, 4):
        up = pltpu.roll(x, sh, axis=1)
        dn = pltpu.roll(x, LANE - sh, axis=1)
        x = op(x, jnp.where((lane & sh) != 0, up, dn))
    return x


def _router_kernel(sc_ref, bias_ref, e_ref, r_ref, w_ref, cnt_ref, run):
    i = pl.program_id(0)

    @pl.when(i == 0)
    def _():
        run[...] = jnp.zeros_like(run)

    sc = sc_ref[...]
    lane = lax.broadcasted_iota(jnp.int32, (TM, LANE), 1)
    valid = lane < N_EXPERTS
    neg = -jnp.inf
    biased = jnp.where(valid, sc + bias_ref[...], neg)
    gmax = _group_allreduce(biased, lane, jnp.maximum)
    first = _group_allreduce(jnp.where(biased == gmax, lane, _BIG), lane, jnp.minimum)
    second = _group_allreduce(jnp.where(lane == first, neg, biased), lane, jnp.maximum)
    gs = jnp.where(valid, gmax + second, neg)
    grp = lane >> 3
    cand = jnp.full((TM, LANE), neg, F32)
    for _ in range(TOPK_GROUP):
        m = jnp.max(gs, axis=1, keepdims=True)
        g1 = jnp.min(jnp.where(gs == m, grp, _BIG), axis=1, keepdims=True)
        hit = grp == g1
        cand = jnp.where(hit, biased, cand)
        gs = jnp.where(hit, neg, gs)
    sel = jnp.zeros((TM, LANE), F32)
    e_out = jnp.zeros((TM, LANE), jnp.int32)
    w_out = jnp.zeros((TM, LANE), F32)
    idxs = []
    for k in range(TOP_K):
        m = jnp.max(cand, axis=1, keepdims=True)
        ik = jnp.min(jnp.where(cand == m, lane, _BIG), axis=1, keepdims=True)
        hit = lane == ik
        vk = jnp.sum(jnp.where(hit, sc, 0.0), axis=1, keepdims=True)
        sel = jnp.where(hit, 1.0, sel)
        cand = jnp.where(hit, neg, cand)
        e_out = jnp.where(lane == k, ik, e_out)
        w_out = jnp.where(lane == k, vk, w_out)
        idxs.append(ik)
    wsum = jnp.sum(w_out, axis=1, keepdims=True)
    w_ref[...] = w_out / wsum * ROUTE_SCALE
    e_ref[...] = e_out
    rowi = lax.broadcasted_iota(jnp.int32, (TM, 1), 0)
    sel = jnp.where(rowi < NT - i * TM, sel, 0.0)
    ri = lax.broadcasted_iota(jnp.int32, (TM, TM), 0)
    ci = lax.broadcasted_iota(jnp.int32, (TM, TM), 1)
    before = _dot((ri > ci).astype(BF16), sel.astype(BF16))
    rank = run[0:1, :] + before
    r_out = jnp.zeros((TM, LANE), F32)
    for k in range(TOP_K):
        rk = jnp.sum(jnp.where(lane == idxs[k], rank, 0.0), axis=1, keepdims=True)
        r_out = jnp.where(lane == k, rk, r_out)
    r_ref[...] = r_out.astype(jnp.int32)
    run[...] = jnp.broadcast_to(rank[TM - 1:TM, :] + sel[TM - 1:TM, :], run.shape)

    @pl.when(i == pl.num_programs(0) - 1)
    def _():
        cnt_ref[...] = run[...]


def router(scores, bias_row):
    return pl.pallas_call(
        _router_kernel,
        grid=(N_TILES,),
        in_specs=[pl.BlockSpec((TM, LANE), lambda i: (i, 0)), _full((1, LANE))],
        out_specs=[pl.BlockSpec((TM, LANE), lambda i: (i, 0)),
                   pl.BlockSpec((TM, LANE), lambda i: (i, 0)),
                   pl.BlockSpec((TM, LANE), lambda i: (i, 0)),
                   _full((8, LANE))],
        out_shape=[jax.ShapeDtypeStruct((NT, LANE), jnp.int32),
                   jax.ShapeDtypeStruct((NT, LANE), jnp.int32),
                   jax.ShapeDtypeStruct((NT, LANE), F32),
                   jax.ShapeDtypeStruct((8, LANE), F32)],
        scratch_shapes=[pltpu.VMEM((8, LANE), F32)],
        compiler_params=_cparams(("arbitrary",)),
        name="router",
    )(scores, bias_row)


def _dispatch_kernel(cnt_ref, pst_ref, dest_ref, x_ref, xg_hbm, zbuf, sem, zsem):
    i = pl.program_id(0)

    def tile_rows(row):
        return pl.ds(pl.multiple_of(row * ROW_TILE, ROW_TILE), ROW_TILE)

    def row_copy(src_row, dst_tile_row):
        dst = pl.ds(pl.multiple_of(dst_tile_row, ROW_TILE), ROW_TILE)
        return pltpu.make_async_copy(x_ref.at[tile_rows(src_row)], xg_hbm.at[dst], sem)

    def zero_copy(dst_row):
        return pltpu.make_async_copy(zbuf, xg_hbm.at[tile_rows(dst_row)], zsem)

    @pl.when(i == 0)
    def _():
        zbuf[...] = jnp.zeros_like(zbuf)

        def per_expert(e, carry):
            c = cnt_ref[e]
            npad = (MOE_T - c % MOE_T) % MOE_T
            base = pst_ref[e] + c

            def start(r, cc):
                zero_copy(base + r).start()
                return cc

            lax.fori_loop(0, npad, start, 0)

            def wait(r, cc):
                zero_copy(base + r).wait()
                return cc

            lax.fori_loop(0, npad, wait, 0)
            return carry

        lax.fori_loop(0, N_EXPERTS, per_expert, 0)

    def start(t, carry):
        for s in range(TOP_K):
            row_copy(t, dest_ref[t * TOP_K + s]).start(priority=s % 2)
        return carry

    lax.fori_loop(0, TM, start, 0, unroll=DMA_UNROLL)

    for s in range(TOP_K):
        pltpu.make_async_copy(x_ref, x_ref, sem).wait()


def dispatch(counts, pstarts, dest_flat, x1p):
    grid_spec = pltpu.PrefetchScalarGridSpec(
        num_scalar_prefetch=2,
        grid=(N_TILES,),
        in_specs=[
            pl.BlockSpec((TM * TOP_K,), lambda i, c, p: (i,), memory_space=pltpu.SMEM),
            pl.BlockSpec((TM * ROW_TILE, LANE), lambda i, c, p: (i, 0)),
        ],
        out_specs=pl.BlockSpec(memory_space=pl.ANY),
        scratch_shapes=[pltpu.VMEM((ROW_TILE, LANE), jnp.uint32),
                        pltpu.SemaphoreType.DMA(()), pltpu.SemaphoreType.DMA(())],
    )
    return pl.pallas_call(
        _dispatch_kernel,
        grid_spec=grid_spec,
        out_shape=jax.ShapeDtypeStruct(((MOE_NB * MOE_T + N_TRASH) * ROW_TILE, LANE), jnp.uint32),
        compiler_params=_cparams(("arbitrary",)),
        name="dispatch",
    )(counts, pstarts, dest_flat, x1p)


def _moe_kernel(blk_e_ref, nused_ref, next_e_ref, x_ref, wg_hbm, wu_hbm, wd_hbm, o_ref,
                wg_st, wu_st, wd_st, wg_bf, wu_bf, wd_bf, sems, *, layer):
    i = pl.program_id(0)
    changed = jnp.logical_or(i == 0, blk_e_ref[i] != blk_e_ref[jnp.maximum(i - 1, 0)])
    active = i < nused_ref[0]

    def fetch(e):
        return (pltpu.make_async_copy(wg_hbm.at[layer, e], wg_st, sems.at[0]),
                pltpu.make_async_copy(wu_hbm.at[layer, e], wu_st, sems.at[1]),
                pltpu.make_async_copy(wd_hbm.at[layer, e], wd_st, sems.at[2]))

    @pl.when(i == 0)
    def _():
        for c in fetch(blk_e_ref[0]):
            c.start()

    @pl.when(jnp.logical_and(changed, active))
    def _():
        for c in fetch(blk_e_ref[i]):
            c.wait()
        wg_bf[...] = wg_st[...].astype(BF16)
        wu_bf[...] = wu_st[...].astype(BF16)
        wd_bf[...] = wd_st[...].astype(BF16)

        @pl.when(next_e_ref[i] >= 0)
        def _():
            for c in fetch(next_e_ref[i]):
                c.start()

    @pl.when(active)
    def _():
        halves = [_unpack_bf16_pairs(_load_tile_cols(x_ref, MOE_T, j)) for j in range(ROW_TILE)]
        lo = jnp.concatenate([p[0] for p in halves], axis=1)
        hi = jnp.concatenate([p[1] for p in halves], axis=1)
        gate = _dot(lo, wg_bf[0:HALF, :]) + _dot(hi, wg_bf[HALF:D_MODEL, :])
        up = _dot(lo, wu_bf[0:HALF, :]) + _dot(hi, wu_bf[HALF:D_MODEL, :])
        hb = jax.nn.silu(gate) * up
        _store_tile_rows(o_ref, 0, _pack_bf16_pairs(_dot(hb.astype(BF16), wd_bf[...])))

    @pl.when(i >= nused_ref[0])
    def _():
        o_ref[...] = jnp.zeros_like(o_ref)


def moe_experts(blk_e, nused, next_e, xg, wg, wu, wd, layer):
    grid_spec = pltpu.PrefetchScalarGridSpec(
        num_scalar_prefetch=3,
        grid=(MOE_NB,),
        in_specs=[
            pl.BlockSpec((MOE_T * ROW_TILE, LANE),
                         lambda i, be, nu, ne: (jnp.minimum(i, nu[0] - 1), 0)),
            pl.BlockSpec(memory_space=pl.ANY),
            pl.BlockSpec(memory_space=pl.ANY),
            pl.BlockSpec(memory_space=pl.ANY),
        ],
        out_specs=pl.BlockSpec((MOE_T * ROW_TILE, LANE), lambda i, be, nu, ne: (i, 0)),
        scratch_shapes=[pltpu.VMEM((D_MODEL, D_EXPERT), F32),
                        pltpu.VMEM((D_MODEL, D_EXPERT), F32),
                        pltpu.VMEM((D_EXPERT, D_MODEL), F32),
                        pltpu.VMEM((D_MODEL, D_EXPERT), BF16),
                        pltpu.VMEM((D_MODEL, D_EXPERT), BF16),
                        pltpu.VMEM((D_EXPERT, D_MODEL), BF16),
                        pltpu.SemaphoreType.DMA((3,))],
    )
    return pl.pallas_call(
        functools.partial(_moe_kernel, layer=layer),
        grid_spec=grid_spec,
        out_shape=jax.ShapeDtypeStruct((MOE_NB * MOE_T * ROW_TILE, LANE), jnp.uint32),
        compiler_params=_cparams(("arbitrary",)),
        name="moe_experts",
    )(blk_e, nused, next_e, xg, wg, wu, wd)


TMC = 256
NC_TILES = -(-NT // TMC)
DEST_LEN = max(N_TILES * TM, NC_TILES * TMC) * TOP_K
N_TRASH = DEST_LEN - NK


def _combine_kernel(dest_ref, dest_next_ref, x1_ref, w_ref, yb_hbm, wg_ref, wu_ref, wd_ref,
                    g_ref, b_ref, o_ref, ybuf, sems):
    i = pl.program_id(0)
    slot = lax.rem(i, 2)

    def issue(d_ref, buf_slot):
        def start(t, carry):
            for s in range(TOP_K):
                src = pl.ds(pl.multiple_of(d_ref[t * TOP_K + s], ROW_TILE), ROW_TILE)
                dst = pl.ds(pl.multiple_of(t * ROW_TILE, ROW_TILE), ROW_TILE)
                pltpu.make_async_copy(yb_hbm.at[src], ybuf.at[buf_slot, s, dst],
                                      sems.at[buf_slot]).start(priority=s % 2)
            return carry

        lax.fori_loop(0, TMC, start, 0, unroll=DMA_UNROLL)

    @pl.when(i == 0)
    def _():
        issue(dest_ref, 0)

    @pl.when(i + 1 < pl.num_programs(0))
    def _():
        issue(dest_next_ref, 1 - slot)

    x1 = x1_ref[...]
    xb = x1.astype(BF16)
    hb = jax.nn.silu(_dot(xb, wg_ref[...])) * _dot(xb, wu_ref[...])
    shared = _dot(hb.astype(BF16), wd_ref[...])

    pltpu.make_async_copy(ybuf.at[slot], ybuf.at[slot], sems.at[slot]).wait()

    w = w_ref[...]
    acc_lo = [shared[:, LANE * j:LANE * (j + 1)] for j in range(ROW_TILE)]
    acc_hi = [shared[:, HALF + LANE * j:HALF + LANE * (j + 1)] for j in range(ROW_TILE)]
    for s in range(TOP_K):
        ws = _col(w, s)
        for j in range(ROW_TILE):
            lo, hi = _unpack_pairs_f32(_load_tile_cols(ybuf.at[slot, s], TMC, j))
            acc_lo[j] = acc_lo[j] + ws * lo
            acc_hi[j] = acc_hi[j] + ws * hi
    acc = jnp.concatenate(acc_lo + acc_hi, axis=1)
    o_ref[...] = _ln(ALPHA * x1 + acc, g_ref[...], b_ref[...])


def combine_shared_ln2(dest_flat, x1, w, yb, wg, wu, wd, g, b):
    grid_spec = pl.GridSpec(
        grid=(NC_TILES,),
        in_specs=[
            pl.BlockSpec((TMC * TOP_K,), lambda i: (i,), memory_space=pltpu.SMEM),
            pl.BlockSpec((TMC * TOP_K,), lambda i: (jnp.minimum(i + 1, NC_TILES - 1),),
                         memory_space=pltpu.SMEM),
            pl.BlockSpec((TMC, D_MODEL), lambda i: (i, 0)),
            pl.BlockSpec((TMC, LANE), lambda i: (i, 0)),
            pl.BlockSpec(memory_space=pl.ANY),
            _full((D_MODEL, D_EXPERT)), _full((D_MODEL, D_EXPERT)), _full((D_EXPERT, D_MODEL)),
            _full((1, D_MODEL)), _full((1, D_MODEL)),
        ],
        out_specs=pl.BlockSpec((TMC, D_MODEL), lambda i: (i, 0)),
        scratch_shapes=[pltpu.VMEM((2, TOP_K, TMC * ROW_TILE, LANE), jnp.uint32),
                        pltpu.SemaphoreType.DMA((2,))],
    )
    return pl.pallas_call(
        _combine_kernel,
        grid_spec=grid_spec,
        out_shape=jax.ShapeDtypeStruct((NT, D_MODEL), F32),
        compiler_params=_cparams(("arbitrary",)),
        name="combine_shared_ln2",
    )(dest_flat, dest_flat, x1, w, yb, wg, wu, wd, g, b)


def _schedule(counts, e_sel, rank_sel):
    padded = (counts + MOE_T - 1) // MOE_T * MOE_T
    pends = jnp.cumsum(padded)
    pstarts = pends - padded
    onehot = e_sel[:, :, None] == jnp.arange(N_EXPERTS, dtype=jnp.int32)
    dest = rank_sel + jnp.sum(jnp.where(onehot, pstarts, 0), axis=-1)
    dest_flat = dest.reshape(NK).astype(jnp.int32) * ROW_TILE
    trash = (MOE_NB * MOE_T + jnp.arange(N_TRASH, dtype=jnp.int32)) * ROW_TILE
    dest_disp = jnp.concatenate([dest_flat, trash])
    dest_comb = jnp.pad(dest_flat, (0, N_TRASH))
    blk_row0 = jnp.arange(MOE_NB, dtype=jnp.int32) * MOE_T
    experts_ending_by = lambda row: jnp.sum((pends[None, :] <= row[:, None]).astype(jnp.int32), axis=1)
    blk_e = jnp.minimum(experts_ending_by(blk_row0), N_EXPERTS - 1).astype(jnp.int32)
    nused = (pends[-1] // MOE_T).astype(jnp.int32).reshape(1)
    group_end = jnp.sum(jnp.where(blk_e[:, None] == jnp.arange(N_EXPERTS, dtype=jnp.int32), pends, 0), axis=1)
    next_e = jnp.where(group_end < pends[-1],
                       jnp.minimum(experts_ending_by(group_end), N_EXPERTS - 1), -1).astype(jnp.int32)
    return pstarts.astype(jnp.int32), dest_disp, dest_comb, blk_e, nused, next_e


def _pad_rows(x, rows):
    b, r, c = x.shape
    return jnp.pad(x, ((0, 0), (rows - r, 0), (0, 0))).reshape(b * rows, c)


def _layer(h, st, lw):
    proj = in_proj(h, lw['w_in'])
    sinks = lw['sinks']
    zeros16 = jnp.zeros((BATCH * 16, GROUP_W), F32)
    zeros8 = jnp.zeros((BATCH * 8, C_CONV_CH), F32)

    meta_blk = lambda b, j=None: M_ROW0 // N_META + b
    mix = attn_call(sinks, proj, proj, (proj, proj), (proj, proj), nb=BATCH, length=N_META,
                    ch=N_META, nq=1, use_meta=False, chunk0=0, q_row0=M_ROW0,
                    kp_map=lambda b, j: 0, km_map=meta_blk,
                    kp_cols=(C_K // KV_W, C_V // KV_W), km_cols=(C_K // KV_W, C_V // KV_W),
                    mix=None, mix_rows=NT)
    nq_p = 4
    mix = attn_call(sinks, proj, proj, (proj, proj), (proj, proj), nb=BATCH, length=SEQ,
                    ch=64, nq=nq_p, use_meta=True, chunk0=0, q_row0=0,
                    kp_map=lambda b, j: jnp.maximum(b * (SEQ // WINDOW) + j * (64 * nq_p // WINDOW) - 1, 0),
                    km_map=meta_blk,
                    kp_cols=(C_K // KV_W, C_V // KV_W), km_cols=(C_K // KV_W, C_V // KV_W),
                    mix=mix, mix_rows=NT)
    mix = attn_call(sinks, proj, proj, (st['win_k'], st['win_v']), (st['meta_k'], st['meta_v']),
                    nb=DEC_BATCH, length=DEC_SEQ, ch=64, nq=1, use_meta=True, chunk0=2,
                    q_row0=S_ROW0, kp_map=lambda b, j: b, km_map=lambda b, j: b,
                    kp_cols=(0, 0), km_cols=(0, 0), mix=mix, mix_rows=NT)

    mix = pool_call(proj, zeros16, lambda b: b, 0, lw['pool_w'], lw['pool_scale'],
                    nb=BATCH, length=N_META, tb=N_META, row0=M_ROW0, ramp=True, mix=mix, mix_rows=NT)
    mix = pool_call(proj, proj, lambda b: M_ROW0 // 16 + b, C_U // GROUP_W, lw['pool_w'],
                    lw['pool_scale'], nb=BATCH, length=SEQ, tb=512, row0=0, ramp=False,
                    mix=mix, mix_rows=NT)
    mix = pool_call(proj, st['pool'], lambda b: b, 0, lw['pool_w'], lw['pool_scale'],
                    nb=DEC_BATCH, length=DEC_SEQ, tb=DEC_SEQ, row0=S_ROW0, ramp=False,
                    mix=mix, mix_rows=NT)

    ssd_w = lw['ssd']
    h0z = jnp.zeros((BATCH, 512, C_STATE), F32)
    mix, hc_m = ssd_call(proj, zeros8, lambda b: b, (0, 2, 3), h0z, ssd_w,
                         nb=BATCH, length=N_META, q=N_META, row0=M_ROW0, mix=mix, mix_rows=NT)
    meta_tail = lambda b: (M_ROW0 + 8) // 8 + 2 * b
    mix, hc_p = ssd_call(proj, proj, meta_tail, (C_XS // 512, C_B // 256, C_C // 256), hc_m, ssd_w,
                         nb=BATCH, length=SEQ, q=256, row0=0, mix=mix, mix_rows=NT)
    mix, hc_s = ssd_call(proj, st['ssm_conv'], lambda b: b, (0, 2, 3), st['ssm'], ssd_w,
                         nb=DEC_BATCH, length=DEC_SEQ, q=DEC_SEQ, row0=S_ROW0, mix=mix, mix_rows=NT)

    lru_w = lw['lru']
    l0z = jnp.zeros((BATCH, 1, GROUP_W), F32)
    mix, hd_m = lru_call(proj, zeros8, lambda b: b, 0, l0z, lru_w,
                         nb=BATCH, length=N_META, tb=N_META, row0=M_ROW0, mix=mix, mix_rows=NT)
    mix, hd_p = lru_call(proj, proj, meta_tail, C_RX // GROUP_W, hd_m, lru_w,
                         nb=BATCH, length=SEQ, tb=256, row0=0, mix=mix, mix_rows=NT)
    mix, hd_s = lru_call(proj, st['lru_conv'], lambda b: b, 0, st['lru'], lru_w,
                         nb=DEC_BATCH, length=DEC_SEQ, tb=DEC_SEQ, row0=S_ROW0, mix=mix, mix_rows=NT)

    x1, x1p, sc = out_proj(mix, h, lw['w_out'], lw['ln1_g'], lw['ln1_b'], lw['rw'])

    e_sel, rank_sel, w_sel, cnt = router(sc, lw['router_bias'])
    counts = cnt[0, :N_EXPERTS].astype(jnp.int32)
    pstarts, dest_disp, dest_comb, blk_e, nused, next_e = _schedule(
        counts, e_sel[:, :TOP_K], rank_sel[:, :TOP_K])
    xg = dispatch(counts, pstarts, dest_disp, x1p)
    yb = moe_experts(blk_e, nused, next_e, xg, lw['wg'], lw['wu'], lw['wd'], lw['layer'])
    h_new = combine_shared_ln2(dest_comb, x1, w_sel, yb, lw['sh_wg'], lw['sh_wu'], lw['sh_wd'],
                               lw['ln2_g'], lw['ln2_b'])

    def tail(row0, nb, length, nrows, c0, width):
        return jnp.stack([proj[row0 + (b + 1) * length - nrows:row0 + (b + 1) * length, c0:c0 + width]
                          for b in range(nb)])

    kv4 = lambda x: x.reshape(x.shape[0], x.shape[1], A_KV_HEADS, HEAD_DIM)
    p_state = (
        kv4(tail(M_ROW0, BATCH, N_META, N_META, C_K, KV_W)),
        kv4(tail(M_ROW0, BATCH, N_META, N_META, C_V, KV_W)),
        kv4(tail(0, BATCH, SEQ, WINDOW, C_K, KV_W)), kv4(tail(0, BATCH, SEQ, WINDOW, C_V, KV_W)),
        tail(0, BATCH, SEQ, POOL_STATE, C_U, GROUP_W),
        tail(0, BATCH, SEQ, CONV_W - 1, C_XS, C_CONV_CH),
        hc_p.reshape(BATCH, C_HEADS, 64, C_STATE),
        tail(0, BATCH, SEQ, CONV_W - 1, C_RX, GROUP_W),
        hd_p.reshape(BATCH, GROUP_W),
    )
    s_state = (
        kv4(tail(S_ROW0, DEC_BATCH, DEC_SEQ, DEC_SEQ, C_K, KV_W)),
        kv4(tail(S_ROW0, DEC_BATCH, DEC_SEQ, DEC_SEQ, C_V, KV_W)),
        tail(S_ROW0, DEC_BATCH, DEC_SEQ, POOL_STATE, C_U, GROUP_W),
        tail(S_ROW0, DEC_BATCH, DEC_SEQ, CONV_W - 1, C_XS, C_CONV_CH),
        hc_s.reshape(DEC_BATCH, C_HEADS, 64, C_STATE),
        tail(S_ROW0, DEC_BATCH, DEC_SEQ, CONV_W - 1, C_RX, GROUP_W),
        hd_s.reshape(DEC_BATCH, GROUP_W),
    )
    return h_new, p_state, s_state


def _block_diag(w):
    z = jnp.zeros((D_BLOCK_W, D_BLOCK_W), w.dtype)
    return jnp.stack([jnp.block([[w[2 * s], z], [z, w[2 * s + 1]]]) for s in range(4)])


def _pad_lanes(v, width=LANE):
    return jnp.pad(v, (0, width - v.shape[0])).reshape(1, width)


def kernel(x_prompt, x_sample, cache_attn_meta_k, cache_attn_meta_v, cache_attn_k, cache_attn_v, state_pool, state_ssm_conv, state_ssm, state_lru_conv, state_lru, meta_tokens, ln_in_g, ln_in_b, w_in, w_out, attn_sinks, pool_w, pool_scale, ssm_conv_w, ssm_conv_b, ssm_dt_bias, ssm_a_log, ssm_d, ssm_norm_g, lru_conv_w, lru_conv_b, lru_wr, lru_br, lru_wi, lru_bi, lru_lambda, ln1_g, ln1_b, ln2_g, ln2_b, router_w, router_bias, exp_w_gate, exp_w_up, exp_w_down, sh_w_gate, sh_w_up, sh_w_down):
    row = lambda v: v.reshape(1, -1).astype(F32)
    h = ln_in(x_prompt.reshape(P_ROWS, D_MODEL), x_sample.reshape(S_ROWS, D_MODEL),
              meta_tokens.astype(F32), row(ln_in_g), row(ln_in_b))
    p_states, s_states = [], []
    for i in range(DEPTH):
        wi = w_in[i]
        s0 = 0
        parts = {}
        for name, size in zip(('q', 'k', 'v', 'u', 'z', 'xbc', 'dt', 'rx', 'rg'),
                              (512, 128, 128, 512, 512, 1024, 8, 512, 512)):
            parts[name] = wi[:, s0:s0 + size]
            s0 += size
        w_in_p = jnp.concatenate(
            [parts[n] for n in ('q', 'u', 'z', 'rx', 'rg', 'xbc', 'k', 'v', 'dt')]
            + [jnp.zeros((D_MODEL, PROJ_P - C_DT - C_HEADS), F32)], axis=1).astype(BF16)
        rw = jnp.pad(router_w[i].astype(F32), ((0, 0), (0, LANE - N_EXPERTS))).astype(BF16)
        cw = ssm_conv_w[i].astype(F32)
        cb = ssm_conv_b[i].astype(F32)
        lw = dict(
            w_in=w_in_p, w_out=w_out[i].astype(BF16), sinks=attn_sinks[i].astype(F32),
            pool_w=pool_w[i].astype(BF16), pool_scale=row(pool_scale[i]),
            ssd=dict(cw_x=cw[:, :512], cw_b=cw[:, 512:768], cw_c=cw[:, 768:],
                     cb_x=row(cb[:512]), cb_b=row(cb[512:768]), cb_c=row(cb[768:]),
                     dt_bias=_pad_lanes(ssm_dt_bias[i].astype(F32)),
                     a_log=_pad_lanes(ssm_a_log[i].astype(F32)),
                     d_skip=row(jnp.repeat(ssm_d[i].astype(F32), 64)),
                     norm_g=row(ssm_norm_g[i])),
            lru=dict(cw=lru_conv_w[i].astype(F32), cb=row(lru_conv_b[i]),
                     wr=_block_diag(lru_wr[i]).astype(BF16), wi=_block_diag(lru_wi[i]).astype(BF16),
                     br=row(lru_br[i]), bi=row(lru_bi[i]), lam=row(lru_lambda[i])),
            ln1_g=row(ln1_g[i]), ln1_b=row(ln1_b[i]), ln2_g=row(ln2_g[i]), ln2_b=row(ln2_b[i]),
            rw=rw, router_bias=_pad_lanes(router_bias[i].astype(F32)),
            wg=exp_w_gate, wu=exp_w_up, wd=exp_w_down, layer=i,
            sh_wg=sh_w_gate[i].astype(BF16), sh_wu=sh_w_up[i].astype(BF16),
            sh_wd=sh_w_down[i].astype(BF16),
        )
        st = dict(
            meta_k=cache_attn_meta_k[i].reshape(DEC_BATCH * N_META, KV_W),
            meta_v=cache_attn_meta_v[i].reshape(DEC_BATCH * N_META, KV_W),
            win_k=cache_attn_k[i].reshape(DEC_BATCH * WINDOW, KV_W),
            win_v=cache_attn_v[i].reshape(DEC_BATCH * WINDOW, KV_W),
            pool=_pad_rows(state_pool[i], 16),
            ssm_conv=_pad_rows(state_ssm_conv[i], 8),
            ssm=state_ssm[i].reshape(DEC_BATCH, 512, C_STATE),
            lru_conv=_pad_rows(state_lru_conv[i], 8),
            lru=state_lru[i].reshape(DEC_BATCH, 1, GROUP_W),
        )
        h, ps, ss = _layer(h, st, lw)
        p_states.append(ps)
        s_states.append(ss)
    stk = lambda sts, j: jnp.stack([s[j] for s in sts])
    y_prompt = h[:P_ROWS].reshape(BATCH, SEQ, D_MODEL)
    y_sample = h[S_ROW0:S_ROW0 + S_ROWS].reshape(DEC_BATCH, DEC_SEQ, D_MODEL)
    return ((y_prompt, y_sample)
            + tuple(stk(p_states, j) for j in range(9))
            + tuple(stk(s_states, j) for j in range(7)))
```

```python
import functools
import math

import jax
import jax.numpy as jnp
from jax import lax
from jax.experimental import pallas as pl
from jax.experimental.pallas import tpu as pltpu

F32 = jnp.float32
BF16 = jnp.bfloat16

D_MODEL = 2048
BATCH = 4
SEQ = 4096
DEPTH = 4
DEC_BATCH = 16
DEC_SEQ = 64
N_META = 16
GROUP_W = 512
HEAD_DIM = 64
A_HEADS = 8
A_KV_HEADS = 2
A_GROUP = 4
KV_W = 128
WINDOW = 128
POOL_SIZES = (2, 4, 8, 16)
POOL_GW = 128
POOL_STATE = 15
C_HEADS = 8
C_STATE = 128
C_CONV_CH = 1024
CONV_W = 4
D_BLOCKS = 8
D_BLOCK_W = 64
LRU_C = 8.0
N_EXPERTS = 64
TOP_K = 8
N_GROUP = 8
TOPK_GROUP = 4
D_EXPERT = 512
ROUTE_SCALE = 2.5
ALPHA = (2 * DEPTH) ** 0.25
LN_EPS = 1e-5

LANE = 128
SUBLANE = 8
VMEM_LIMIT = 56 * 1024 * 1024

P_ROWS = BATCH * SEQ
S_ROWS = DEC_BATCH * DEC_SEQ
M_ROWS = BATCH * N_META
S_ROW0 = P_ROWS
M_ROW0 = P_ROWS + S_ROWS
NT = P_ROWS + S_ROWS + M_ROWS
TM = 512
N_TILES = -(-NT // TM)

C_Q, C_U, C_Z, C_RX, C_RG, C_XS, C_B, C_C, C_K, C_V, C_DT = (
    0, 512, 1024, 1536, 2048, 2560, 3072, 3328, 3584, 3712, 3840)
PROJ_P = 4096
PROJ_TN = 1024

MOE_T = 512
NK = NT * TOP_K
MOE_NB = (NK + N_EXPERTS * (MOE_T - 1) + MOE_T - 1) // MOE_T


def _cparams(sem):
    return pltpu.CompilerParams(dimension_semantics=sem, vmem_limit_bytes=VMEM_LIMIT)


def _ln(x, g, b):
    mu = jnp.mean(x, axis=-1, keepdims=True)
    xc = x - mu
    var = jnp.mean(xc * xc, axis=-1, keepdims=True)
    return xc * lax.rsqrt(var + LN_EPS) * g + b


def _dot(a, b):
    return jnp.dot(a, b, preferred_element_type=F32)


def _dot_nt(a, b, precision=None):
    return lax.dot_general(a, b, (((1,), (1,)), ((), ())), precision=precision,
                           preferred_element_type=F32)


def _full(shape):
    nd = len(shape)
    return pl.BlockSpec(shape, lambda *_: (0,) * nd)


MIX_W = 4 * GROUP_W


def _skip_ref(kern, idx, *refs):
    return kern(*refs[:idx], *refs[idx + 1:])


def _mixer_call(kern, grid, in_specs, args, y_spec, more_out_specs, more_out_shapes, scratch,
                mix, mix_rows, name):
    in_specs = list(in_specs)
    args = tuple(args)
    aliases = {}
    if mix is not None:
        n_in = len(in_specs)
        kern = functools.partial(_skip_ref, kern, n_in)
        in_specs.append(pl.BlockSpec(memory_space=pl.ANY))
        args = args + (mix,)
        aliases = {n_in: 0}
    return pl.pallas_call(
        kern,
        grid=grid,
        in_specs=in_specs,
        out_specs=[y_spec] + list(more_out_specs),
        out_shape=[jax.ShapeDtypeStruct((mix_rows, MIX_W), F32)] + list(more_out_shapes),
        scratch_shapes=scratch,
        input_output_aliases=aliases,
        compiler_params=_cparams(("arbitrary",) * len(grid)),
        name=name,
    )(*args)


def _ln_in_kernel(xp_ref, xs_ref, meta_ref, g_ref, b_ref, o_ref):
    i = pl.program_id(0)
    g = g_ref[...]
    b = b_ref[...]
    n_p = P_ROWS // TM
    n_s = S_ROWS // TM

    @pl.when(i < n_p)
    def _():
        o_ref[...] = _ln(xp_ref[...], g, b)

    @pl.when((i >= n_p) & (i < n_p + n_s))
    def _():
        o_ref[...] = _ln(xs_ref[...], g, b)

    @pl.when(i == n_p + n_s)
    def _():
        m = _ln(meta_ref[...], g, b)
        for r in range(BATCH):
            o_ref[N_META * r:N_META * (r + 1), :] = m


def ln_in(xp2, xs2, meta, g, b):
    n_p = P_ROWS // TM
    n_s = S_ROWS // TM
    return pl.pallas_call(
        _ln_in_kernel,
        grid=(N_TILES,),
        in_specs=[
            pl.BlockSpec((TM, D_MODEL), lambda i: (jnp.minimum(i, n_p - 1), 0)),
            pl.BlockSpec((TM, D_MODEL), lambda i: (jnp.clip(i - n_p, 0, n_s - 1), 0)),
            _full((N_META, D_MODEL)),
            _full((1, D_MODEL)),
            _full((1, D_MODEL)),
        ],
        out_specs=pl.BlockSpec((TM, D_MODEL), lambda i: (i, 0)),
        out_shape=jax.ShapeDtypeStruct((NT, D_MODEL), F32),
        compiler_params=_cparams(("arbitrary",)),
        name="ln_in",
    )(xp2, xs2, meta, g, b)


def _in_proj_kernel(x_ref, w_ref, o_ref):
    o_ref[...] = _dot(x_ref[...].astype(BF16), w_ref[...])


PROJ_TM = 1024


def in_proj(h, w_bf):
    return pl.pallas_call(
        _in_proj_kernel,
        grid=(-(-NT // PROJ_TM), PROJ_P // PROJ_TN),
        in_specs=[
            pl.BlockSpec((PROJ_TM, D_MODEL), lambda i, n: (i, 0)),
            pl.BlockSpec((D_MODEL, PROJ_TN), lambda i, n: (0, n)),
        ],
        out_specs=pl.BlockSpec((PROJ_TM, PROJ_TN), lambda i, n: (i, n)),
        out_shape=jax.ShapeDtypeStruct((NT, PROJ_P), F32),
        compiler_params=_cparams(("arbitrary", "arbitrary")),
        name="in_proj",
    )(h, w_bf)


ATT_PAD = 64
ATT_WIN = ATT_PAD + WINDOW + 64


def _attn_kernel(sink_ref, q_ref, kc_ref, vc_ref, kp_ref, vp_ref, km_ref, vm_ref, o_ref,
                 kbuf, vbuf, *, ch, nq, use_meta, chunk0):
    j = pl.program_id(1)
    tq = ch * nq
    zpad = jnp.zeros((ATT_PAD - N_META, KV_W), BF16)
    kbuf[0:N_META, :] = km_ref[...].astype(BF16)
    kbuf[N_META:ATT_PAD, :] = zpad
    vbuf[0:N_META, :] = vm_ref[...].astype(BF16)
    vbuf[N_META:ATT_PAD, :] = zpad
    kbuf[ATT_PAD:ATT_PAD + WINDOW, :] = kp_ref[...].astype(BF16)
    vbuf[ATT_PAD:ATT_PAD + WINDOW, :] = vp_ref[...].astype(BF16)
    kbuf[ATT_PAD + WINDOW:ATT_PAD + WINDOW + tq, :] = kc_ref[...].astype(BF16)
    vbuf[ATT_PAD + WINDOW:ATT_PAD + WINDOW + tq, :] = vc_ref[...].astype(BF16)
    if ch < 64:
        zc = jnp.zeros((64 - ch, KV_W), BF16)
        kbuf[ATT_PAD + WINDOW + tq:ATT_PAD + WINDOW + tq + 64 - ch, :] = zc
        vbuf[ATT_PAD + WINDOW + tq:ATT_PAD + WINDOW + tq + 64 - ch, :] = zc

    rows = A_GROUP * ch
    col = lax.broadcasted_iota(jnp.int32, (rows, ATT_WIN), 1)
    row = lax.broadcasted_iota(jnp.int32, (rows, 1), 0)
    lane = lax.broadcasted_iota(jnp.int32, (ch, LANE), 1)
    lo = lane < HEAD_DIM

    for i in range(nq):
        c = chunk0 + j * nq + i
        first_band = ATT_PAD + 64 * jnp.maximum(2 - c, 0)
        valid = (col >= first_band) & (col < ATT_PAD + WINDOW + ch)
        if use_meta:
            valid = valid | (col < N_META)
        kcat = jnp.concatenate(
            [kbuf[0:ATT_PAD, :], kbuf[ATT_PAD + ch * i:ATT_PAD + ch * i + WINDOW + 64, :]], axis=0)
        vcat = jnp.concatenate(
            [vbuf[0:ATT_PAD, :], vbuf[ATT_PAD + ch * i:ATT_PAD + ch * i + WINDOW + 64, :]], axis=0)
        qi = q_ref[ch * i:ch * (i + 1), :] * (HEAD_DIM ** -0.5)
        tiles = [qi[:, LANE * t:LANE * (t + 1)] for t in range(A_HEADS // 2)]
        out_tiles = [None] * (A_HEADS // 2)
        for kh in range(A_KV_HEADS):
            qs = []
            for r in range(A_GROUP):
                h = A_GROUP * kh + r
                t = tiles[h // 2]
                if h % 2 != kh:
                    t = pltpu.roll(t, HEAD_DIM, axis=1)
                keep = lo if kh == 0 else jnp.logical_not(lo)
                qs.append(jnp.where(keep, t, 0.0).astype(BF16))
            qz = jnp.concatenate(qs, axis=0)
            s = _dot_nt(qz, kcat)
            s = jnp.where(valid, s, -jnp.inf)
            sink = jnp.zeros((rows, 1), F32)
            for r in range(A_GROUP):
                sink = jnp.where((row >= r * ch) & (row < (r + 1) * ch),
                                 sink_ref[A_GROUP * kh + r], sink)
            m = jnp.maximum(jnp.max(s, axis=-1, keepdims=True), sink)
            p = jnp.exp(s - m)
            den = jnp.sum(p, axis=-1, keepdims=True) + jnp.exp(sink - m)
            probs = (p / den).astype(BF16)
            o = _dot(probs, vcat)
            for r in range(A_GROUP):
                h = A_GROUP * kh + r
                oh = o[r * ch:(r + 1) * ch, :]
                if h % 2 != kh:
                    oh = pltpu.roll(oh, HEAD_DIM, axis=1)
                keep = lo if h % 2 == 0 else jnp.logical_not(lo)
                prev = out_tiles[h // 2]
                out_tiles[h // 2] = jnp.where(keep, oh, 0.0 if prev is None else prev)
        o_ref[ch * i:ch * (i + 1), :] = jnp.concatenate(out_tiles, axis=1)


def attn_call(sinks, q_src, kc_src, kp_src, km_src, *, nb, length, ch, nq, use_meta, chunk0,
              q_row0, kp_map, km_map, kp_cols, km_cols, mix, mix_rows):
    tq = ch * nq
    nj = length // tq
    qb0 = q_row0 // tq
    kern = functools.partial(_attn_kernel, ch=ch, nq=nq, use_meta=use_meta, chunk0=chunk0)
    in_specs = [
        pl.BlockSpec(memory_space=pltpu.SMEM),
        pl.BlockSpec((tq, GROUP_W), lambda b, j: (qb0 + b * nj + j, C_Q // GROUP_W)),
        pl.BlockSpec((tq, KV_W), lambda b, j: (qb0 + b * nj + j, C_K // KV_W)),
        pl.BlockSpec((tq, KV_W), lambda b, j: (qb0 + b * nj + j, C_V // KV_W)),
        pl.BlockSpec((WINDOW, KV_W), lambda b, j: (kp_map(b, j), kp_cols[0])),
        pl.BlockSpec((WINDOW, KV_W), lambda b, j: (kp_map(b, j), kp_cols[1])),
        pl.BlockSpec((N_META, KV_W), lambda b, j: (km_map(b, j), km_cols[0])),
        pl.BlockSpec((N_META, KV_W), lambda b, j: (km_map(b, j), km_cols[1])),
    ]
    return _mixer_call(
        kern, (nb, nj), in_specs,
        (sinks, q_src, kc_src, kc_src, kp_src[0], kp_src[1], km_src[0], km_src[1]),
        pl.BlockSpec((tq, GROUP_W), lambda b, j: (qb0 + b * nj + j, 0)), [], [],
        [pltpu.VMEM((ATT_PAD + WINDOW + tq + 64, KV_W), BF16),
         pltpu.VMEM((ATT_PAD + WINDOW + tq + 64, KV_W), BF16)],
        mix, mix_rows, "attn")[0]


def _pool_kernel(u_ref, prev_ref, w_ref, scale_ref, o_ref, buf, *, tb, ramp):
    j = pl.program_id(1)

    @pl.when(j == 0)
    def _():
        buf[0:16, :] = prev_ref[...]

    buf[16:16 + tb, :] = u_ref[...]
    pos = j * tb + lax.broadcasted_iota(jnp.int32, (tb, 1), 0)
    outs = []
    for g, win in enumerate(POOL_SIZES):
        sl = slice(g * POOL_GW, (g + 1) * POOL_GW)
        tot = buf[16:16 + tb, sl]
        for k in range(1, win):
            tot = tot + buf[16 - k:16 - k + tb, sl]
        if ramp:
            cnt = jnp.minimum(win, pos + 1).astype(F32)
            mean = tot / cnt
        else:
            mean = tot * (1.0 / win)
        d = mean - buf[16:16 + tb, sl]
        outs.append(_dot(d.astype(BF16), w_ref[g]))
    o_ref[...] = jnp.concatenate(outs, axis=1) * scale_ref[...]
    buf[0:16, :] = buf[tb:tb + 16, :]


def pool_call(proj, prev_src, prev_map, prev_col, w_bf, scale, *, nb, length, tb, row0, ramp,
              mix, mix_rows):
    nj = length // tb
    rb0 = row0 // tb
    kern = functools.partial(_pool_kernel, tb=tb, ramp=ramp)
    in_specs = [
        pl.BlockSpec((tb, GROUP_W), lambda b, j: (rb0 + b * nj + j, C_U // GROUP_W)),
        pl.BlockSpec((16, GROUP_W), lambda b, j: (prev_map(b), prev_col)),
        _full((4, POOL_GW, POOL_GW)),
        _full((1, GROUP_W)),
    ]
    return _mixer_call(
        kern, (nb, nj), in_specs, (proj, prev_src, w_bf, scale),
        pl.BlockSpec((tb, GROUP_W), lambda b, j: (rb0 + b * nj + j, 1)), [], [],
        [pltpu.VMEM((tb + 16, GROUP_W), F32)], mix, mix_rows, "pool")[0]


def _conv_block(buf, x_ref, w_ref, b_ref, tb):
    buf[8:8 + tb, :] = x_ref[...]
    acc = b_ref[...] + buf[5:5 + tb, :] * w_ref[0:1, :]
    for k in range(1, CONV_W):
        acc = acc + buf[5 + k:5 + k + tb, :] * w_ref[k:k + 1, :]
    return acc


def _conv_carry(buf, tb):
    buf[0:8, :] = buf[tb:tb + 8, :]


def _col(x, h):
    lane = lax.broadcasted_iota(jnp.int32, x.shape, 1)
    return jnp.sum(jnp.where(lane == h, x, 0.0), axis=1, keepdims=True)


def _ssd_kernel(xs_ref, bm_ref, cm_ref, dt_ref, z_ref, px_ref, pb_ref, pc_ref, h0_ref,
                wx_ref, wb_ref, wc_ref, bx_ref, bb_ref, bc_ref,
                dtb_ref, alog_ref, dskip_ref, ng_ref,
                y_ref, hout_ref, bufx, bufb, bufc, hst, *, q):
    j = pl.program_id(1)
    nj = pl.num_programs(1)
    hi = lax.Precision.HIGHEST

    @pl.when(j == 0)
    def _():
        bufx[0:8, :] = px_ref[...]
        bufb[0:8, :] = pb_ref[...]
        bufc[0:8, :] = pc_ref[...]
        hst[...] = h0_ref[...]

    xs = jax.nn.silu(_conv_block(bufx, xs_ref, wx_ref, bx_ref, q))
    bm = jax.nn.silu(_conv_block(bufb, bm_ref, wb_ref, bb_ref, q))
    cm = jax.nn.silu(_conv_block(bufc, cm_ref, wc_ref, bc_ref, q))
    _conv_carry(bufx, q)
    _conv_carry(bufb, q)
    _conv_carry(bufc, q)

    lane1 = lax.broadcasted_iota(jnp.int32, (1, LANE), 1)
    hmask = lane1 < C_HEADS
    dt = jnp.where(hmask, jax.nn.softplus(dt_ref[...] + dtb_ref[...]), 0.0)
    a = jnp.where(hmask, -jnp.exp(alog_ref[...]), 0.0)
    dta = dt * a
    ri = lax.broadcasted_iota(jnp.int32, (q, q), 0)
    ci = lax.broadcasted_iota(jnp.int32, (q, q), 1)
    tri = ri >= ci
    cum = jnp.dot(tri.astype(F32), dta, precision=hi, preferred_element_type=F32)
    eye = (lax.broadcasted_iota(jnp.int32, (LANE, LANE), 0)
           == lax.broadcasted_iota(jnp.int32, (LANE, LANE), 1)).astype(F32)
    cum_t = _dot_nt(eye, cum, precision=hi)
    dt_t = _dot_nt(eye, dt, precision=hi)
    ecum = jnp.exp(cum)
    cum_last = cum[q - 1:q, :]
    te = jnp.exp(cum_last - cum) * dt

    lane = lax.broadcasted_iota(jnp.int32, (q, LANE), 1)
    lo = lane < 64
    bm_bf = bm.astype(BF16)
    cm_bf = cm.astype(BF16)
    cb = [_dot_nt(cm_bf[:, LANE * g:LANE * (g + 1)], bm_bf[:, LANE * g:LANE * (g + 1)])
          for g in range(2)]
    yoff = [_dot_nt(cm_bf[:, LANE * g:LANE * (g + 1)], hst[256 * g:256 * (g + 1), :].astype(BF16))
            for g in range(2)]

    y_tiles = []
    xw_tiles = []
    for k in range(C_HEADS // 2):
        g = k // 2
        x_pair = xs[:, LANE * k:LANE * (k + 1)]
        ydiag = None
        for par in range(2):
            h = 2 * k + par
            seg = _col(cum, h) - cum_t[h:h + 1, :]
            lm = jnp.exp(jnp.where(tri, seg, -jnp.inf))
            mm = (cb[g] * lm * dt_t[h:h + 1, :]).astype(BF16)
            xm = jnp.where(lo if par == 0 else jnp.logical_not(lo), x_pair, 0.0).astype(BF16)
            part = _dot(mm, xm)
            ydiag = part if ydiag is None else ydiag + part
        e_pair = jnp.where(lo, _col(ecum, 2 * k), _col(ecum, 2 * k + 1))
        te_pair = jnp.where(lo, _col(te, 2 * k), _col(te, 2 * k + 1))
        kk = k % 2
        y_tiles.append(ydiag + yoff[g][:, LANE * kk:LANE * (kk + 1)] * e_pair
                       + dskip_ref[:, LANE * k:LANE * (k + 1)] * x_pair)
        xw_tiles.append((x_pair * te_pair).astype(BF16))

    eye2 = (lax.broadcasted_iota(jnp.int32, (256, 256), 0)
            == lax.broadcasted_iota(jnp.int32, (256, 256), 1)).astype(BF16)
    for g in range(2):
        xw = jnp.concatenate(xw_tiles[2 * g:2 * g + 2], axis=1)
        xw_t = _dot_nt(eye2, xw).astype(BF16)
        s_new = _dot(xw_t, bm_bf[:, LANE * g:LANE * (g + 1)])
        dec = jnp.concatenate(
            [jnp.broadcast_to(jnp.exp(cum_t[4 * g + r:4 * g + r + 1, q - 1:q]), (64, LANE))
             for r in range(4)], axis=0)
        hst[256 * g:256 * (g + 1), :] = dec * hst[256 * g:256 * (g + 1), :] + s_new

    y = jnp.concatenate(y_tiles, axis=1) * jax.nn.silu(z_ref[...])
    y = y * lax.rsqrt(jnp.mean(y * y, axis=-1, keepdims=True) + 1e-6) * ng_ref[...]
    y_ref[...] = y

    @pl.when(j == nj - 1)
    def _():
        hout_ref[...] = hst[...]


def ssd_call(proj, prev_src, prev_map, prev_cols, h0, lw, *, nb, length, q, row0, mix, mix_rows):
    nj = length // q
    rb0 = row0 // q
    kern = functools.partial(_ssd_kernel, q=q)
    blk = lambda width, col: pl.BlockSpec((q, width), lambda b, j: (rb0 + b * nj + j, col // width))
    pblk = lambda width, col: pl.BlockSpec((8, width), lambda b, j: (prev_map(b), col))
    in_specs = [
        blk(512, C_XS), blk(256, C_B), blk(256, C_C), blk(LANE, C_DT), blk(512, C_Z),
        pblk(512, prev_cols[0]), pblk(256, prev_cols[1]), pblk(256, prev_cols[2]),
        pl.BlockSpec((None, 512, C_STATE), lambda b, j: (b, 0, 0)),
        _full((CONV_W, 512)), _full((CONV_W, 256)), _full((CONV_W, 256)),
        _full((1, 512)), _full((1, 256)), _full((1, 256)),
        _full((1, LANE)), _full((1, LANE)), _full((1, 512)), _full((1, 512)),
    ]
    args = (proj, proj, proj, proj, proj, prev_src, prev_src, prev_src, h0,
            lw['cw_x'], lw['cw_b'], lw['cw_c'], lw['cb_x'], lw['cb_b'], lw['cb_c'],
            lw['dt_bias'], lw['a_log'], lw['d_skip'], lw['norm_g'])
    return _mixer_call(
        kern, (nb, nj), in_specs, args,
        pl.BlockSpec((q, GROUP_W), lambda b, j: (rb0 + b * nj + j, 2)),
        [pl.BlockSpec((None, 512, C_STATE), lambda b, j: (b, 0, 0))],
        [jax.ShapeDtypeStruct((nb, 512, C_STATE), F32)],
        [pltpu.VMEM((q + 8, 512), F32), pltpu.VMEM((q + 8, 256), F32),
         pltpu.VMEM((q + 8, 256), F32), pltpu.VMEM((512, C_STATE), F32)],
        mix, mix_rows, "ssd")


def _lru_kernel(rx_ref, rg_ref, prev_ref, h0_ref, cw_ref, cb_ref, wr_ref, wi_ref,
                br_ref, bi_ref, lam_ref, y_ref, hout_ref, buf, hc, *, tb):
    j = pl.program_id(1)
    nj = pl.num_programs(1)

    @pl.when(j == 0)
    def _():
        buf[0:8, :] = prev_ref[...]
        hc[...] = jnp.broadcast_to(h0_ref[...], hc.shape)

    xc = _conv_block(buf, rx_ref, cw_ref, cb_ref, tb)
    _conv_carry(buf, tb)
    rs, gs = [], []
    for s in range(GROUP_W // LANE):
        xb = xc[:, LANE * s:LANE * (s + 1)].astype(BF16)
        rs.append(_dot(xb, wr_ref[s]))
        gs.append(_dot(xb, wi_ref[s]))
    r = jax.nn.sigmoid(jnp.concatenate(rs, axis=1) + br_ref[...])
    gi = jax.nn.sigmoid(jnp.concatenate(gs, axis=1) + bi_ref[...])
    log_a = -LRU_C * r * jax.nn.softplus(-lam_ref[...])
    a = jnp.exp(log_a)
    u = jnp.sqrt(jnp.maximum(1.0 - jnp.exp(2.0 * log_a), 0.0)) * (gi * xc)
    t = lax.broadcasted_iota(jnp.int32, (tb, 1), 0)
    d = 1
    while d < tb:
        a_sh = jnp.where(t >= d, pltpu.roll(a, d, axis=0), 1.0)
        u_sh = jnp.where(t >= d, pltpu.roll(u, d, axis=0), 0.0)
        u = a * u_sh + u
        a = a * a_sh
        d *= 2
    h = u + a * hc[0:1, :]
    y_ref[...] = h * jax.nn.gelu(rg_ref[...])
    hc[...] = jnp.broadcast_to(h[tb - 1:tb, :], hc.shape)

    @pl.when(j == nj - 1)
    def _():
        hout_ref[...] = h[tb - 1:tb, :]


def lru_call(proj, prev_src, prev_map, prev_col, h0, lw, *, nb, length, tb, row0, mix, mix_rows):
    nj = length // tb
    rb0 = row0 // tb
    kern = functools.partial(_lru_kernel, tb=tb)
    in_specs = [
        pl.BlockSpec((tb, GROUP_W), lambda b, j: (rb0 + b * nj + j, C_RX // GROUP_W)),
        pl.BlockSpec((tb, GROUP_W), lambda b, j: (rb0 + b * nj + j, C_RG // GROUP_W)),
        pl.BlockSpec((8, GROUP_W), lambda b, j: (prev_map(b), prev_col)),
        pl.BlockSpec((None, 1, GROUP_W), lambda b, j: (b, 0, 0)),
        _full((CONV_W, GROUP_W)), _full((1, GROUP_W)),
        _full((4, LANE, LANE)), _full((4, LANE, LANE)),
        _full((1, GROUP_W)), _full((1, GROUP_W)), _full((1, GROUP_W)),
    ]
    args = (proj, proj, prev_src, h0, lw['cw'], lw['cb'], lw['wr'], lw['wi'],
            lw['br'], lw['bi'], lw['lam'])
    return _mixer_call(
        kern, (nb, nj), in_specs, args,
        pl.BlockSpec((tb, GROUP_W), lambda b, j: (rb0 + b * nj + j, 3)),
        [pl.BlockSpec((None, 1, GROUP_W), lambda b, j: (b, 0, 0))],
        [jax.ShapeDtypeStruct((nb, 1, GROUP_W), F32)],
        [pltpu.VMEM((tb + 8, GROUP_W), F32), pltpu.VMEM((8, GROUP_W), F32)],
        mix, mix_rows, "lru")


HALF = D_MODEL // 2


def _pack_bf16_pairs(x):
    bits = lax.bitcast_convert_type(x.astype(BF16).astype(F32), jnp.uint32)
    return (bits[:, :HALF] >> 16) | (bits[:, HALF:] & jnp.uint32(0xFFFF0000))


def _unpack_pairs_f32(w):
    lo = lax.bitcast_convert_type(w << 16, F32)
    hi = lax.bitcast_convert_type(w & jnp.uint32(0xFFFF0000), F32)
    return lo, hi


def _unpack_bf16_pairs(w):
    lo, hi = _unpack_pairs_f32(w)
    return lo.astype(BF16), hi.astype(BF16)


DMA_UNROLL = 8

ROW_TILE = HALF // LANE


def _store_tile_rows(ref, row0, mat):
    n = mat.shape[0]
    for j in range(ROW_TILE):
        ref[pl.ds(row0 * ROW_TILE + j, n, stride=ROW_TILE), :] = mat[:, LANE * j:LANE * (j + 1)]


def _load_tile_cols(ref, n, j):
    return ref[pl.ds(j, n, stride=ROW_TILE), :]


OUT_SUB = 256


def _out_proj_kernel(mix_ref, h_ref, w_ref, g_ref, b_ref, rw_ref, x1_ref, xp_ref, sc_ref):
    for r in range(TM // OUT_SUB):
        rows = slice(r * OUT_SUB, (r + 1) * OUT_SUB)
        y = _dot(mix_ref[rows, :].astype(BF16), w_ref[...])
        x1 = _ln(ALPHA * h_ref[rows, :] + y, g_ref[...], b_ref[...])
        x1_ref[rows, :] = x1
        _store_tile_rows(xp_ref, r * OUT_SUB, _pack_bf16_pairs(x1))
        logits = lax.dot_general(x1, rw_ref[...], (((1,), (0,)), ((), ())),
                                 preferred_element_type=F32)
        sc_ref[rows, :] = jax.nn.sigmoid(logits)


def out_proj(mix, h, w_bf, g, b, rw):
    return pl.pallas_call(
        _out_proj_kernel,
        grid=(N_TILES,),
        in_specs=[
            pl.BlockSpec((TM, D_MODEL), lambda i: (i, 0)),
            pl.BlockSpec((TM, D_MODEL), lambda i: (i, 0)),
            _full((D_MODEL, D_MODEL)),
            _full((1, D_MODEL)), _full((1, D_MODEL)),
            _full((D_MODEL, LANE)),
        ],
        out_specs=[pl.BlockSpec((TM, D_MODEL), lambda i: (i, 0)),
                   pl.BlockSpec((TM * ROW_TILE, LANE), lambda i: (i, 0)),
                   pl.BlockSpec((TM, LANE), lambda i: (i, 0))],
        out_shape=[jax.ShapeDtypeStruct((NT, D_MODEL), F32),
                   jax.ShapeDtypeStruct((NT * ROW_TILE, LANE), jnp.uint32),
                   jax.ShapeDtypeStruct((NT, LANE), F32)],
        compiler_params=_cparams(("arbitrary",)),
        name="out_proj",
    )(mix, h, w_bf, g, b, rw)


_BIG = 4096


def _group_allreduce(x, lane, op):
    for sh in (1, 2, 4):
        up = pltpu.roll(x, sh, axis=1)
        dn = pltpu.roll(x, LANE - sh, axis=1)
        x = op(x, jnp.where((lane & sh) != 0, up, dn))
    return x


def _router_kernel(sc_ref, bias_ref, e_ref, r_ref, w_ref, cnt_ref, run):
    i = pl.program_id(0)

    @pl.when(i == 0)
    def _():
        run[...] = jnp.zeros_like(run)

    sc = sc_ref[...]
    lane = lax.broadcasted_iota(jnp.int32, (TM, LANE), 1)
    valid = lane < N_EXPERTS
    neg = -jnp.inf
    biased = jnp.where(valid, sc + bias_ref[...], neg)
    gmax = _group_allreduce(biased, lane, jnp.maximum)
    first = _group_allreduce(jnp.where(biased == gmax, lane, _BIG), lane, jnp.minimum)
    second = _group_allreduce(jnp.where(lane == first, neg, biased), lane, jnp.maximum)
    gs = jnp.where(valid, gmax + second, neg)
    grp = lane >> 3
    cand = jnp.full((TM, LANE), neg, F32)
    for _ in range(TOPK_GROUP):
        m = jnp.max(gs, axis=1, keepdims=True)
        g1 = jnp.min(jnp.where(gs == m, grp, _BIG), axis=1, keepdims=True)
        hit = grp == g1
        cand = jnp.where(hit, biased, cand)
        gs = jnp.where(hit, neg, gs)
    sel = jnp.zeros((TM, LANE), F32)
    e_out = jnp.zeros((TM, LANE), jnp.int32)
    w_out = jnp.zeros((TM, LANE), F32)
    idxs = []
    for k in range(TOP_K):
        m = jnp.max(cand, axis=1, keepdims=True)
        ik = jnp.min(jnp.where(cand == m, lane, _BIG), axis=1, keepdims=True)
        hit = lane == ik
        vk = jnp.sum(jnp.where(hit, sc, 0.0), axis=1, keepdims=True)
        sel = jnp.where(hit, 1.0, sel)
        cand = jnp.where(hit, neg, cand)
        e_out = jnp.where(lane == k, ik, e_out)
        w_out = jnp.where(lane == k, vk, w_out)
        idxs.append(ik)
    wsum = jnp.sum(w_out, axis=1, keepdims=True)
    w_ref[...] = w_out / wsum * ROUTE_SCALE
    e_ref[...] = e_out
    rowi = lax.broadcasted_iota(jnp.int32, (TM, 1), 0)
    sel = jnp.where(rowi < NT - i * TM, sel, 0.0)
    ri = lax.broadcasted_iota(jnp.int32, (TM, TM), 0)
    ci = lax.broadcasted_iota(jnp.int32, (TM, TM), 1)
    before = _dot((ri > ci).astype(BF16), sel.astype(BF16))
    rank = run[0:1, :] + before
    r_out = jnp.zeros((TM, LANE), F32)
    for k in range(TOP_K):
        rk = jnp.sum(jnp.where(lane == idxs[k], rank, 0.0), axis=1, keepdims=True)
        r_out = jnp.where(lane == k, rk, r_out)
    r_ref[...] = r_out.astype(jnp.int32)
    run[...] = jnp.broadcast_to(rank[TM - 1:TM, :] + sel[TM - 1:TM, :], run.shape)

    @pl.when(i == pl.num_programs(0) - 1)
    def _():
        cnt_ref[...] = run[...]


def router(scores, bias_row):
    return pl.pallas_call(
        _router_kernel,
        grid=(N_TILES,),
        in_specs=[pl.BlockSpec((TM, LANE), lambda i: (i, 0)), _full((1, LANE))],
        out_specs=[pl.BlockSpec((TM, LANE), lambda i: (i, 0)),
                   pl.BlockSpec((TM, LANE), lambda i: (i, 0)),
                   pl.BlockSpec((TM, LANE), lambda i: (i, 0)),
                   _full((8, LANE))],
        out_shape=[jax.ShapeDtypeStruct((NT, LANE), jnp.int32),
                   jax.ShapeDtypeStruct((NT, LANE), jnp.int32),
                   jax.ShapeDtypeStruct((NT, LANE), F32),
                   jax.ShapeDtypeStruct((8, LANE), F32)],
        scratch_shapes=[pltpu.VMEM((8, LANE), F32)],
        compiler_params=_cparams(("arbitrary",)),
        name="router",
    )(scores, bias_row)


def _dispatch_kernel(cnt_ref, pst_ref, dest_ref, x_ref, xg_hbm, zbuf, sem, zsem):
    i = pl.program_id(0)

    def tile_rows(row):
        return pl.ds(pl.multiple_of(row * ROW_TILE, ROW_TILE), ROW_TILE)

    def row_copy(src_row, dst_tile_row):
        dst = pl.ds(pl.multiple_of(dst_tile_row, ROW_TILE), ROW_TILE)
        return pltpu.make_async_copy(x_ref.at[tile_rows(src_row)], xg_hbm.at[dst], sem)

    def zero_copy(dst_row):
        return pltpu.make_async_copy(zbuf, xg_hbm.at[tile_rows(dst_row)], zsem)

    @pl.when(i == 0)
    def _():
        zbuf[...] = jnp.zeros_like(zbuf)

        def per_expert(e, carry):
            c = cnt_ref[e]
            npad = (MOE_T - c % MOE_T) % MOE_T
            base = pst_ref[e] + c

            def start(r, cc):
                zero_copy(base + r).start()
                return cc

            lax.fori_loop(0, npad, start, 0)

            def wait(r, cc):
                zero_copy(base + r).wait()
                return cc

            lax.fori_loop(0, npad, wait, 0)
            return carry

        lax.fori_loop(0, N_EXPERTS, per_expert, 0)

    def start(t, carry):
        for s in range(TOP_K):
            row_copy(t, dest_ref[t * TOP_K + s]).start(priority=s % 2)
        return carry

    lax.fori_loop(0, TM, start, 0, unroll=DMA_UNROLL)

    for s in range(TOP_K):
        pltpu.make_async_copy(x_ref, x_ref, sem).wait()


def dispatch(counts, pstarts, dest_flat, x1p):
    grid_spec = pltpu.PrefetchScalarGridSpec(
        num_scalar_prefetch=2,
        grid=(N_TILES,),
        in_specs=[
            pl.BlockSpec((TM * TOP_K,), lambda i, c, p: (i,), memory_space=pltpu.SMEM),
            pl.BlockSpec((TM * ROW_TILE, LANE), lambda i, c, p: (i, 0)),
        ],
        out_specs=pl.BlockSpec(memory_space=pl.ANY),
        scratch_shapes=[pltpu.VMEM((ROW_TILE, LANE), jnp.uint32),
                        pltpu.SemaphoreType.DMA(()), pltpu.SemaphoreType.DMA(())],
    )
    return pl.pallas_call(
        _dispatch_kernel,
        grid_spec=grid_spec,
        out_shape=jax.ShapeDtypeStruct(((MOE_NB * MOE_T + N_TRASH) * ROW_TILE, LANE), jnp.uint32),
        compiler_params=_cparams(("arbitrary",)),
        name="dispatch",
    )(counts, pstarts, dest_flat, x1p)


def _moe_kernel(blk_e_ref, nused_ref, next_e_ref, x_ref, wg_hbm, wu_hbm, wd_hbm, o_ref,
                wg_st, wu_st, wd_st, wg_bf, wu_bf, wd_bf, sems, *, layer):
    i = pl.program_id(0)
    changed = jnp.logical_or(i == 0, blk_e_ref[i] != blk_e_ref[jnp.maximum(i - 1, 0)])
    active = i < nused_ref[0]

    def fetch(e):
        return (pltpu.make_async_copy(wg_hbm.at[layer, e], wg_st, sems.at[0]),
                pltpu.make_async_copy(wu_hbm.at[layer, e], wu_st, sems.at[1]),
                pltpu.make_async_copy(wd_hbm.at[layer, e], wd_st, sems.at[2]))

    @pl.when(i == 0)
    def _():
        for c in fetch(blk_e_ref[0]):
            c.start()

    @pl.when(jnp.logical_and(changed, active))
    def _():
        for c in fetch(blk_e_ref[i]):
            c.wait()
        wg_bf[...] = wg_st[...].astype(BF16)
        wu_bf[...] = wu_st[...].astype(BF16)
        wd_bf[...] = wd_st[...].astype(BF16)

        @pl.when(next_e_ref[i] >= 0)
        def _():
            for c in fetch(next_e_ref[i]):
                c.start()

    @pl.when(active)
    def _():
        halves = [_unpack_bf16_pairs(_load_tile_cols(x_ref, MOE_T, j)) for j in range(ROW_TILE)]
        lo = jnp.concatenate([p[0] for p in halves], axis=1)
        hi = jnp.concatenate([p[1] for p in halves], axis=1)
        gate = _dot(lo, wg_bf[0:HALF, :]) + _dot(hi, wg_bf[HALF:D_MODEL, :])
        up = _dot(lo, wu_bf[0:HALF, :]) + _dot(hi, wu_bf[HALF:D_MODEL, :])
        hb = jax.nn.silu(gate) * up
        _store_tile_rows(o_ref, 0, _pack_bf16_pairs(_dot(hb.astype(BF16), wd_bf[...])))

    @pl.when(i >= nused_ref[0])
    def _():
        o_ref[...] = jnp.zeros_like(o_ref)


def moe_experts(blk_e, nused, next_e, xg, wg, wu, wd, layer):
    grid_spec = pltpu.PrefetchScalarGridSpec(
        num_scalar_prefetch=3,
        grid=(MOE_NB,),
        in_specs=[
            pl.BlockSpec((MOE_T * ROW_TILE, LANE),
                         lambda i, be, nu, ne: (jnp.minimum(i, nu[0] - 1), 0)),
            pl.BlockSpec(memory_space=pl.ANY),
            pl.BlockSpec(memory_space=pl.ANY),
            pl.BlockSpec(memory_space=pl.ANY),
        ],
        out_specs=pl.BlockSpec((MOE_T * ROW_TILE, LANE), lambda i, be, nu, ne: (i, 0)),
        scratch_shapes=[pltpu.VMEM((D_MODEL, D_EXPERT), F32),
                        pltpu.VMEM((D_MODEL, D_EXPERT), F32),
                        pltpu.VMEM((D_EXPERT, D_MODEL), F32),
                        pltpu.VMEM((D_MODEL, D_EXPERT), BF16),
                        pltpu.VMEM((D_MODEL, D_EXPERT), BF16),
                        pltpu.VMEM((D_EXPERT, D_MODEL), BF16),
                        pltpu.SemaphoreType.DMA((3,))],
    )
    return pl.pallas_call(
        functools.partial(_moe_kernel, layer=layer),
        grid_spec=grid_spec,
        out_shape=jax.ShapeDtypeStruct((MOE_NB * MOE_T * ROW_TILE, LANE), jnp.uint32),
        compiler_params=_cparams(("arbitrary",)),
        name="moe_experts",
    )(blk_e, nused, next_e, xg, wg, wu, wd)


TMC = 256
ISSUE_SLICES = 8
NC_TILES = -(-NT // TMC)
DEST_LEN = max(N_TILES * TM, NC_TILES * TMC) * TOP_K
N_TRASH = DEST_LEN - NK


def _combine_kernel(dest_ref, dest_next_ref, x1_ref, w_ref, yb_hbm, wg_ref, wu_ref, wd_ref,
                    g_ref, b_ref, o_ref, ybuf, sems):
    i = pl.program_id(0)
    slot = lax.rem(i, 2)

    def issue(d_ref, buf_slot, t0, t1):
        def start(t, carry):
            for s in range(TOP_K):
                src = pl.ds(pl.multiple_of(d_ref[t * TOP_K + s], ROW_TILE), ROW_TILE)
                dst = pl.ds(pl.multiple_of(t * ROW_TILE, ROW_TILE), ROW_TILE)
                pltpu.make_async_copy(yb_hbm.at[src], ybuf.at[buf_slot, s, dst],
                                      sems.at[buf_slot]).start(priority=s % 2)
            return carry

        lax.fori_loop(t0, t1, start, 0, unroll=DMA_UNROLL)

    @pl.when(i == 0)
    def _():
        issue(dest_ref, 0, 0, TMC)

    has_next = i + 1 < pl.num_programs(0)
    per = TMC // ISSUE_SLICES

    def issue_next(c):
        @pl.when(has_next)
        def _():
            issue(dest_next_ref, 1 - slot, c * per, (c + 1) * per)

    x1 = x1_ref[...]
    xb = x1.astype(BF16)
    issue_next(0)
    gate = _dot(xb, wg_ref[...])
    issue_next(1)
    up = _dot(xb, wu_ref[...])
    hb = jax.nn.silu(gate) * up
    issue_next(2)
    shared = _dot(hb.astype(BF16), wd_ref[...])
    issue_next(3)

    pltpu.make_async_copy(ybuf.at[slot], ybuf.at[slot], sems.at[slot]).wait()

    w = w_ref[...]
    acc_lo = [shared[:, LANE * j:LANE * (j + 1)] for j in range(ROW_TILE)]
    acc_hi = [shared[:, HALF + LANE * j:HALF + LANE * (j + 1)] for j in range(ROW_TILE)]
    for s in range(TOP_K):
        ws = _col(w, s)
        for j in range(ROW_TILE):
            lo, hi = _unpack_pairs_f32(_load_tile_cols(ybuf.at[slot, s], TMC, j))
            acc_lo[j] = acc_lo[j] + ws * lo
            acc_hi[j] = acc_hi[j] + ws * hi
        if s % 2 == 1:
            issue_next(4 + s // 2)
    acc = jnp.concatenate(acc_lo + acc_hi, axis=1)
    o_ref[...] = _ln(ALPHA * x1 + acc, g_ref[...], b_ref[...])


def combine_shared_ln2(dest_flat, x1, w, yb, wg, wu, wd, g, b):
    grid_spec = pl.GridSpec(
        grid=(NC_TILES,),
        in_specs=[
            pl.BlockSpec((TMC * TOP_K,), lambda i: (i,), memory_space=pltpu.SMEM),
            pl.BlockSpec((TMC * TOP_K,), lambda i: (jnp.minimum(i + 1, NC_TILES - 1),),
                         memory_space=pltpu.SMEM),
            pl.BlockSpec((TMC, D_MODEL), lambda i: (i, 0)),
            pl.BlockSpec((TMC, LANE), lambda i: (i, 0)),
            pl.BlockSpec(memory_space=pl.ANY),
            _full((D_MODEL, D_EXPERT)), _full((D_MODEL, D_EXPERT)), _full((D_EXPERT, D_MODEL)),
            _full((1, D_MODEL)), _full((1, D_MODEL)),
        ],
        out_specs=pl.BlockSpec((TMC, D_MODEL), lambda i: (i, 0)),
        scratch_shapes=[pltpu.VMEM((2, TOP_K, TMC * ROW_TILE, LANE), jnp.uint32),
                        pltpu.SemaphoreType.DMA((2,))],
    )
    return pl.pallas_call(
        _combine_kernel,
        grid_spec=grid_spec,
        out_shape=jax.ShapeDtypeStruct((NT, D_MODEL), F32),
        compiler_params=_cparams(("arbitrary",)),
        name="combine_shared_ln2",
    )(dest_flat, dest_flat, x1, w, yb, wg, wu, wd, g, b)


def _schedule(counts, e_sel, rank_sel):
    padded = (counts + MOE_T - 1) // MOE_T * MOE_T
    pends = jnp.cumsum(padded)
    pstarts = pends - padded
    onehot = e_sel[:, :, None] == jnp.arange(N_EXPERTS, dtype=jnp.int32)
    dest = rank_sel + jnp.sum(jnp.where(onehot, pstarts, 0), axis=-1)
    dest_flat = dest.reshape(NK).astype(jnp.int32) * ROW_TILE
    trash = (MOE_NB * MOE_T + jnp.arange(N_TRASH, dtype=jnp.int32)) * ROW_TILE
    dest_disp = jnp.concatenate([dest_flat, trash])
    dest_comb = jnp.pad(dest_flat, (0, N_TRASH))
    blk_row0 = jnp.arange(MOE_NB, dtype=jnp.int32) * MOE_T
    experts_ending_by = lambda row: jnp.sum((pends[None, :] <= row[:, None]).astype(jnp.int32), axis=1)
    blk_e = jnp.minimum(experts_ending_by(blk_row0), N_EXPERTS - 1).astype(jnp.int32)
    nused = (pends[-1] // MOE_T).astype(jnp.int32).reshape(1)
    group_end = jnp.sum(jnp.where(blk_e[:, None] == jnp.arange(N_EXPERTS, dtype=jnp.int32), pends, 0), axis=1)
    next_e = jnp.where(group_end < pends[-1],
                       jnp.minimum(experts_ending_by(group_end), N_EXPERTS - 1), -1).astype(jnp.int32)
    return pstarts.astype(jnp.int32), dest_disp, dest_comb, blk_e, nused, next_e


def _pad_rows(x, rows):
    b, r, c = x.shape
    return jnp.pad(x, ((0, 0), (rows - r, 0), (0, 0))).reshape(b * rows, c)


def _layer(h, st, lw):
    proj = in_proj(h, lw['w_in'])
    sinks = lw['sinks']
    zeros16 = jnp.zeros((BATCH * 16, GROUP_W), F32)
    zeros8 = jnp.zeros((BATCH * 8, C_CONV_CH), F32)

    meta_blk = lambda b, j=None: M_ROW0 // N_META + b
    mix = attn_call(sinks, proj, proj, (proj, proj), (proj, proj), nb=BATCH, length=N_META,
                    ch=N_META, nq=1, use_meta=False, chunk0=0, q_row0=M_ROW0,
                    kp_map=lambda b, j: 0, km_map=meta_blk,
                    kp_cols=(C_K // KV_W, C_V // KV_W), km_cols=(C_K // KV_W, C_V // KV_W),
                    mix=None, mix_rows=NT)
    nq_p = 4
    mix = attn_call(sinks, proj, proj, (proj, proj), (proj, proj), nb=BATCH, length=SEQ,
                    ch=64, nq=nq_p, use_meta=True, chunk0=0, q_row0=0,
                    kp_map=lambda b, j: jnp.maximum(b * (SEQ // WINDOW) + j * (64 * nq_p // WINDOW) - 1, 0),
                    km_map=meta_blk,
                    kp_cols=(C_K // KV_W, C_V // KV_W), km_cols=(C_K // KV_W, C_V // KV_W),
                    mix=mix, mix_rows=NT)
    mix = attn_call(sinks, proj, proj, (st['win_k'], st['win_v']), (st['meta_k'], st['meta_v']),
                    nb=DEC_BATCH, length=DEC_SEQ, ch=64, nq=1, use_meta=True, chunk0=2,
                    q_row0=S_ROW0, kp_map=lambda b, j: b, km_map=lambda b, j: b,
                    kp_cols=(0, 0), km_cols=(0, 0), mix=mix, mix_rows=NT)

    mix = pool_call(proj, zeros16, lambda b: b, 0, lw['pool_w'], lw['pool_scale'],
                    nb=BATCH, length=N_META, tb=N_META, row0=M_ROW0, ramp=True, mix=mix, mix_rows=NT)
    mix = pool_call(proj, proj, lambda b: M_ROW0 // 16 + b, C_U // GROUP_W, lw['pool_w'],
                    lw['pool_scale'], nb=BATCH, length=SEQ, tb=512, row0=0, ramp=False,
                    mix=mix, mix_rows=NT)
    mix = pool_call(proj, st['pool'], lambda b: b, 0, lw['pool_w'], lw['pool_scale'],
                    nb=DEC_BATCH, length=DEC_SEQ, tb=DEC_SEQ, row0=S_ROW0, ramp=False,
                    mix=mix, mix_rows=NT)

    ssd_w = lw['ssd']
    h0z = jnp.zeros((BATCH, 512, C_STATE), F32)
    mix, hc_m = ssd_call(proj, zeros8, lambda b: b, (0, 2, 3), h0z, ssd_w,
                         nb=BATCH, length=N_META, q=N_META, row0=M_ROW0, mix=mix, mix_rows=NT)
    meta_tail = lambda b: (M_ROW0 + 8) // 8 + 2 * b
    mix, hc_p = ssd_call(proj, proj, meta_tail, (C_XS // 512, C_B // 256, C_C // 256), hc_m, ssd_w,
                         nb=BATCH, length=SEQ, q=256, row0=0, mix=mix, mix_rows=NT)
    mix, hc_s = ssd_call(proj, st['ssm_conv'], lambda b: b, (0, 2, 3), st['ssm'], ssd_w,
                         nb=DEC_BATCH, length=DEC_SEQ, q=DEC_SEQ, row0=S_ROW0, mix=mix, mix_rows=NT)

    lru_w = lw['lru']
    l0z = jnp.zeros((BATCH, 1, GROUP_W), F32)
    mix, hd_m = lru_call(proj, zeros8, lambda b: b, 0, l0z, lru_w,
                         nb=BATCH, length=N_META, tb=N_META, row0=M_ROW0, mix=mix, mix_rows=NT)
    mix, hd_p = lru_call(proj, proj, meta_tail, C_RX // GROUP_W, hd_m, lru_w,
                         nb=BATCH, length=SEQ, tb=256, row0=0, mix=mix, mix_rows=NT)
    mix, hd_s = lru_call(proj, st['lru_conv'], lambda b: b, 0, st['lru'], lru_w,
                         nb=DEC_BATCH, length=DEC_SEQ, tb=DEC_SEQ, row0=S_ROW0, mix=mix, mix_rows=NT)

    x1, x1p, sc = out_proj(mix, h, lw['w_out'], lw['ln1_g'], lw['ln1_b'], lw['rw'])

    e_sel, rank_sel, w_sel, cnt = router(sc, lw['router_bias'])
    counts = cnt[0, :N_EXPERTS].astype(jnp.int32)
    pstarts, dest_disp, dest_comb, blk_e, nused, next_e = _schedule(
        counts, e_sel[:, :TOP_K], rank_sel[:, :TOP_K])
    xg = dispatch(counts, pstarts, dest_disp, x1p)
    yb = moe_experts(blk_e, nused, next_e, xg, lw['wg'], lw['wu'], lw['wd'], lw['layer'])
    h_new = combine_shared_ln2(dest_comb, x1, w_sel, yb, lw['sh_wg'], lw['sh_wu'], lw['sh_wd'],
                               lw['ln2_g'], lw['ln2_b'])

    def tail(row0, nb, length, nrows, c0, width):
        return jnp.stack([proj[row0 + (b + 1) * length - nrows:row0 + (b + 1) * length, c0:c0 + width]
                          for b in range(nb)])

    kv4 = lambda x: x.reshape(x.shape[0], x.shape[1], A_KV_HEADS, HEAD_DIM)
    p_state = (
        kv4(tail(M_ROW0, BATCH, N_META, N_META, C_K, KV_W)),
        kv4(tail(M_ROW0, BATCH, N_META, N_META, C_V, KV_W)),
        kv4(tail(0, BATCH, SEQ, WINDOW, C_K, KV_W)), kv4(tail(0, BATCH, SEQ, WINDOW, C_V, KV_W)),
        tail(0, BATCH, SEQ, POOL_STATE, C_U, GROUP_W),
        tail(0, BATCH, SEQ, CONV_W - 1, C_XS, C_CONV_CH),
        hc_p.reshape(BATCH, C_HEADS, 64, C_STATE),
        tail(0, BATCH, SEQ, CONV_W - 1, C_RX, GROUP_W),
        hd_p.reshape(BATCH, GROUP_W),
    )
    s_state = (
        kv4(tail(S_ROW0, DEC_BATCH, DEC_SEQ, DEC_SEQ, C_K, KV_W)),
        kv4(tail(S_ROW0, DEC_BATCH, DEC_SEQ, DEC_SEQ, C_V, KV_W)),
        tail(S_ROW0, DEC_BATCH, DEC_SEQ, POOL_STATE, C_U, GROUP_W),
        tail(S_ROW0, DEC_BATCH, DEC_SEQ, CONV_W - 1, C_XS, C_CONV_CH),
        hc_s.reshape(DEC_BATCH, C_HEADS, 64, C_STATE),
        tail(S_ROW0, DEC_BATCH, DEC_SEQ, CONV_W - 1, C_RX, GROUP_W),
        hd_s.reshape(DEC_BATCH, GROUP_W),
    )
    return h_new, p_state, s_state


def _block_diag(w):
    z = jnp.zeros((D_BLOCK_W, D_BLOCK_W), w.dtype)
    return jnp.stack([jnp.block([[w[2 * s], z], [z, w[2 * s + 1]]]) for s in range(4)])


def _pad_lanes(v, width=LANE):
    return jnp.pad(v, (0, width - v.shape[0])).reshape(1, width)


def kernel(x_prompt, x_sample, cache_attn_meta_k, cache_attn_meta_v, cache_attn_k, cache_attn_v, state_pool, state_ssm_conv, state_ssm, state_lru_conv, state_lru, meta_tokens, ln_in_g, ln_in_b, w_in, w_out, attn_sinks, pool_w, pool_scale, ssm_conv_w, ssm_conv_b, ssm_dt_bias, ssm_a_log, ssm_d, ssm_norm_g, lru_conv_w, lru_conv_b, lru_wr, lru_br, lru_wi, lru_bi, lru_lambda, ln1_g, ln1_b, ln2_g, ln2_b, router_w, router_bias, exp_w_gate, exp_w_up, exp_w_down, sh_w_gate, sh_w_up, sh_w_down):
    row = lambda v: v.reshape(1, -1).astype(F32)
    h = ln_in(x_prompt.reshape(P_ROWS, D_MODEL), x_sample.reshape(S_ROWS, D_MODEL),
              meta_tokens.astype(F32), row(ln_in_g), row(ln_in_b))
    p_states, s_states = [], []
    for i in range(DEPTH):
        wi = w_in[i]
        s0 = 0
        parts = {}
        for name, size in zip(('q', 'k', 'v', 'u', 'z', 'xbc', 'dt', 'rx', 'rg'),
                              (512, 128, 128, 512, 512, 1024, 8, 512, 512)):
            parts[name] = wi[:, s0:s0 + size]
            s0 += size
        w_in_p = jnp.concatenate(
            [parts[n] for n in ('q', 'u', 'z', 'rx', 'rg', 'xbc', 'k', 'v', 'dt')]
            + [jnp.zeros((D_MODEL, PROJ_P - C_DT - C_HEADS), F32)], axis=1).astype(BF16)
        rw = jnp.pad(router_w[i].astype(F32), ((0, 0), (0, LANE - N_EXPERTS))).astype(BF16)
        cw = ssm_conv_w[i].astype(F32)
        cb = ssm_conv_b[i].astype(F32)
        lw = dict(
            w_in=w_in_p, w_out=w_out[i].astype(BF16), sinks=attn_sinks[i].astype(F32),
            pool_w=pool_w[i].astype(BF16), pool_scale=row(pool_scale[i]),
            ssd=dict(cw_x=cw[:, :512], cw_b=cw[:, 512:768], cw_c=cw[:, 768:],
                     cb_x=row(cb[:512]), cb_b=row(cb[512:768]), cb_c=row(cb[768:]),
                     dt_bias=_pad_lanes(ssm_dt_bias[i].astype(F32)),
                     a_log=_pad_lanes(ssm_a_log[i].astype(F32)),
                     d_skip=row(jnp.repeat(ssm_d[i].astype(F32), 64)),
                     norm_g=row(ssm_norm_g[i])),
            lru=dict(cw=lru_conv_w[i].astype(F32), cb=row(lru_conv_b[i]),
                     wr=_block_diag(lru_wr[i]).astype(BF16), wi=_block_diag(lru_wi[i]).astype(BF16),
                     br=row(lru_br[i]), bi=row(lru_bi[i]), lam=row(lru_lambda[i])),
            ln1_g=row(ln1_g[i]), ln1_b=row(ln1_b[i]), ln2_g=row(ln2_g[i]), ln2_b=row(ln2_b[i]),
            rw=rw, router_bias=_pad_lanes(router_bias[i].astype(F32)),
            wg=exp_w_gate, wu=exp_w_up, wd=exp_w_down, layer=i,
            sh_wg=sh_w_gate[i].astype(BF16), sh_wu=sh_w_up[i].astype(BF16),
            sh_wd=sh_w_down[i].astype(BF16),
        )
        st = dict(
            meta_k=cache_attn_meta_k[i].reshape(DEC_BATCH * N_META, KV_W),
            meta_v=cache_attn_meta_v[i].reshape(DEC_BATCH * N_META, KV_W),
            win_k=cache_attn_k[i].reshape(DEC_BATCH * WINDOW, KV_W),
            win_v=cache_attn_v[i].reshape(DEC_BATCH * WINDOW, KV_W),
            pool=_pad_rows(state_pool[i], 16),
            ssm_conv=_pad_rows(state_ssm_conv[i], 8),
            ssm=state_ssm[i].reshape(DEC_BATCH, 512, C_STATE),
            lru_conv=_pad_rows(state_lru_conv[i], 8),
            lru=state_lru[i].reshape(DEC_BATCH, 1, GROUP_W),
        )
        h, ps, ss = _layer(h, st, lw)
        p_states.append(ps)
        s_states.append(ss)
    stk = lambda sts, j: jnp.stack([s[j] for s in sts])
    y_prompt = h[:P_ROWS].reshape(BATCH, SEQ, D_MODEL)
    y_sample = h[S_ROW0:S_ROW0 + S_ROWS].reshape(DEC_BATCH, DEC_SEQ, D_MODEL)
    return ((y_prompt, y_sample)
            + tuple(stk(p_states, j) for j in range(9))
            + tuple(stk(s_states, j) for j in range(7)))
```
